```python
import math
import jax, jax.numpy as jnp
from jax import lax
import numpy as np

D_MODEL = 2048
BATCH = 4
SEQ = 4096
DEPTH = 2
DEC_BATCH = 1
DEC_SEQ = 16384
PAST_LEN = 128

N_BRANCH = 4
BRANCH_W = 512
QBLK = 128
EPS = 1e-6
HA = 4
DHA = 64
DVA = 2 * DHA
SGU_CHUNK = 128
SGU_GROUPS = 4
SGU_GW = BRANCH_W // SGU_GROUPS
LRU_W = 512
LRU_BLOCKS = 4
LRU_BW = LRU_W // LRU_BLOCKS
CONV_W = 4
CONV_LEFT = 1
LRU_C = 8.0
HD = 4
Q_LORA = 512
KV_LORA = 256
D_NOPE = 128
D_ROPE = 64
DV_D = 128
ROPE_THETA = 10000.0
N_EXPERTS = 64
TOP_K = 8
N_GROUPS = 8
TOPK_GROUPS = 4
D_EXPERT = 512
D_SHARED = 512
ROUTE_SCALE = 2.5
MOE_BLK = 128
A_Q = HA * 2 * DHA
A_K = HA * 2 * DHA
A_V = HA * DVA
B_UV = 2 * BRANCH_W
C_XG = 2 * LRU_W
D_CQ = Q_LORA
D_CKV = KV_LORA
D_KR = D_ROPE
GATE_COLS = N_BRANCH * D_MODEL
IN_COLS = A_Q + A_K + A_V + B_UV + C_XG + D_CQ + D_CKV + D_KR + GATE_COLS

kernel_name = 'hybrid_gated_parallel_encoder'


def _rmsnorm(x, g):
    xf = x.astype(jnp.float32)
    y = xf * lax.rsqrt(jnp.mean(xf * xf, axis=-1, keepdims=True) + EPS)
    return (y * g.astype(jnp.float32)).astype(x.dtype)


def _layernorm(x, g, b):
    xf = x.astype(jnp.float32)
    mu = jnp.mean(xf, axis=-1, keepdims=True)
    var = jnp.mean(jnp.square(xf - mu), axis=-1, keepdims=True)
    y = (xf - mu) * lax.rsqrt(var + EPS) * g.astype(jnp.float32) + b.astype(jnp.float32)
    return y.astype(x.dtype)


def _rope(x, cos, sin):
    half = x.shape[-1] // 2
    x1, x2 = x[..., :half], x[..., half:]
    cos = cos.astype(x.dtype)
    sin = sin.astype(x.dtype)
    return jnp.concatenate([x1 * cos - x2 * sin, x1 * sin + x2 * cos], axis=-1)


def _diff_attention(q, k, v, lam, slopes):
    B, S = q.shape[:2]
    nb = S // QBLK
    qb = q.reshape(B, nb, QBLK, HA, 2, DHA).transpose(1, 0, 2, 3, 4, 5)
    key_pos = jnp.arange(S)
    scale = DHA ** -0.5

    def one(args):
        q_i, i = args
        s = jnp.einsum('bqhmd,bkhmd->mbhqk', q_i, k).astype(jnp.float32) * scale
        q_pos = i * QBLK + jnp.arange(QBLK)
        dist = jnp.abs(q_pos[:, None] - key_pos[None, :]).astype(jnp.float32)
        s = s - slopes[:, None, None] * dist
        pr = jax.nn.softmax(s, axis=-1)
        a = (pr[0] - lam * pr[1]).astype(v.dtype)
        return jnp.einsum('bhqk,bkhd->bqhd', a, v)

    o = lax.map(one, (qb, jnp.arange(nb)))
    return o.transpose(1, 0, 2, 3, 4).reshape(B, S, HA, DVA)


def _mla_attention(q_nope, q_rope, k_nope, k_rope, v):
    B, S = q_nope.shape[:2]
    nb = S // QBLK
    qn = q_nope.reshape(B, nb, QBLK, HD, D_NOPE).transpose(1, 0, 2, 3, 4)
    qr = q_rope.reshape(B, nb, QBLK, HD, D_ROPE).transpose(1, 0, 2, 3, 4)
    scale = (D_NOPE + D_ROPE) ** -0.5

    def one(args):
        qn_i, qr_i = args
        s = (jnp.einsum('bqhd,bkhd->bhqk', qn_i, k_nope)
             + jnp.einsum('bqhr,bkr->bhqk', qr_i, k_rope)).astype(jnp.float32) * scale
        pr = jax.nn.softmax(s, axis=-1).astype(v.dtype)
        return jnp.einsum('bhqk,bkhd->bqhd', pr, v)

    o = lax.map(one, (qn, qr))
    return o.transpose(1, 0, 2, 3, 4).reshape(B, S, HD * DV_D)


def _spatial_gating(uv, ln_g, ln_b, w_s, b_s):
    B, S = uv.shape[:2]
    z = jax.nn.gelu(uv)
    u, v = z[..., :BRANCH_W], z[..., BRANCH_W:]
    v = _layernorm(v, ln_g, ln_b)
    nc = S // SGU_CHUNK
    vg = v.reshape(B, nc, SGU_CHUNK, SGU_GROUPS, SGU_GW)
    vm = jnp.einsum('gts,bnsgc->bntgc', w_s, vg) + b_s.T[None, None, :, :, None]
    return u * vm.reshape(B, S, BRANCH_W)


def _lin_rec_combine(left, right):
    a1, b1 = left
    a2, b2 = right
    return a1 * a2, a2 * b1 + b2


def _rglru_branch(xg, conv_w, conv_b, wr, br, wi, bi, lam):
    B, S = xg.shape[:2]
    xc, gate = xg[..., :LRU_W], xg[..., LRU_W:]
    xp = jnp.pad(xc, ((0, 0), (CONV_LEFT, CONV_W - 1 - CONV_LEFT), (0, 0)))
    xc = sum(xp[:, j:j + S] * conv_w[j] for j in range(CONV_W)) + conv_b
    xb = xc.reshape(B, S, LRU_BLOCKS, LRU_BW)

    def direction(d, reverse):
        r = jax.nn.sigmoid(jnp.einsum('bsgc,gce->bsge', xb, wr[d]).reshape(B, S, LRU_W) + br[d])
        i = jax.nn.sigmoid(jnp.einsum('bsgc,gce->bsge', xb, wi[d]).reshape(B, S, LRU_W) + bi[d])
        log_a = (-LRU_C * r.astype(jnp.float32)) * jax.nn.softplus(-lam[d].astype(jnp.float32))
        a = jnp.exp(log_a)
        u = jnp.sqrt(-jnp.expm1(2.0 * log_a)) * (i * xc).astype(jnp.float32)
        _, h = lax.associative_scan(_lin_rec_combine, (a, u), reverse=reverse, axis=1)
        return h

    h = direction(0, False) + direction(1, True)
    return jax.nn.gelu(gate) * h.astype(xg.dtype)


def _token_mixer(h, l, p):
    B, S, _ = h.shape
    proj = h @ p['w_in'][l]
    sizes = [A_Q, A_K, A_V, B_UV, C_XG, D_CQ, D_CKV, D_KR, GATE_COLS]
    qa, ka, va, uv, xg, cq, ckv, kr, gl = jnp.split(proj, np.cumsum(sizes)[:-1].tolist(), axis=-1)

    lam_init = 0.8 - 0.6 * math.exp(-0.3 * l)
    lam = (jnp.exp(jnp.sum(p['lam_q1'][l].astype(jnp.float32) * p['lam_k1'][l].astype(jnp.float32)))
           - jnp.exp(jnp.sum(p['lam_q2'][l].astype(jnp.float32) * p['lam_k2'][l].astype(jnp.float32)))
           + lam_init)
    slopes = jnp.asarray(2.0 ** (-8.0 * np.arange(1, HA + 1) / HA), jnp.float32)
    oa = _diff_attention(qa.reshape(B, S, HA, 2, DHA), ka.reshape(B, S, HA, 2, DHA),
                         va.reshape(B, S, HA, DVA), lam, slopes)
    ya = (_rmsnorm(oa, p['subln_g'][l]) * (1.0 - lam_init)).reshape(B, S, BRANCH_W)

    yb = _spatial_gating(uv, p['sgu_ln_g'][l], p['sgu_ln_b'][l], p['sgu_w'][l], p['sgu_b'][l])

    yc = _rglru_branch(xg, p['conv_w'][l], p['conv_b'][l], p['rg_wr'][l], p['rg_br'][l],
                       p['rg_wi'][l], p['rg_bi'][l], p['rg_lam'][l])

    q = (_rmsnorm(cq, p['q_norm_g'][l]) @ p['w_uq'][l]).reshape(B, S, HD, D_NOPE + D_ROPE)
    kv = (_rmsnorm(ckv, p['kv_norm_g'][l]) @ p['w_ukv'][l]).reshape(B, S, HD, D_NOPE + DV_D)
    inv_freq = ROPE_THETA ** (-jnp.arange(0, D_ROPE, 2, dtype=jnp.float32) / D_ROPE)
    ang = jnp.arange(S, dtype=jnp.float32)[:, None] * inv_freq[None, :]
    cos, sin = jnp.cos(ang), jnp.sin(ang)
    q_rope = _rope(q[..., D_NOPE:], cos[:, None, :], sin[:, None, :])
    k_rope = _rope(kr, cos, sin)
    yd = _mla_attention(q[..., :D_NOPE], q_rope, kv[..., :D_NOPE], k_rope, kv[..., D_NOPE:])

    gates = jax.nn.sigmoid(gl.reshape(B, S, N_BRANCH, D_MODEL))
    w_br = p['w_branch'][l]
    merged = sum(gates[:, :, k] * (y_k @ w_br[k]) for k, y_k in enumerate((ya, yb, yc, yd)))
    return merged @ p['w_out'][l]


def _swiglu(x, wg, wu, wd):
    return (jax.nn.silu(x @ wg) * (x @ wu)) @ wd


def _routed_experts(x, eidx, w, e_wg, e_wu, e_wd):
    n, D = x.shape
    nk = n * TOP_K
    flat_e = eidx.reshape(-1)
    flat_w = w.reshape(-1)
    flat_tok = jnp.arange(nk, dtype=jnp.int32) // TOP_K
    order = jnp.argsort(flat_e, stable=True)
    se = flat_e[order]
    counts = jnp.bincount(flat_e, length=N_EXPERTS)
    pcounts = ((counts + MOE_BLK - 1) // MOE_BLK) * MOE_BLK
    pend = jnp.cumsum(pcounts)
    pstart = pend - pcounts
    start = jnp.cumsum(counts) - counts
    dest = pstart[se] + jnp.arange(nk) - start[se]
    nblk = (nk + N_EXPERTS * (MOE_BLK - 1) + MOE_BLK - 1) // MOE_BLK
    buf_tok = jnp.full((nblk * MOE_BLK,), n, jnp.int32).at[dest].set(flat_tok[order])
    buf_w = jnp.zeros((nblk * MOE_BLK,), x.dtype).at[dest].set(flat_w[order].astype(x.dtype))
    blk_e = jnp.minimum(jnp.searchsorted(pend, jnp.arange(nblk) * MOE_BLK, side='right'), N_EXPERTS - 1)
    x_pad = jnp.concatenate([x, jnp.zeros((1, D), x.dtype)], axis=0)

    def one(args):
        tok, e, wt = args
        xb = x_pad[tok]
        hb = jax.nn.silu(xb @ e_wg[e]) * (xb @ e_wu[e])
        return (hb @ e_wd[e]) * wt[:, None]

    out = lax.map(one, (buf_tok.reshape(nblk, MOE_BLK), blk_e, buf_w.reshape(nblk, MOE_BLK)))
    return jax.ops.segment_sum(out.reshape(-1, D), buf_tok, num_segments=n + 1)[:n]


def _moe(h, l, p):
    B, S, D = h.shape
    x = h.reshape(B * S, D)
    n = B * S
    s = jax.nn.sigmoid((x @ p['router_w'][l]).astype(jnp.float32))
    sel = s + p['router_bias'][l].astype(jnp.float32)
    grp_score = lax.top_k(sel.reshape(n, N_GROUPS, N_EXPERTS // N_GROUPS), 2)[0].sum(-1)
    _, gidx = lax.top_k(grp_score, TOPK_GROUPS)
    gmask = jax.nn.one_hot(gidx, N_GROUPS, dtype=jnp.float32).sum(1)
    emask = jnp.repeat(gmask, N_EXPERTS // N_GROUPS, axis=1) > 0
    _, eidx = lax.top_k(jnp.where(emask, sel, -jnp.inf), TOP_K)
    w = jnp.take_along_axis(s, eidx, axis=1)
    w = w / jnp.sum(w, axis=-1, keepdims=True) * ROUTE_SCALE
    routed = _routed_experts(x, eidx, w, p['exp_wg'][l], p['exp_wu'][l], p['exp_wd'][l])
    shared = _swiglu(x, p['sh_wg'][l], p['sh_wu'][l], p['sh_wd'][l])
    return (routed + shared).reshape(B, S, D)


def _trunk(x, c, p):
    for l in range(DEPTH):
        mod = jax.nn.silu(c) @ p['ada_w'][l] + p['ada_b'][l]
        sh1, sc1, g1, sh2, sc2, g2 = jnp.split(mod[:, None, :], 6, axis=-1)
        h = _rmsnorm(x, p['norm1_g'][l]) * (1 + sc1) + sh1
        x = x + g1 * _token_mixer(h, l, p)
        h = _rmsnorm(x, p['norm2_g'][l]) * (1 + sc2) + sh2
        x = x + g2 * _moe(h, l, p)
    return _rmsnorm(x, p['final_g'])


def setup_inputs(seed: int = 0) -> dict:
    key = jax.random.key(seed)
    ks = iter(jax.random.split(key, 64))
    f32 = jnp.float32

    def nrm(shape, scale):
        return jax.random.normal(next(ks), shape, f32) * scale

    def gain(shape):
        return 1.0 + nrm(shape, 0.02)

    D = D_MODEL
    u = jax.random.uniform(next(ks), (DEPTH, 2, LRU_W), f32, minval=0.9, maxval=0.999)
    pa = u ** (1.0 / LRU_C)
    rg_lam = jnp.log(pa) - jnp.log1p(-pa)
    return {
        'x_prompt': nrm((BATCH, SEQ, D), 1.0),
        'x_sample': nrm((DEC_BATCH, DEC_SEQ, D), 1.0),
        'c_prompt': nrm((BATCH, D), 1.0),
        'c_sample': nrm((DEC_BATCH, D), 1.0),
        'ada_w': nrm((DEPTH, D, 6 * D), 0.2 * D ** -0.5),
        'ada_b': nrm((DEPTH, 6 * D), 0.02),
        'norm1_g': gain((DEPTH, D)),
        'norm2_g': gain((DEPTH, D)),
        'w_in': nrm((DEPTH, D, IN_COLS), D ** -0.5),
        'lam_q1': nrm((DEPTH, DHA), 0.1),
        'lam_k1': nrm((DEPTH, DHA), 0.1),
        'lam_q2': nrm((DEPTH, DHA), 0.1),
        'lam_k2': nrm((DEPTH, DHA), 0.1),
        'subln_g': gain((DEPTH, DVA)),
        'sgu_ln_g': gain((DEPTH, BRANCH_W)),
        'sgu_ln_b': nrm((DEPTH, BRANCH_W), 0.02),
        'sgu_w': nrm((DEPTH, SGU_GROUPS, SGU_CHUNK, SGU_CHUNK), SGU_CHUNK ** -0.5),
        'sgu_b': gain((DEPTH, SGU_GROUPS, SGU_CHUNK)),
        'conv_w': nrm((DEPTH, CONV_W, LRU_W), CONV_W ** -0.5),
        'conv_b': nrm((DEPTH, LRU_W), 0.02),
        'rg_wr': nrm((DEPTH, 2, LRU_BLOCKS, LRU_BW, LRU_BW), LRU_BW ** -0.5),
        'rg_br': nrm((DEPTH, 2, LRU_W), 0.02),
        'rg_wi': nrm((DEPTH, 2, LRU_BLOCKS, LRU_BW, LRU_BW), LRU_BW ** -0.5),
        'rg_bi': nrm((DEPTH, 2, LRU_W), 0.02),
        'rg_lam': rg_lam,
        'q_norm_g': gain((DEPTH, Q_LORA)),
        'kv_norm_g': gain((DEPTH, KV_LORA)),
        'w_uq': nrm((DEPTH, Q_LORA, HD * (D_NOPE + D_ROPE)), Q_LORA ** -0.5),
        'w_ukv': nrm((DEPTH, KV_LORA, HD * (D_NOPE + DV_D)), KV_LORA ** -0.5),
        'w_branch': nrm((DEPTH, N_BRANCH, BRANCH_W, D), BRANCH_W ** -0.5),
        'w_out': nrm((DEPTH, D, D), D ** -0.5),
        'router_w': nrm((DEPTH, D, N_EXPERTS), D ** -0.5),
        'router_bias': nrm((DEPTH, N_EXPERTS), 0.01),
        'exp_wg': nrm((DEPTH, N_EXPERTS, D, D_EXPERT), D ** -0.5),
        'exp_wu': nrm((DEPTH, N_EXPERTS, D, D_EXPERT), D ** -0.5),
        'exp_wd': nrm((DEPTH, N_EXPERTS, D_EXPERT, D), D_EXPERT ** -0.5),
        'sh_wg': nrm((DEPTH, D, D_SHARED), D ** -0.5),
        'sh_wu': nrm((DEPTH, D, D_SHARED), D ** -0.5),
        'sh_wd': nrm((DEPTH, D_SHARED, D), D_SHARED ** -0.5),
        'final_g': gain((D,)),
    }


def reference(x_prompt, x_sample, c_prompt, c_sample, ada_w, ada_b, norm1_g, norm2_g, w_in,
              lam_q1, lam_k1, lam_q2, lam_k2, subln_g, sgu_ln_g, sgu_ln_b, sgu_w, sgu_b,
              conv_w, conv_b, rg_wr, rg_br, rg_wi, rg_bi, rg_lam, q_norm_g, kv_norm_g, w_uq, w_ukv,
              w_branch, w_out, router_w, router_bias, exp_wg, exp_wu, exp_wd, sh_wg, sh_wu, sh_wd,
              final_g):
    p = dict(ada_w=ada_w, ada_b=ada_b, norm1_g=norm1_g, norm2_g=norm2_g, w_in=w_in,
             lam_q1=lam_q1, lam_k1=lam_k1, lam_q2=lam_q2, lam_k2=lam_k2, subln_g=subln_g,
             sgu_ln_g=sgu_ln_g, sgu_ln_b=sgu_ln_b, sgu_w=sgu_w, sgu_b=sgu_b,
             conv_w=conv_w, conv_b=conv_b, rg_wr=rg_wr, rg_br=rg_br, rg_wi=rg_wi, rg_bi=rg_bi,
             rg_lam=rg_lam, q_norm_g=q_norm_g, kv_norm_g=kv_norm_g, w_uq=w_uq, w_ukv=w_ukv,
             w_branch=w_branch, w_out=w_out, router_w=router_w, router_bias=router_bias,
             exp_wg=exp_wg, exp_wu=exp_wu, exp_wd=exp_wd, sh_wg=sh_wg, sh_wu=sh_wu, sh_wd=sh_wd,
             final_g=final_g)
    y_prompt = _trunk(x_prompt, c_prompt, p)
    y_sample = _trunk(x_sample, c_sample, p)
    return (y_prompt, y_sample)
```

```python
import functools
import math

import numpy as np
import jax
import jax.numpy as jnp
from jax import lax
from jax.experimental import pallas as pl
from jax.experimental.pallas import tpu as pltpu

F32 = jnp.float32
BF16 = jnp.bfloat16
I32 = jnp.int32

EPS = 1e-6
LANE = 128
SUBLANE = 8
VMEM_LIMIT = 48 * 1024 * 1024

HA = 4
DHA = 64
DVA = 2 * DHA
BRANCH_W = 512
SGU_CHUNK = 128
SGU_GROUPS = 4
LRU_W = 512
LRU_BLOCKS = 4
LRU_BW = LRU_W // LRU_BLOCKS
LRU_C = 8.0
HD = 4
Q_LORA = 512
KV_LORA = 256
D_NOPE = 128
D_ROPE = 64
DV_D = 128
ROPE_THETA = 10000.0
N_EXPERTS = 64
TOP_K = 8
N_GROUPS = 8
TOPK_GROUPS = 4
GROUP_SZ = N_EXPERTS // N_GROUPS
ROUTE_SCALE = 2.5
MOE_BLK = 256
N_BRANCH = 4

R_UV = 0
R_XC = 1024
R_GATE = 1536
R_CQ = 2048
R_CKV = 2560
R_GL = 3072

NT_DIMS = (((1,), (1,)), ((), ()))


def _params(*sem):
    return pltpu.CompilerParams(dimension_semantics=sem, vmem_limit_bytes=VMEM_LIMIT)


def _ada_kernel(c_ref, w_ref, b_ref, o_ref):
    a = jax.nn.silu(c_ref[...]).astype(BF16)
    o_ref[...] = jnp.dot(a, w_ref[...].astype(BF16), preferred_element_type=F32) + b_ref[...]


def _ada_mod(c8, ada_w, ada_b):
    depth, d, n = ada_w.shape
    tn = 1024
    return pl.pallas_call(
        _ada_kernel,
        grid=(depth, n // tn),
        in_specs=[
            pl.BlockSpec((8, d), lambda l, j: (0, 0)),
            pl.BlockSpec((None, d, tn), lambda l, j: (l, 0, j)),
            pl.BlockSpec((None, 1, tn), lambda l, j: (l, 0, j)),
        ],
        out_specs=pl.BlockSpec((None, 8, tn), lambda l, j: (l, 0, j)),
        out_shape=jax.ShapeDtypeStruct((depth, 8, n), F32),
        compiler_params=_params("parallel", "parallel"),
    )(c8, ada_w, ada_b.reshape(depth, 1, n))


def _inproj_kernel(x_ref, g_ref, mod_ref, w_ref, o_ref, h_ref):
    @pl.when(pl.program_id(1) == 0)
    def _():
        x = x_ref[...]
        y = x * lax.rsqrt(jnp.mean(x * x, axis=-1, keepdims=True) + EPS) * g_ref[...]
        h_ref[...] = (y * (1.0 + mod_ref[0:1, :]) + mod_ref[1:2, :]).astype(BF16)

    o_ref[...] = jnp.dot(h_ref[...], w_ref[...], preferred_element_type=F32).astype(o_ref.dtype)


def _inproj(x, g, mod, w, out_dtype, seq):
    n, d = x.shape
    nc = w.shape[1]
    tm = min(1024, seq)
    tn = 512
    tps = seq // tm
    return pl.pallas_call(
        _inproj_kernel,
        grid=(n // tm, nc // tn),
        in_specs=[
            pl.BlockSpec((tm, d), lambda i, j: (i, 0)),
            pl.BlockSpec((1, d), lambda i, j: (0, 0)),
            pl.BlockSpec((None, 2, d), lambda i, j: (i // tps, 0, 0)),
            pl.BlockSpec((d, tn), lambda i, j: (0, j)),
        ],
        out_specs=pl.BlockSpec((tm, tn), lambda i, j: (i, j)),
        out_shape=jax.ShapeDtypeStruct((n, nc), out_dtype),
        scratch_shapes=[pltpu.VMEM((tm, d), BF16)],
        compiler_params=_params("parallel", "arbitrary"),
    )(x, g.reshape(1, d), mod, w)


def _diff_attn_kernel(par_ref, q_ref, k_ref, v_ref, g_ref, o_ref, m_ref, l_ref, acc_ref, *, tq, tk, nk):
    h = pl.program_id(1)
    qi = pl.program_id(2)
    ki = pl.program_id(3)

    @pl.when(ki == 0)
    def _():
        m_ref[...] = jnp.full(m_ref.shape, -jnp.inf, F32)
        l_ref[...] = jnp.zeros(l_ref.shape, F32)
        acc_ref[...] = jnp.zeros(acc_ref.shape, F32)

    q = q_ref[...]
    lane = lax.broadcasted_iota(I32, q.shape, 1)
    zero = jnp.zeros_like(q)
    qs = (jnp.where(lane < DHA, q, zero), jnp.where(lane < DHA, zero, q))
    k = k_ref[...]
    v = v_ref[...]
    row = lax.broadcasted_iota(I32, (tq, tk), 0)
    col = lax.broadcasted_iota(I32, (tq, tk), 1)
    dist = jnp.abs(row - col + (qi * tq - ki * tk)).astype(F32)
    bias = dist * par_ref[h]
    for mi in range(2):
        s = lax.dot_general(qs[mi], k, NT_DIMS, preferred_element_type=F32) - bias
        m_prev = m_ref[mi]
        m_new = jnp.maximum(m_prev, jnp.max(s, axis=-1, keepdims=True))
        alpha = jnp.exp(m_prev - m_new)
        p = jnp.exp(s - m_new)
        l_ref[mi] = alpha * l_ref[mi] + jnp.sum(p, axis=-1, keepdims=True)
        acc_ref[mi] = alpha * acc_ref[mi] + jnp.dot(p.astype(BF16), v, preferred_element_type=F32)
        m_ref[mi] = m_new

    @pl.when(ki == nk - 1)
    def _():
        lam = par_ref[HA]
        o = acc_ref[0] / l_ref[0] - lam * (acc_ref[1] / l_ref[1])
        y = o * lax.rsqrt(jnp.mean(o * o, axis=-1, keepdims=True) + EPS) * g_ref[...]
        o_ref[...] = (y * par_ref[HA + 1]).astype(o_ref.dtype)


def _diff_attn(att, par, subln_g, n_seq, seq):
    n = att.shape[0]
    tq = min(512, seq)
    tk = min(512, seq)
    nq, nk = seq // tq, seq // tk
    kern = functools.partial(_diff_attn_kernel, tq=tq, tk=tk, nk=nk)
    return pl.pallas_call(
        kern,
        grid=(n_seq, HA, nq, nk),
        in_specs=[
            pl.BlockSpec(memory_space=pltpu.SMEM),
            pl.BlockSpec((tq, LANE), lambda b, h, qi, ki: (b * nq + qi, h)),
            pl.BlockSpec((tk, LANE), lambda b, h, qi, ki: (b * nk + ki, HA + h)),
            pl.BlockSpec((tk, LANE), lambda b, h, qi, ki: (b * nk + ki, 2 * HA + h)),
            pl.BlockSpec((1, DVA), lambda b, h, qi, ki: (0, 0)),
        ],
        out_specs=pl.BlockSpec((tq, LANE), lambda b, h, qi, ki: (b * nq + qi, h)),
        out_shape=jax.ShapeDtypeStruct((n, HA * DVA), BF16),
        scratch_shapes=[
            pltpu.VMEM((2, tq, 1), F32),
            pltpu.VMEM((2, tq, 1), F32),
            pltpu.VMEM((2, tq, DVA), F32),
        ],
        compiler_params=_params("parallel", "parallel", "parallel", "arbitrary"),
    )(par, att, att, att, subln_g.reshape(1, DVA))


def _mla_prep_kernel(cq_ref, ckv_ref, tab_ref, gq_ref, gkv_ref, wuq_ref, wukv_ref, q_ref, k_ref, v_ref, *, scale):
    cq = cq_ref[...]
    cqn = (cq * lax.rsqrt(jnp.mean(cq * cq, axis=-1, keepdims=True) + EPS) * gq_ref[...]).astype(BF16)
    c = ckv_ref[...]
    ckv = c[:, :KV_LORA]
    ckvn = (ckv * lax.rsqrt(jnp.mean(ckv * ckv, axis=-1, keepdims=True) + EPS) * gkv_ref[...]).astype(BF16)
    tab = tab_ref[...]
    lane = lax.broadcasted_iota(I32, tab.shape, 1)

    def rope(pair):
        pr = pair * tab
        return jnp.where(lane < D_ROPE, pr + pltpu.roll(pr, D_ROPE, 1), 0.0)

    kr = rope(c[:, KV_LORA:KV_LORA + LANE]).astype(BF16)
    qf = jnp.dot(cqn, wuq_ref[...], preferred_element_type=F32)
    kvf = jnp.dot(ckvn, wukv_ref[...], preferred_element_type=F32)
    for h in range(HD):
        b0 = h * 2 * LANE
        q_ref[:, b0:b0 + LANE] = (qf[:, b0:b0 + LANE] * scale).astype(BF16)
        q_ref[:, b0 + LANE:b0 + 2 * LANE] = (rope(qf[:, b0 + LANE:b0 + 2 * LANE]) * scale).astype(BF16)
        k_ref[:, b0:b0 + LANE] = kvf[:, b0:b0 + LANE].astype(BF16)
        k_ref[:, b0 + LANE:b0 + 2 * LANE] = kr
        v_ref[:, h * LANE:(h + 1) * LANE] = kvf[:, b0 + LANE:b0 + 2 * LANE].astype(BF16)


def _mla_prep(rest, tab, gq, gkv, wuq, wukv, seq):
    n = rest.shape[0]
    tm = min(512, seq)
    scale = (D_NOPE + D_ROPE) ** -0.5
    kern = functools.partial(_mla_prep_kernel, scale=scale)
    wq = HD * 2 * LANE
    return pl.pallas_call(
        kern,
        grid=(n // tm,),
        in_specs=[
            pl.BlockSpec((tm, Q_LORA), lambda i: (i, R_CQ // Q_LORA)),
            pl.BlockSpec((tm, 512), lambda i: (i, R_CKV // 512)),
            pl.BlockSpec((tm, LANE), lambda i: (i, 0)),
            pl.BlockSpec((1, Q_LORA), lambda i: (0, 0)),
            pl.BlockSpec((1, KV_LORA), lambda i: (0, 0)),
            pl.BlockSpec((Q_LORA, wq), lambda i: (0, 0)),
            pl.BlockSpec((KV_LORA, wq), lambda i: (0, 0)),
        ],
        out_specs=[
            pl.BlockSpec((tm, wq), lambda i: (i, 0)),
            pl.BlockSpec((tm, wq), lambda i: (i, 0)),
            pl.BlockSpec((tm, HD * DV_D), lambda i: (i, 0)),
        ],
        out_shape=[
            jax.ShapeDtypeStruct((n, wq), BF16),
            jax.ShapeDtypeStruct((n, wq), BF16),
            jax.ShapeDtypeStruct((n, HD * DV_D), BF16),
        ],
        compiler_params=_params("parallel"),
    )(rest, rest, tab, gq.reshape(1, -1), gkv.reshape(1, -1), wuq, wukv)


def _mla_attn_kernel(q_ref, k_ref, v_ref, o_ref, m_ref, l_ref, acc_ref, *, nk):
    ki = pl.program_id(3)

    @pl.when(ki == 0)
    def _():
        m_ref[...] = jnp.full(m_ref.shape, -jnp.inf, F32)
        l_ref[...] = jnp.zeros(l_ref.shape, F32)
        acc_ref[...] = jnp.zeros(acc_ref.shape, F32)

    s = lax.dot_general(q_ref[...], k_ref[...], NT_DIMS, preferred_element_type=F32)
    m_prev = m_ref[...]
    m_new = jnp.maximum(m_prev, jnp.max(s, axis=-1, keepdims=True))
    alpha = jnp.exp(m_prev - m_new)
    p = jnp.exp(s - m_new)
    l_ref[...] = alpha * l_ref[...] + jnp.sum(p, axis=-1, keepdims=True)
    acc_ref[...] = alpha * acc_ref[...] + jnp.dot(p.astype(BF16), v_ref[...], preferred_element_type=F32)
    m_ref[...] = m_new

    @pl.when(ki == nk - 1)
    def _():
        o_ref[...] = (acc_ref[...] / l_ref[...]).astype(o_ref.dtype)


def _mla_attn(q, k, v, n_seq, seq):
    n = q.shape[0]
    tq = min(512, seq)
    tk = min(512, seq)
    nq, nk = seq // tq, seq // tk
    kern = functools.partial(_mla_attn_kernel, nk=nk)
    return pl.pallas_call(
        kern,
        grid=(n_seq, HD, nq, nk),
        in_specs=[
            pl.BlockSpec((tq, 2 * LANE), lambda b, h, qi, ki: (b * nq + qi, h)),
            pl.BlockSpec((tk, 2 * LANE), lambda b, h, qi, ki: (b * nk + ki, h)),
            pl.BlockSpec((tk, DV_D), lambda b, h, qi, ki: (b * nk + ki, h)),
        ],
        out_specs=pl.BlockSpec((tq, DV_D), lambda b, h, qi, ki: (b * nq + qi, h)),
        out_shape=jax.ShapeDtypeStruct((n, HD * DV_D), BF16),
        scratch_shapes=[
            pltpu.VMEM((tq, 1), F32),
            pltpu.VMEM((tq, 1), F32),
            pltpu.VMEM((tq, DV_D), F32),
        ],
        compiler_params=_params("parallel", "parallel", "parallel", "arbitrary"),
    )(q, k, v)


def _sgu_kernel(uv_ref, g_ref, b_ref, ws_ref, bs_ref, o_ref, *, tm):
    z = jax.nn.gelu(uv_ref[...])
    u = z[:, :BRANCH_W]
    v = z[:, BRANCH_W:]
    mu = jnp.mean(v, axis=-1, keepdims=True)
    vc = v - mu
    var = jnp.mean(vc * vc, axis=-1, keepdims=True)
    vn = (vc * lax.rsqrt(var + EPS) * g_ref[...] + b_ref[...]).astype(BF16)
    for c in range(tm // SGU_CHUNK):
        r0 = c * SGU_CHUNK
        for g in range(SGU_GROUPS):
            c0 = g * LANE
            vm = jnp.dot(ws_ref[g], vn[r0:r0 + SGU_CHUNK, c0:c0 + LANE], preferred_element_type=F32)
            vm = vm + bs_ref[:, c0:c0 + LANE]
            o_ref[r0:r0 + SGU_CHUNK, c0:c0 + LANE] = (u[r0:r0 + SGU_CHUNK, c0:c0 + LANE] * vm).astype(o_ref.dtype)


def _sgu(rest, ln_g, ln_b, ws, bs, seq):
    n = rest.shape[0]
    tm = min(512, seq)
    kern = functools.partial(_sgu_kernel, tm=tm)
    return pl.pallas_call(
        kern,
        grid=(n // tm,),
        in_specs=[
            pl.BlockSpec((tm, 2 * BRANCH_W), lambda i: (i, R_UV // (2 * BRANCH_W))),
            pl.BlockSpec((1, BRANCH_W), lambda i: (0, 0)),
            pl.BlockSpec((1, BRANCH_W), lambda i: (0, 0)),
            pl.BlockSpec((SGU_GROUPS, SGU_CHUNK, SGU_CHUNK), lambda i: (0, 0, 0)),
            pl.BlockSpec((SGU_CHUNK, BRANCH_W), lambda i: (0, 0)),
        ],
        out_specs=pl.BlockSpec((tm, BRANCH_W), lambda i: (i, 0)),
        out_shape=jax.ShapeDtypeStruct((n, BRANCH_W), BF16),
        compiler_params=_params("parallel"),
    )(rest, ln_g.reshape(1, -1), ln_b.reshape(1, -1), ws, bs)


def _lru_kernel(xf_ref, xfp_ref, xfn_ref, xr_ref, xrp_ref, xrn_ref, cw_ref, cb_ref, wr_ref, br_ref, wi_ref,
                bi_ref, lam_ref, hf_ref, hr_ref, af_ref, uf_ref, ar_ref, ur_ref, hc_ref, *, tt, tps):
    j = pl.program_id(0) % tps
    rowi = lax.broadcasted_iota(I32, (tt, LRU_W), 0)

    def gates(x_ref, xp_ref, xn_ref, at_start, at_end, d, a_ref, u_ref):
        x = x_ref[...]
        prev = jnp.where(at_start, 0.0, xp_ref[SUBLANE - 1:SUBLANE, :])
        nxt = jnp.where(at_end, 0.0, xn_ref[0:2, :])
        xm1 = jnp.where(rowi == 0, prev, pltpu.roll(x, 1, 0))
        xp1 = jnp.where(rowi == tt - 1, nxt[0:1, :], pltpu.roll(x, tt - 1, 0))
        xp2 = jnp.where(rowi == tt - 2, nxt[0:1, :],
                        jnp.where(rowi == tt - 1, nxt[1:2, :], pltpu.roll(x, tt - 2, 0)))
        xc = xm1 * cw_ref[0:1, :] + x * cw_ref[1:2, :] + xp1 * cw_ref[2:3, :] + xp2 * cw_ref[3:4, :] + cb_ref[...]
        xcb = xc.astype(BF16)

        def blockdiag(w_ref):
            return jnp.concatenate(
                [jnp.dot(xcb[:, g * LRU_BW:(g + 1) * LRU_BW], w_ref[d, g], preferred_element_type=F32)
                 for g in range(LRU_BLOCKS)], axis=1)

        r = jax.nn.sigmoid(blockdiag(wr_ref) + br_ref[d:d + 1, :])
        ig = jax.nn.sigmoid(blockdiag(wi_ref) + bi_ref[d:d + 1, :])
        log_a = (-LRU_C * r) * jax.nn.softplus(-lam_ref[d:d + 1, :])
        a = jnp.exp(log_a)
        u = jnp.sqrt(jnp.tanh(-log_a) * (1.0 + a * a)) * (ig * xc)
        a_ref[...] = a
        u_ref[...] = u

    gates(xf_ref, xfp_ref, xfn_ref, j == 0, j == tps - 1, 0, af_ref, uf_ref)
    gates(xr_ref, xrp_ref, xrn_ref, j == tps - 1, j == 0, 1, ar_ref, ur_ref)

    @pl.when(j == 0)
    def _():
        hc_ref[...] = jnp.zeros(hc_ref.shape, F32)

    ng = tt // SUBLANE
    sub = lax.broadcasted_iota(I32, (SUBLANE, LRU_W), 0)

    def body(g, carry):
        hf_prev, hr_prev = carry
        r0 = pl.multiple_of(g * SUBLANE, SUBLANE)
        a = af_ref[pl.ds(r0, SUBLANE), :]
        u = uf_ref[pl.ds(r0, SUBLANE), :]
        for s in (1, 2, 4):
            a_s = jnp.where(sub >= s, pltpu.roll(a, s, 0), 1.0)
            u_s = jnp.where(sub >= s, pltpu.roll(u, s, 0), 0.0)
            u = a * u_s + u
            a = a * a_s
        h = a * hf_prev + u
        hf_ref[pl.ds(r0, SUBLANE), :] = h
        hf_new = jnp.broadcast_to(h[SUBLANE - 1:SUBLANE, :], (SUBLANE, LRU_W))

        r1 = pl.multiple_of((ng - 1 - g) * SUBLANE, SUBLANE)
        a = ar_ref[pl.ds(r1, SUBLANE), :]
        u = ur_ref[pl.ds(r1, SUBLANE), :]
        for s in (1, 2, 4):
            a_s = jnp.where(sub < SUBLANE - s, pltpu.roll(a, SUBLANE - s, 0), 1.0)
            u_s = jnp.where(sub < SUBLANE - s, pltpu.roll(u, SUBLANE - s, 0), 0.0)
            u = a * u_s + u
            a = a * a_s
        h = a * hr_prev + u
        hr_ref[pl.ds(r1, SUBLANE), :] = h
        hr_new = jnp.broadcast_to(h[0:1, :], (SUBLANE, LRU_W))
        return hf_new, hr_new

    hf_c, hr_c = lax.fori_loop(0, ng, body, (hc_ref[0], hc_ref[1]))
    hc_ref[0] = hf_c
    hc_ref[1] = hr_c


def _lru(rest, cw, cb, wr, br, wi, bi, lam, seq):
    n = rest.shape[0]
    tt = min(512, seq)
    tps = seq // tt
    nt = n // tt
    t8 = tt // SUBLANE
    last8 = n // SUBLANE - 1
    cblk = R_XC // LRU_W

    def rev(i):
        return (i // tps) * tps + (tps - 1 - i % tps)

    kern = functools.partial(_lru_kernel, tt=tt, tps=tps)
    full = lambda shape: pl.BlockSpec(shape, lambda i: (0,) * len(shape))
    return pl.pallas_call(
        kern,
        grid=(nt,),
        in_specs=[
            pl.BlockSpec((tt, LRU_W), lambda i: (i, cblk)),
            pl.BlockSpec((SUBLANE, LRU_W), lambda i: (jnp.maximum(i * t8 - 1, 0), cblk)),
            pl.BlockSpec((SUBLANE, LRU_W), lambda i: (jnp.minimum((i + 1) * t8, last8), cblk)),
            pl.BlockSpec((tt, LRU_W), lambda i: (rev(i), cblk)),
            pl.BlockSpec((SUBLANE, LRU_W), lambda i: (jnp.maximum(rev(i) * t8 - 1, 0), cblk)),
            pl.BlockSpec((SUBLANE, LRU_W), lambda i: (jnp.minimum((rev(i) + 1) * t8, last8), cblk)),
            full((4, LRU_W)),
            full((1, LRU_W)),
            full((2, LRU_BLOCKS, LRU_BW, LRU_BW)),
            full((2, LRU_W)),
            full((2, LRU_BLOCKS, LRU_BW, LRU_BW)),
            full((2, LRU_W)),
            full((2, LRU_W)),
        ],
        out_specs=[
            pl.BlockSpec((tt, LRU_W), lambda i: (i, 0)),
            pl.BlockSpec((tt, LRU_W), lambda i: (rev(i), 0)),
        ],
        out_shape=[jax.ShapeDtypeStruct((n, LRU_W), F32), jax.ShapeDtypeStruct((n, LRU_W), F32)],
        scratch_shapes=[pltpu.VMEM((tt, LRU_W), F32)] * 4 + [pltpu.VMEM((2, SUBLANE, LRU_W), F32)],
        compiler_params=_params("arbitrary"),
    )(rest, rest, rest, rest, rest, rest, cw, cb.reshape(1, -1), wr, br, wi, bi, lam)


def _merge_kernel(ya_ref, yb_ref, hf_ref, hr_ref, gate_ref, yd_ref, wbr_ref, g0_ref, g1_ref, g2_ref, g3_ref, o_ref):
    yc = (jax.nn.gelu(gate_ref[...]) * (hf_ref[...] + hr_ref[...])).astype(BF16)
    ys = (ya_ref[...], yb_ref[...], yc, yd_ref[...])
    gls = (g0_ref, g1_ref, g2_ref, g3_ref)
    acc = None
    for k in range(N_BRANCH):
        t = jax.nn.sigmoid(gls[k][...]) * jnp.dot(ys[k], wbr_ref[k], preferred_element_type=F32)
        acc = t if acc is None else acc + t
    o_ref[...] = acc.astype(o_ref.dtype)


def _merge(ya, yb, hf, hr, rest, yd, wbr, seq):
    n = ya.shape[0]
    d = wbr.shape[-1]
    tm = min(512, seq)
    tn = 512
    row = lambda w: pl.BlockSpec((tm, w), lambda i, j: (i, 0))

    def gl_spec(k):
        base = (R_GL + k * d) // tn
        return pl.BlockSpec((tm, tn), lambda i, j: (i, base + j))

    return pl.pallas_call(
        _merge_kernel,
        grid=(n // tm, d // tn),
        in_specs=[
            row(BRANCH_W), row(BRANCH_W), row(LRU_W), row(LRU_W),
            pl.BlockSpec((tm, LRU_W), lambda i, j: (i, R_GATE // LRU_W)),
            row(BRANCH_W),
            pl.BlockSpec((N_BRANCH, BRANCH_W, tn), lambda i, j: (0, 0, j)),
            gl_spec(0), gl_spec(1), gl_spec(2), gl_spec(3),
        ],
        out_specs=pl.BlockSpec((tm, tn), lambda i, j: (i, j)),
        out_shape=jax.ShapeDtypeStruct((n, d), BF16),
        compiler_params=_params("parallel", "parallel"),
    )(ya, yb, hf, hr, rest, yd, wbr, rest, rest, rest, rest)


def _outproj_kernel(m_ref, w_ref, x_ref, g_ref, o_ref):
    o_ref[...] = x_ref[...] + g_ref[...] * jnp.dot(m_ref[...], w_ref[...], preferred_element_type=F32)


def _outproj(merged, w, x, g1, seq):
    n, d = x.shape
    tm = min(512, seq)
    tps = seq // tm
    return pl.pallas_call(
        _outproj_kernel,
        grid=(n // tm,),
        in_specs=[
            pl.BlockSpec((tm, d), lambda i: (i, 0)),
            pl.BlockSpec((d, d), lambda i: (0, 0)),
            pl.BlockSpec((tm, d), lambda i: (i, 0)),
            pl.BlockSpec((None, 1, d), lambda i: (i // tps, 0, 0)),
        ],
        out_specs=pl.BlockSpec((tm, d), lambda i: (i, 0)),
        out_shape=jax.ShapeDtypeStruct((n, d), F32),
        compiler_params=_params("parallel"),
    )(merged, w, x, g1)


def _router_kernel(x_ref, g_ref, mod_ref, wh_ref, wl_ref, rb_ref, tri_ref,
                   h2_ref, h2g_ref, e_ref, w_ref, r_ref, cnt_ref, run_ref, *, tm, d):
    @pl.when(pl.program_id(0) == 0)
    def _():
        run_ref[...] = jnp.zeros(run_ref.shape, F32)

    x = x_ref[...]
    y = x * lax.rsqrt(jnp.mean(x * x, axis=-1, keepdims=True) + EPS) * g_ref[...]
    h = y * (1.0 + mod_ref[0:1, :]) + mod_ref[1:2, :]
    hb = h.astype(BF16)
    h2_ref[...] = hb
    slabs = d // LANE
    for s in range(slabs):
        h2g_ref[pl.ds(s, tm, stride=slabs), :] = h[:, s * LANE:(s + 1) * LANE]

    hl = (h - hb.astype(F32)).astype(BF16)
    logits = (lax.dot_general(wh_ref[...], hb, NT_DIMS, preferred_element_type=F32)
              + lax.dot_general(wl_ref[...], hb, NT_DIMS, preferred_element_type=F32)
              + lax.dot_general(wh_ref[...], hl, NT_DIMS, preferred_element_type=F32))
    sc = jax.nn.sigmoid(logits)
    sel = sc + rb_ref[...]

    neg = -jnp.inf
    i8 = lax.broadcasted_iota(I32, (GROUP_SZ, tm), 0).astype(F32)
    rows = []
    for g in range(N_GROUPS):
        blk = sel[g * GROUP_SZ:(g + 1) * GROUP_SZ, :]
        m1 = jnp.max(blk, axis=0, keepdims=True)
        i1 = jnp.min(jnp.where(blk == m1, i8, float(GROUP_SZ)), axis=0, keepdims=True)
        m2 = jnp.max(jnp.where(i8 == i1, neg, blk), axis=0, keepdims=True)
        rows.append(m1 + m2)
    gs = jnp.concatenate(rows, axis=0)
    gi8 = lax.broadcasted_iota(I32, (N_GROUPS, tm), 0).astype(F32)
    gsel = jnp.zeros((N_GROUPS, tm), F32)
    for _ in range(TOPK_GROUPS):
        gm = jnp.max(gs, axis=0, keepdims=True)
        gi = jnp.min(jnp.where(gs == gm, gi8, float(N_GROUPS)), axis=0, keepdims=True)
        hit = gi8 == gi
        gsel = jnp.where(hit, 1.0, gsel)
        gs = jnp.where(hit, neg, gs)
    emask = jnp.concatenate(
        [jnp.broadcast_to(gsel[g:g + 1, :], (GROUP_SZ, tm)) for g in range(N_GROUPS)], axis=0)
    selm = jnp.where(emask > 0.0, sel, neg)

    i64 = lax.broadcasted_iota(I32, (N_EXPERTS, tm), 0).astype(F32)
    chosen = jnp.zeros((N_EXPERTS, tm), F32)
    idxs, wts = [], []
    for _ in range(TOP_K):
        mx = jnp.max(selm, axis=0, keepdims=True)
        ix = jnp.min(jnp.where(selm == mx, i64, float(N_EXPERTS)), axis=0, keepdims=True)
        oh = i64 == ix
        wts.append(jnp.sum(jnp.where(oh, sc, 0.0), axis=0, keepdims=True))
        idxs.append(ix)
        selm = jnp.where(oh, neg, selm)
        chosen = jnp.where(oh, 1.0, chosen)
    wk = jnp.concatenate(wts, axis=0)
    w_ref[...] = wk / jnp.sum(wk, axis=0, keepdims=True) * ROUTE_SCALE
    e_ref[...] = jnp.concatenate(idxs, axis=0).astype(I32)

    before = jnp.dot(chosen.astype(BF16), tri_ref[...], preferred_element_type=F32)
    rank = run_ref[...] + before
    r_ref[...] = jnp.concatenate(
        [jnp.sum(jnp.where(i64 == idxs[k], rank, 0.0), axis=0, keepdims=True) for k in range(TOP_K)],
        axis=0).astype(I32)
    run_ref[...] = run_ref[...] + jnp.sum(chosen, axis=1, keepdims=True)
    cnt_ref[...] = jnp.broadcast_to(run_ref[...], cnt_ref.shape).astype(I32)


def _router(x, g, mod, wh, wl, rb, seq):
    n, d = x.shape
    tm = min(512, seq)
    tps = seq // tm
    slabs = d // LANE
    tri = jnp.triu(jnp.ones((tm, tm), F32), 1).astype(BF16)
    kern = functools.partial(_router_kernel, tm=tm, d=d)
    return pl.pallas_call(
        kern,
        grid=(n // tm,),
        in_specs=[
            pl.BlockSpec((tm, d), lambda i: (i, 0)),
            pl.BlockSpec((1, d), lambda i: (0, 0)),
            pl.BlockSpec((None, 2, d), lambda i: (i // tps, 0, 0)),
            pl.BlockSpec((N_EXPERTS, d), lambda i: (0, 0)),
            pl.BlockSpec((N_EXPERTS, d), lambda i: (0, 0)),
            pl.BlockSpec((N_EXPERTS, 1), lambda i: (0, 0)),
            pl.BlockSpec((tm, tm), lambda i: (0, 0)),
        ],
        out_specs=[
            pl.BlockSpec((tm, d), lambda i: (i, 0)),
            pl.BlockSpec((tm * slabs, LANE), lambda i: (i, 0)),
            pl.BlockSpec((TOP_K, tm), lambda i: (0, i)),
            pl.BlockSpec((TOP_K, tm), lambda i: (0, i)),
            pl.BlockSpec((TOP_K, tm), lambda i: (0, i)),
            pl.BlockSpec((N_EXPERTS, LANE), lambda i: (0, 0)),
        ],
        out_shape=[
            jax.ShapeDtypeStruct((n, d), BF16),
            jax.ShapeDtypeStruct((n * slabs, LANE), F32),
            jax.ShapeDtypeStruct((TOP_K, n), I32),
            jax.ShapeDtypeStruct((TOP_K, n), F32),
            jax.ShapeDtypeStruct((TOP_K, n), I32),
            jax.ShapeDtypeStruct((N_EXPERTS, LANE), I32),
        ],
        scratch_shapes=[pltpu.VMEM((N_EXPERTS, 1), F32)],
        compiler_params=_params("arbitrary"),
    )(x, g.reshape(1, d), mod, wh, wl, rb.reshape(N_EXPERTS, 1), tri)


def _dispatch_kernel(dest_ref, pend_ref, pc_ref, h_ref, xs_ref, zero_ref, sem, zsem, *, tm, n, slabs):
    i = pl.program_id(0)
    rows_blk = MOE_BLK * slabs

    def tail_copy(e):
        start = pl.multiple_of((pend_ref[e] - MOE_BLK) * slabs, rows_blk)
        return pltpu.make_async_copy(zero_ref, xs_ref.at[pl.ds(start, rows_blk), :], zsem)

    @pl.when(i == 0)
    def _():
        zero_ref[...] = jnp.zeros(zero_ref.shape, F32)

        def zstart(e, c):
            @pl.when(pc_ref[e] > 0)
            def _():
                tail_copy(e).start()
            return c

        def zwait(e, c):
            @pl.when(pc_ref[e] > 0)
            def _():
                tail_copy(e).wait()
            return c

        lax.fori_loop(0, N_EXPERTS, zstart, 0)
        lax.fori_loop(0, N_EXPERTS, zwait, 0)

    def row_copy(r, k):
        src = h_ref.at[pl.ds(pl.multiple_of(r * slabs, slabs), slabs), :]
        dst_row = dest_ref[k * n + i * tm + r]
        dst = xs_ref.at[pl.ds(pl.multiple_of(dst_row * slabs, slabs), slabs), :]
        return pltpu.make_async_copy(src, dst, sem)

    def start(r, c):
        for k in range(TOP_K):
            row_copy(r, k).start()
        return c

    def wait(r, c):
        for k in range(TOP_K):
            row_copy(r, k).wait()
        return c

    lax.fori_loop(0, tm, start, 0)
    lax.fori_loop(0, tm, wait, 0)


def _dispatch(dest, pend, pc, h2g, n_rows, seq):
    slabs = h2g.shape[0] * LANE // (dest.shape[0] // TOP_K) // LANE
    n = dest.shape[0] // TOP_K
    tm = min(256, seq)
    kern = functools.partial(_dispatch_kernel, tm=tm, n=n, slabs=slabs)
    return pl.pallas_call(
        kern,
        grid_spec=pltpu.PrefetchScalarGridSpec(
            num_scalar_prefetch=3,
            grid=(n // tm,),
            in_specs=[pl.BlockSpec((tm * slabs, LANE), lambda i, *_: (i, 0))],
            out_specs=pl.BlockSpec(memory_space=pl.ANY),
            scratch_shapes=[
                pltpu.VMEM((MOE_BLK * slabs, LANE), F32),
                pltpu.SemaphoreType.DMA(()),
                pltpu.SemaphoreType.DMA(()),
            ],
        ),
        out_shape=jax.ShapeDtypeStruct((n_rows * slabs, LANE), F32),
        compiler_params=_params("arbitrary"),
    )(dest, pend, pc, h2g)


def _gmm_kernel(be_ref, nu_ref, x_ref, wg_ref, wu_ref, wd_ref, o_ref, *, slabs):
    @pl.when(pl.program_id(0) < nu_ref[0])
    def _():
        x = jnp.concatenate(
            [x_ref[pl.ds(s, MOE_BLK, stride=slabs), :].astype(BF16) for s in range(slabs)], axis=1)
        hg = jnp.dot(x, wg_ref[...], preferred_element_type=F32)
        hu = jnp.dot(x, wu_ref[...], preferred_element_type=F32)
        hb = (jax.nn.silu(hg) * hu).astype(BF16)
        y = jnp.dot(hb, wd_ref[...], preferred_element_type=F32)
        for s in range(slabs):
            o_ref[pl.ds(s, MOE_BLK, stride=slabs), :] = y[:, s * LANE:(s + 1) * LANE]


def _gmm(blk_e, nused, xs, wg, wu, wd):
    _, d, de = wg.shape
    slabs = d // LANE
    nblk = xs.shape[0] // (MOE_BLK * slabs)
    kern = functools.partial(_gmm_kernel, slabs=slabs)

    def blk(b, be, nu):
        return jnp.minimum(b, nu[0] - 1)

    return pl.pallas_call(
        kern,
        grid_spec=pltpu.PrefetchScalarGridSpec(
            num_scalar_prefetch=2,
            grid=(nblk,),
            in_specs=[
                pl.BlockSpec((MOE_BLK * slabs, LANE), lambda b, be, nu: (blk(b, be, nu), 0)),
                pl.BlockSpec((None, d, de), lambda b, be, nu: (be[blk(b, be, nu)], 0, 0)),
                pl.BlockSpec((None, d, de), lambda b, be, nu: (be[blk(b, be, nu)], 0, 0)),
                pl.BlockSpec((None, de, d), lambda b, be, nu: (be[blk(b, be, nu)], 0, 0)),
            ],
            out_specs=pl.BlockSpec((MOE_BLK * slabs, LANE), lambda b, be, nu: (blk(b, be, nu), 0)),
        ),
        out_shape=jax.ShapeDtypeStruct(xs.shape, F32),
        compiler_params=_params("arbitrary"),
    )(blk_e, nused, xs, wg, wu, wd)


def _combine_kernel(dest_ref, wt_ref, x_ref, h2_ref, g2_ref, swg_ref, swu_ref, swd_ref, ys_ref, o_ref,
                    buf_ref, sem, *, tm, n, slabs):
    i = pl.program_id(0)

    def row_copy(r, k):
        src_row = dest_ref[k * n + i * tm + r]
        src = ys_ref.at[pl.ds(pl.multiple_of(src_row * slabs, slabs), slabs), :]
        dst = buf_ref.at[k, pl.ds(pl.multiple_of(r * slabs, slabs), slabs), :]
        return pltpu.make_async_copy(src, dst, sem)

    def start(r, c):
        for k in range(TOP_K):
            row_copy(r, k).start()
        return c

    def wait(r, c):
        for k in range(TOP_K):
            row_copy(r, k).wait()
        return c

    lax.fori_loop(0, tm, start, 0)
    h2 = h2_ref[...]
    hs = (jax.nn.silu(jnp.dot(h2, swg_ref[...], preferred_element_type=F32))
          * jnp.dot(h2, swu_ref[...], preferred_element_type=F32)).astype(BF16)
    shared = jnp.dot(hs, swd_ref[...], preferred_element_type=F32)
    lax.fori_loop(0, tm, wait, 0)

    wt = wt_ref[...]
    for s in range(slabs):
        c0 = s * LANE
        routed = None
        for k in range(TOP_K):
            t = wt[:, k:k + 1] * buf_ref[k, pl.ds(s, tm, stride=slabs), :]
            routed = t if routed is None else routed + t
        o_ref[:, c0:c0 + LANE] = (x_ref[:, c0:c0 + LANE]
                                  + g2_ref[:, c0:c0 + LANE] * (routed + shared[:, c0:c0 + LANE]))


def _combine(dest, wt, x, h2, g2, swg, swu, swd, ys, seq):
    n, d = x.shape
    slabs = d // LANE
    ds_ = swg.shape[1]
    tm = min(128, seq)
    tps = seq // tm
    kern = functools.partial(_combine_kernel, tm=tm, n=n, slabs=slabs)
    return pl.pallas_call(
        kern,
        grid_spec=pltpu.PrefetchScalarGridSpec(
            num_scalar_prefetch=1,
            grid=(n // tm,),
            in_specs=[
                pl.BlockSpec((tm, TOP_K), lambda i, *_: (i, 0)),
                pl.BlockSpec((tm, d), lambda i, *_: (i, 0)),
                pl.BlockSpec((tm, d), lambda i, *_: (i, 0)),
                pl.BlockSpec((None, 1, d), lambda i, *_: (i // tps, 0, 0)),
                pl.BlockSpec((d, ds_), lambda i, *_: (0, 0)),
                pl.BlockSpec((d, ds_), lambda i, *_: (0, 0)),
                pl.BlockSpec((ds_, d), lambda i, *_: (0, 0)),
                pl.BlockSpec(memory_space=pl.ANY),
            ],
            out_specs=pl.BlockSpec((tm, d), lambda i, *_: (i, 0)),
            scratch_shapes=[
                pltpu.VMEM((TOP_K, tm * slabs, LANE), F32),
                pltpu.SemaphoreType.DMA(()),
            ],
        ),
        out_shape=jax.ShapeDtypeStruct((n, d), F32),
        compiler_params=_params("arbitrary"),
    )(dest, wt, x, h2, g2, swg, swu, swd, ys)


def _final_norm_kernel(x_ref, g_ref, o_ref):
    x = x_ref[...]
    o_ref[...] = x * lax.rsqrt(jnp.mean(x * x, axis=-1, keepdims=True) + EPS) * g_ref[...]


def _final_norm(x, g, seq):
    n, d = x.shape
    tm = min(512, seq)
    return pl.pallas_call(
        _final_norm_kernel,
        grid=(n // tm,),
        in_specs=[pl.BlockSpec((tm, d), lambda i: (i, 0)), pl.BlockSpec((1, d), lambda i: (0, 0))],
        out_specs=pl.BlockSpec((tm, d), lambda i: (i, 0)),
        out_shape=jax.ShapeDtypeStruct((n, d), F32),
        compiler_params=_params("parallel"),
    )(x, g.reshape(1, d))


def _swap_halves(w):
    half = w.shape[-1] // 2
    return jnp.concatenate([w[..., half:], w[..., :half]], axis=-1)


def _prep_layer(l, p):
    d = p['w_in'].shape[1]
    sizes = [HA * 2 * DHA, HA * 2 * DHA, HA * DVA, 2 * BRANCH_W, 2 * LRU_W, Q_LORA, KV_LORA, D_ROPE, N_BRANCH * d]
    offs = np.cumsum([0] + sizes)
    w = p['w_in'][l]
    qa, ka, va, uv, xg, cq, ckv, kr, gl = [w[:, offs[i]:offs[i + 1]] for i in range(len(sizes))]
    pad = jnp.zeros((d, 512 - KV_LORA - 2 * D_ROPE), w.dtype)
    w_att = jnp.concatenate([qa * (DHA ** -0.5), ka, va], axis=1).astype(BF16)
    w_rest = jnp.concatenate([uv, xg, cq, ckv, kr, _swap_halves(kr), pad, gl], axis=1).astype(BF16)

    wuq = p['w_uq'][l].reshape(Q_LORA, HD, D_NOPE + D_ROPE)
    wuq_r = wuq[:, :, D_NOPE:]
    wuq_ext = jnp.concatenate([wuq[:, :, :D_NOPE], wuq_r, _swap_halves(wuq_r)], axis=-1)
    wuq_ext = wuq_ext.reshape(Q_LORA, HD * 2 * LANE).astype(BF16)

    lam_init = 0.8 - 0.6 * math.exp(-0.3 * l)
    lam = (jnp.exp(jnp.sum(p['lam_q1'][l] * p['lam_k1'][l])) - jnp.exp(jnp.sum(p['lam_q2'][l] * p['lam_k2'][l]))
           + lam_init)
    slopes = jnp.asarray(2.0 ** (-8.0 * np.arange(1, HA + 1) / HA), F32)
    par = jnp.concatenate([slopes, lam.reshape(1), jnp.full((1,), 1.0 - lam_init, F32), jnp.zeros((2,), F32)])

    rw = p['router_w'][l].T
    rw_hi = rw.astype(BF16)
    rw_lo = (rw - rw_hi.astype(F32)).astype(BF16)
    bs = jnp.repeat(p['sgu_b'][l].T, LANE, axis=1)
    return dict(
        w_att=w_att, w_rest=w_rest, wuq=wuq_ext, wukv=p['w_ukv'][l].astype(BF16), par=par,
        sgu_w=p['sgu_w'][l].astype(BF16), sgu_bs=bs,
        rg_wr=p['rg_wr'][l].astype(BF16), rg_wi=p['rg_wi'][l].astype(BF16),
        w_branch=p['w_branch'][l].astype(BF16), w_out=p['w_out'][l].astype(BF16),
        rw_hi=rw_hi, rw_lo=rw_lo,
        exp_wg=p['exp_wg'][l].astype(BF16), exp_wu=p['exp_wu'][l].astype(BF16), exp_wd=p['exp_wd'][l].astype(BF16),
        sh_wg=p['sh_wg'][l].astype(BF16), sh_wu=p['sh_wu'][l].astype(BF16), sh_wd=p['sh_wd'][l].astype(BF16),
    )


def _rope_table(n_seq, seq):
    inv_freq = ROPE_THETA ** (-jnp.arange(0, D_ROPE, 2, dtype=F32) / D_ROPE)
    ang = jnp.arange(seq, dtype=F32)[:, None] * inv_freq[None, :]
    cos, sin = jnp.cos(ang), jnp.sin(ang)
    tab = jnp.concatenate([cos, cos, -sin, sin], axis=1)
    return jnp.tile(tab, (n_seq, 1))


def _trunk(x3, mods, p, preps):
    n_seq, seq, d = x3.shape
    n = n_seq * seq
    x = x3.reshape(n, d)
    tab = _rope_table(n_seq, seq)
    depth = len(preps)
    for l in range(depth):
        w = preps[l]
        mod = mods[l]
        mod1 = jnp.stack([mod[:, 1], mod[:, 0]], axis=1)
        mod2 = jnp.stack([mod[:, 4], mod[:, 3]], axis=1)
        g1 = mod[:, 2:3]
        g2 = mod[:, 5:6]

        att = _inproj(x, p['norm1_g'][l], mod1, w['w_att'], BF16, seq)
        rest = _inproj(x, p['norm1_g'][l], mod1, w['w_rest'], F32, seq)
        ya = _diff_attn(att, w['par'], p['subln_g'][l], n_seq, seq)
        yb = _sgu(rest, p['sgu_ln_g'][l], p['sgu_ln_b'][l], w['sgu_w'], w['sgu_bs'], seq)
        hf, hr = _lru(rest, p['conv_w'][l], p['conv_b'][l], w['rg_wr'], p['rg_br'][l], w['rg_wi'],
                      p['rg_bi'][l], p['rg_lam'][l], seq)
        q, k, v = _mla_prep(rest, tab, p['q_norm_g'][l], p['kv_norm_g'][l], w['wuq'], w['wukv'], seq)
        yd = _mla_attn(q, k, v, n_seq, seq)
        merged = _merge(ya, yb, hf, hr, rest, yd, w['w_branch'], seq)
        x = _outproj(merged, w['w_out'], x, g1, seq)

        h2, h2g, eidx, wgt, rank, cnt = _router(x, p['norm2_g'][l], mod2, w['rw_hi'], w['rw_lo'],
                                                p['router_bias'][l], seq)
        counts = cnt[:, 0]
        pc = ((counts + MOE_BLK - 1) // MOE_BLK) * MOE_BLK
        pend = jnp.cumsum(pc).astype(I32)
        pstart = pend - pc
        dest = (pstart[eidx] + rank).reshape(-1).astype(I32)
        nblk = n * TOP_K // MOE_BLK + N_EXPERTS
        blk_e = jnp.minimum(jnp.searchsorted(pend, jnp.arange(nblk, dtype=I32) * MOE_BLK, side='right'),
                            N_EXPERTS - 1).astype(I32)
        nused = (pend[-1:] // MOE_BLK).astype(I32)
        xs = _dispatch(dest, pend, pc.astype(I32), h2g, nblk * MOE_BLK, seq)
        ys = _gmm(blk_e, nused, xs, w['exp_wg'], w['exp_wu'], w['exp_wd'])
        x = _combine(dest, wgt.T, x, h2, g2, w['sh_wg'], w['sh_wu'], w['sh_wd'], ys, seq)
    return _final_norm(x, p['final_g'], seq).reshape(n_seq, seq, d)


def kernel(x_prompt, x_sample, c_prompt, c_sample, ada_w, ada_b, norm1_g, norm2_g, w_in, lam_q1, lam_k1, lam_q2, lam_k2, subln_g, sgu_ln_g, sgu_ln_b, sgu_w, sgu_b, conv_w, conv_b, rg_wr, rg_br, rg_wi, rg_bi, rg_lam, q_norm_g, kv_norm_g, w_uq, w_ukv, w_branch, w_out, router_w, router_bias, exp_wg, exp_wu, exp_wd, sh_wg, sh_wu, sh_wd, final_g):
    p = dict(ada_w=ada_w, ada_b=ada_b, norm1_g=norm1_g, norm2_g=norm2_g, w_in=w_in,
             lam_q1=lam_q1, lam_k1=lam_k1, lam_q2=lam_q2, lam_k2=lam_k2, subln_g=subln_g,
             sgu_ln_g=sgu_ln_g, sgu_ln_b=sgu_ln_b, sgu_w=sgu_w, sgu_b=sgu_b,
             conv_w=conv_w, conv_b=conv_b, rg_wr=rg_wr, rg_br=rg_br, rg_wi=rg_wi, rg_bi=rg_bi,
             rg_lam=rg_lam, q_norm_g=q_norm_g, kv_norm_g=kv_norm_g, w_uq=w_uq, w_ukv=w_ukv,
             w_branch=w_branch, w_out=w_out, router_w=router_w, router_bias=router_bias,
             exp_wg=exp_wg, exp_wu=exp_wu, exp_wd=exp_wd, sh_wg=sh_wg, sh_wu=sh_wu, sh_wd=sh_wd,
             final_g=final_g)
    depth, d, _ = ada_w.shape
    bp, bs = x_prompt.shape[0], x_sample.shape[0]
    assert bp + bs <= 8
    c8 = jnp.concatenate([c_prompt, c_sample, jnp.zeros((8 - bp - bs, d), F32)], axis=0)
    mod = _ada_mod(c8, ada_w, ada_b).reshape(depth, 8, 6, d)
    preps = [_prep_layer(l, p) for l in range(depth)]
    y_prompt = _trunk(x_prompt, mod[:, :bp], p, preps)
    y_sample = _trunk(x_sample, mod[:, bp:bp + bs], p, preps)
    return (y_prompt, y_sample)
```

```python
import functools
import math

import numpy as np
import jax
import jax.numpy as jnp
from jax import lax
from jax.experimental import pallas as pl
from jax.experimental.pallas import tpu as pltpu

F32 = jnp.float32
BF16 = jnp.bfloat16
I32 = jnp.int32

EPS = 1e-6
LOG2E = math.log2(math.e)
LANE = 128
SUBLANE = 8
VMEM_LIMIT = 48 * 1024 * 1024

HA = 4
DHA = 64
DVA = 2 * DHA
BRANCH_W = 512
SGU_CHUNK = 128
SGU_GROUPS = 4
LRU_W = 512
LRU_BLOCKS = 4
LRU_BW = LRU_W // LRU_BLOCKS
LRU_C = 8.0
HD = 4
Q_LORA = 512
KV_LORA = 256
D_NOPE = 128
D_ROPE = 64
DV_D = 128
ROPE_THETA = 10000.0
N_EXPERTS = 64
TOP_K = 8
N_GROUPS = 8
TOPK_GROUPS = 4
GROUP_SZ = N_EXPERTS // N_GROUPS
ROUTE_SCALE = 2.5
MOE_BLK = 256
N_BRANCH = 4

R_UV = 0
R_XC = 1024
R_GATE = 1536
R_CQ = 2048
R_CKV = 2560
R_GL = 3072

NT_DIMS = (((1,), (1,)), ((), ()))


def _params(*sem):
    return pltpu.CompilerParams(dimension_semantics=sem, vmem_limit_bytes=VMEM_LIMIT)


def _ada_kernel(c_ref, w_ref, b_ref, o_ref):
    a = jax.nn.silu(c_ref[...]).astype(BF16)
    o_ref[...] = jnp.dot(a, w_ref[...].astype(BF16), preferred_element_type=F32) + b_ref[...]


def _ada_mod(c8, ada_w, ada_b):
    depth, d, n = ada_w.shape
    tn = 1024
    return pl.pallas_call(
        _ada_kernel,
        grid=(depth, n // tn),
        in_specs=[
            pl.BlockSpec((8, d), lambda l, j: (0, 0)),
            pl.BlockSpec((None, d, tn), lambda l, j: (l, 0, j)),
            pl.BlockSpec((None, 1, tn), lambda l, j: (l, 0, j)),
        ],
        out_specs=pl.BlockSpec((None, 8, tn), lambda l, j: (l, 0, j)),
        out_shape=jax.ShapeDtypeStruct((depth, 8, n), F32),
        compiler_params=_params("parallel", "parallel"),
    )(c8, ada_w, ada_b.reshape(depth, 1, n))


def _inproj_kernel(x_ref, g_ref, mod_ref, w_ref, o_ref, h_ref):
    @pl.when(pl.program_id(1) == 0)
    def _():
        x = x_ref[...]
        y = x * lax.rsqrt(jnp.mean(x * x, axis=-1, keepdims=True) + EPS) * g_ref[...]
        h_ref[...] = (y * (1.0 + mod_ref[0:1, :]) + mod_ref[1:2, :]).astype(BF16)

    o_ref[...] = jnp.dot(h_ref[...], w_ref[...], preferred_element_type=F32).astype(o_ref.dtype)


def _inproj(x, g, mod, w, out_dtype, seq):
    n, d = x.shape
    nc = w.shape[1]
    tm = min(1024, seq)
    tn = 512
    tps = seq // tm
    return pl.pallas_call(
        _inproj_kernel,
        grid=(n // tm, nc // tn),
        in_specs=[
            pl.BlockSpec((tm, d), lambda i, j: (i, 0)),
            pl.BlockSpec((1, d), lambda i, j: (0, 0)),
            pl.BlockSpec((None, 2, d), lambda i, j: (i // tps, 0, 0)),
            pl.BlockSpec((d, tn), lambda i, j: (0, j)),
        ],
        out_specs=pl.BlockSpec((tm, tn), lambda i, j: (i, j)),
        out_shape=jax.ShapeDtypeStruct((n, nc), out_dtype),
        scratch_shapes=[pltpu.VMEM((tm, d), BF16)],
        compiler_params=_params("parallel", "arbitrary"),
    )(x, g.reshape(1, d), mod, w)


def _online_softmax_step(s, vx, m_ref, acc_ref, idx, tk):
    m_prev = m_ref[idx]
    m_new = jnp.maximum(m_prev, jnp.max(s, axis=-1, keepdims=True))
    alpha = jnp.exp2(m_prev - m_new)
    p = jnp.exp2(s - jnp.concatenate([m_new] * (tk // LANE), axis=1))
    pv = jnp.dot(p.astype(BF16), vx, preferred_element_type=F32)
    acc_ref[idx] = jnp.concatenate([alpha, alpha], axis=1) * acc_ref[idx] + pv
    m_ref[idx] = m_new


def _diff_attn_kernel(par_ref, q_ref, k_ref, v_ref, g_ref, o_ref, m_ref, acc_ref, *, tq, tk, nk):
    qi = pl.program_id(1)
    ki = pl.program_id(2)

    @pl.when(ki == 0)
    def _():
        m_ref[...] = jnp.full(m_ref.shape, -jnp.inf, F32)
        acc_ref[...] = jnp.zeros(acc_ref.shape, F32)

    lane = lax.broadcasted_iota(I32, (tq, LANE), 1)
    row = lax.broadcasted_iota(I32, (tq, tk), 0)
    col = lax.broadcasted_iota(I32, (tq, tk), 1)
    dist = jnp.abs(row - col + (qi * tq - ki * tk)).astype(F32)
    ones = jnp.ones((tk, LANE), BF16)
    for h in range(HA):
        q = q_ref[:, h * LANE:(h + 1) * LANE]
        k = k_ref[:, h * LANE:(h + 1) * LANE]
        vx = jnp.concatenate([v_ref[:, h * LANE:(h + 1) * LANE], ones], axis=1)
        bias = dist * par_ref[h]
        zero = jnp.zeros_like(q)
        for mi, qm in enumerate((jnp.where(lane < DHA, q, zero), jnp.where(lane < DHA, zero, q))):
            s = lax.dot_general(qm, k, NT_DIMS, preferred_element_type=F32) - bias
            _online_softmax_step(s, vx, m_ref, acc_ref, 2 * h + mi, tk)

    @pl.when(ki == nk - 1)
    def _():
        lam = par_ref[HA]
        for h in range(HA):
            a1 = acc_ref[2 * h]
            a2 = acc_ref[2 * h + 1]
            o = a1[:, :LANE] / a1[:, LANE:] - lam * (a2[:, :LANE] / a2[:, LANE:])
            y = o * lax.rsqrt(jnp.mean(o * o, axis=-1, keepdims=True) + EPS) * g_ref[...]
            o_ref[:, h * LANE:(h + 1) * LANE] = (y * par_ref[HA + 1]).astype(o_ref.dtype)


def _diff_attn(att, par, subln_g, n_seq, seq):
    n = att.shape[0]
    tq = min(512, seq)
    tk = min(512, seq)
    nq, nk = seq // tq, seq // tk
    w = HA * LANE
    kern = functools.partial(_diff_attn_kernel, tq=tq, tk=tk, nk=nk)
    return pl.pallas_call(
        kern,
        grid=(n_seq, nq, nk),
        in_specs=[
            pl.BlockSpec(memory_space=pltpu.SMEM),
            pl.BlockSpec((tq, w), lambda b, qi, ki: (b * nq + qi, 0)),
            pl.BlockSpec((tk, w), lambda b, qi, ki: (b * nk + ki, 1)),
            pl.BlockSpec((tk, w), lambda b, qi, ki: (b * nk + ki, 2)),
            pl.BlockSpec((1, DVA), lambda b, qi, ki: (0, 0)),
        ],
        out_specs=pl.BlockSpec((tq, w), lambda b, qi, ki: (b * nq + qi, 0)),
        out_shape=jax.ShapeDtypeStruct((n, HA * DVA), BF16),
        scratch_shapes=[
            pltpu.VMEM((2 * HA, tq, LANE), F32),
            pltpu.VMEM((2 * HA, tq, 2 * LANE), F32),
        ],
        compiler_params=_params("parallel", "parallel", "arbitrary"),
    )(par, att, att, att, subln_g.reshape(1, DVA))


def _mla_prep_kernel(cq_ref, ckv_ref, tab_ref, gq_ref, gkv_ref, wuq_ref, wukv_ref, q_ref, k_ref, v_ref, *, scale):
    cq = cq_ref[...]
    cqn = (cq * lax.rsqrt(jnp.mean(cq * cq, axis=-1, keepdims=True) + EPS) * gq_ref[...]).astype(BF16)
    c = ckv_ref[...]
    ckv = c[:, :KV_LORA]
    ckvn = (ckv * lax.rsqrt(jnp.mean(ckv * ckv, axis=-1, keepdims=True) + EPS) * gkv_ref[...]).astype(BF16)
    tab = tab_ref[...]
    lane = lax.broadcasted_iota(I32, tab.shape, 1)

    def rope(pair):
        pr = pair * tab
        return jnp.where(lane < D_ROPE, pr + pltpu.roll(pr, D_ROPE, 1), 0.0)

    kr = rope(c[:, KV_LORA:KV_LORA + LANE]).astype(BF16)
    qf = jnp.dot(cqn, wuq_ref[...], preferred_element_type=F32)
    kvf = jnp.dot(ckvn, wukv_ref[...], preferred_element_type=F32)
    for h in range(HD):
        b0 = h * 2 * LANE
        q_ref[:, b0:b0 + LANE] = (qf[:, b0:b0 + LANE] * scale).astype(BF16)
        q_ref[:, b0 + LANE:b0 + 2 * LANE] = (rope(qf[:, b0 + LANE:b0 + 2 * LANE]) * scale).astype(BF16)
        k_ref[:, b0:b0 + LANE] = kvf[:, b0:b0 + LANE].astype(BF16)
        k_ref[:, b0 + LANE:b0 + 2 * LANE] = kr
        v_ref[:, h * LANE:(h + 1) * LANE] = kvf[:, b0 + LANE:b0 + 2 * LANE].astype(BF16)


def _mla_prep(rest, tab, gq, gkv, wuq, wukv, seq):
    n = rest.shape[0]
    tm = min(512, seq)
    scale = (D_NOPE + D_ROPE) ** -0.5 * LOG2E
    kern = functools.partial(_mla_prep_kernel, scale=scale)
    wq = HD * 2 * LANE
    return pl.pallas_call(
        kern,
        grid=(n // tm,),
        in_specs=[
            pl.BlockSpec((tm, Q_LORA), lambda i: (i, R_CQ // Q_LORA)),
            pl.BlockSpec((tm, 512), lambda i: (i, R_CKV // 512)),
            pl.BlockSpec((tm, LANE), lambda i: (i, 0)),
            pl.BlockSpec((1, Q_LORA), lambda i: (0, 0)),
            pl.BlockSpec((1, KV_LORA), lambda i: (0, 0)),
            pl.BlockSpec((Q_LORA, wq), lambda i: (0, 0)),
            pl.BlockSpec((KV_LORA, wq), lambda i: (0, 0)),
        ],
        out_specs=[
            pl.BlockSpec((tm, wq), lambda i: (i, 0)),
            pl.BlockSpec((tm, wq), lambda i: (i, 0)),
            pl.BlockSpec((tm, HD * DV_D), lambda i: (i, 0)),
        ],
        out_shape=[
            jax.ShapeDtypeStruct((n, wq), BF16),
            jax.ShapeDtypeStruct((n, wq), BF16),
            jax.ShapeDtypeStruct((n, HD * DV_D), BF16),
        ],
        compiler_params=_params("parallel"),
    )(rest, rest, tab, gq.reshape(1, -1), gkv.reshape(1, -1), wuq, wukv)


def _mla_attn_kernel(q_ref, k_ref, v_ref, o_ref, m_ref, acc_ref, *, tk, nk):
    ki = pl.program_id(2)

    @pl.when(ki == 0)
    def _():
        m_ref[...] = jnp.full(m_ref.shape, -jnp.inf, F32)
        acc_ref[...] = jnp.zeros(acc_ref.shape, F32)

    ones = jnp.ones((tk, LANE), BF16)
    for h in range(HD):
        q = q_ref[:, h * 2 * LANE:(h + 1) * 2 * LANE]
        k = k_ref[:, h * 2 * LANE:(h + 1) * 2 * LANE]
        vx = jnp.concatenate([v_ref[:, h * LANE:(h + 1) * LANE], ones], axis=1)
        s = lax.dot_general(q, k, NT_DIMS, preferred_element_type=F32)
        _online_softmax_step(s, vx, m_ref, acc_ref, h, tk)

    @pl.when(ki == nk - 1)
    def _():
        for h in range(HD):
            a = acc_ref[h]
            o_ref[:, h * LANE:(h + 1) * LANE] = (a[:, :LANE] / a[:, LANE:]).astype(o_ref.dtype)


def _mla_attn(q, k, v, n_seq, seq):
    n = q.shape[0]
    tq = min(512, seq)
    tk = min(512, seq)
    nq, nk = seq // tq, seq // tk
    wq = HD * 2 * LANE
    kern = functools.partial(_mla_attn_kernel, tk=tk, nk=nk)
    return pl.pallas_call(
        kern,
        grid=(n_seq, nq, nk),
        in_specs=[
            pl.BlockSpec((tq, wq), lambda b, qi, ki: (b * nq + qi, 0)),
            pl.BlockSpec((tk, wq), lambda b, qi, ki: (b * nk + ki, 0)),
            pl.BlockSpec((tk, HD * DV_D), lambda b, qi, ki: (b * nk + ki, 0)),
        ],
        out_specs=pl.BlockSpec((tq, HD * DV_D), lambda b, qi, ki: (b * nq + qi, 0)),
        out_shape=jax.ShapeDtypeStruct((n, HD * DV_D), BF16),
        scratch_shapes=[
            pltpu.VMEM((HD, tq, LANE), F32),
            pltpu.VMEM((HD, tq, 2 * LANE), F32),
        ],
        compiler_params=_params("parallel", "parallel", "arbitrary"),
    )(q, k, v)


def _sgu_kernel(uv_ref, g_ref, b_ref, ws_ref, bs_ref, o_ref, *, tm):
    z = jax.nn.gelu(uv_ref[...])
    u = z[:, :BRANCH_W]
    v = z[:, BRANCH_W:]
    mu = jnp.mean(v, axis=-1, keepdims=True)
    vc = v - mu
    var = jnp.mean(vc * vc, axis=-1, keepdims=True)
    vn = (vc * lax.rsqrt(var + EPS) * g_ref[...] + b_ref[...]).astype(BF16)
    for c in range(tm // SGU_CHUNK):
        r0 = c * SGU_CHUNK
        for g in range(SGU_GROUPS):
            c0 = g * LANE
            vm = jnp.dot(ws_ref[g], vn[r0:r0 + SGU_CHUNK, c0:c0 + LANE], preferred_element_type=F32)
            vm = vm + bs_ref[:, c0:c0 + LANE]
            o_ref[r0:r0 + SGU_CHUNK, c0:c0 + LANE] = (u[r0:r0 + SGU_CHUNK, c0:c0 + LANE] * vm).astype(o_ref.dtype)


def _sgu(rest, ln_g, ln_b, ws, bs, seq):
    n = rest.shape[0]
    tm = min(512, seq)
    kern = functools.partial(_sgu_kernel, tm=tm)
    return pl.pallas_call(
        kern,
        grid=(n // tm,),
        in_specs=[
            pl.BlockSpec((tm, 2 * BRANCH_W), lambda i: (i, R_UV // (2 * BRANCH_W))),
            pl.BlockSpec((1, BRANCH_W), lambda i: (0, 0)),
            pl.BlockSpec((1, BRANCH_W), lambda i: (0, 0)),
            pl.BlockSpec((SGU_GROUPS, SGU_CHUNK, SGU_CHUNK), lambda i: (0, 0, 0)),
            pl.BlockSpec((SGU_CHUNK, BRANCH_W), lambda i: (0, 0)),
        ],
        out_specs=pl.BlockSpec((tm, BRANCH_W), lambda i: (i, 0)),
        out_shape=jax.ShapeDtypeStruct((n, BRANCH_W), BF16),
        compiler_params=_params("parallel"),
    )(rest, ln_g.reshape(1, -1), ln_b.reshape(1, -1), ws, bs)


def _lru_kernel(xf_ref, xfp_ref, xfn_ref, xr_ref, xrp_ref, xrn_ref, cw_ref, cb_ref, wr_ref, br_ref, wi_ref,
                bi_ref, lam_ref, hf_ref, hr_ref, af_ref, uf_ref, ar_ref, ur_ref, hc_ref, *, tt, tps):
    j = pl.program_id(0) % tps
    rowi = lax.broadcasted_iota(I32, (tt, LRU_W), 0)

    def gates(x_ref, xp_ref, xn_ref, at_start, at_end, d, a_ref, u_ref):
        x = x_ref[...]
        prev = jnp.where(at_start, 0.0, xp_ref[SUBLANE - 1:SUBLANE, :])
        nxt = jnp.where(at_end, 0.0, xn_ref[0:2, :])
        xm1 = jnp.where(rowi == 0, prev, pltpu.roll(x, 1, 0))
        xp1 = jnp.where(rowi == tt - 1, nxt[0:1, :], pltpu.roll(x, tt - 1, 0))
        xp2 = jnp.where(rowi == tt - 2, nxt[0:1, :],
                        jnp.where(rowi == tt - 1, nxt[1:2, :], pltpu.roll(x, tt - 2, 0)))
        xc = xm1 * cw_ref[0:1, :] + x * cw_ref[1:2, :] + xp1 * cw_ref[2:3, :] + xp2 * cw_ref[3:4, :] + cb_ref[...]
        xcb = xc.astype(BF16)

        def blockdiag(w_ref):
            return jnp.concatenate(
                [jnp.dot(xcb[:, g * LRU_BW:(g + 1) * LRU_BW], w_ref[d, g], preferred_element_type=F32)
                 for g in range(LRU_BLOCKS)], axis=1)

        r = jax.nn.sigmoid(blockdiag(wr_ref) + br_ref[d:d + 1, :])
        ig = jax.nn.sigmoid(blockdiag(wi_ref) + bi_ref[d:d + 1, :])
        log_a = (-LRU_C * r) * jax.nn.softplus(-lam_ref[d:d + 1, :])
        a = jnp.exp(log_a)
        u = jnp.sqrt(jnp.tanh(-log_a) * (1.0 + a * a)) * (ig * xc)
        a_ref[...] = a
        u_ref[...] = u

    gates(xf_ref, xfp_ref, xfn_ref, j == 0, j == tps - 1, 0, af_ref, uf_ref)
    gates(xr_ref, xrp_ref, xrn_ref, j == tps - 1, j == 0, 1, ar_ref, ur_ref)

    @pl.when(j == 0)
    def _():
        hc_ref[...] = jnp.zeros(hc_ref.shape, F32)

    ng = tt // SUBLANE
    sub = lax.broadcasted_iota(I32, (SUBLANE, LRU_W), 0)

    def body(g, carry):
        hf_prev, hr_prev = carry
        r0 = pl.multiple_of(g * SUBLANE, SUBLANE)
        a = af_ref[pl.ds(r0, SUBLANE), :]
        u = uf_ref[pl.ds(r0, SUBLANE), :]
        for s in (1, 2, 4):
            a_s = jnp.where(sub >= s, pltpu.roll(a, s, 0), 1.0)
            u_s = jnp.where(sub >= s, pltpu.roll(u, s, 0), 0.0)
            u = a * u_s + u
            a = a * a_s
        h = a * hf_prev + u
        hf_ref[pl.ds(r0, SUBLANE), :] = h
        hf_new = jnp.broadcast_to(h[SUBLANE - 1:SUBLANE, :], (SUBLANE, LRU_W))

        r1 = pl.multiple_of((ng - 1 - g) * SUBLANE, SUBLANE)
        a = ar_ref[pl.ds(r1, SUBLANE), :]
        u = ur_ref[pl.ds(r1, SUBLANE), :]
        for s in (1, 2, 4):
            a_s = jnp.where(sub < SUBLANE - s, pltpu.roll(a, SUBLANE - s, 0), 1.0)
            u_s = jnp.where(sub < SUBLANE - s, pltpu.roll(u, SUBLANE - s, 0), 0.0)
            u = a * u_s + u
            a = a * a_s
        h = a * hr_prev + u
        hr_ref[pl.ds(r1, SUBLANE), :] = h
        hr_new = jnp.broadcast_to(h[0:1, :], (SUBLANE, LRU_W))
        return hf_new, hr_new

    hf_c, hr_c = lax.fori_loop(0, ng, body, (hc_ref[0], hc_ref[1]))
    hc_ref[0] = hf_c
    hc_ref[1] = hr_c


def _lru(rest, cw, cb, wr, br, wi, bi, lam, seq):
    n = rest.shape[0]
    tt = min(512, seq)
    tps = seq // tt
    nt = n // tt
    t8 = tt // SUBLANE
    last8 = n // SUBLANE - 1
    cblk = R_XC // LRU_W

    def rev(i):
        return (i // tps) * tps + (tps - 1 - i % tps)

    kern = functools.partial(_lru_kernel, tt=tt, tps=tps)
    full = lambda shape: pl.BlockSpec(shape, lambda i: (0,) * len(shape))
    return pl.pallas_call(
        kern,
        grid=(nt,),
        in_specs=[
            pl.BlockSpec((tt, LRU_W), lambda i: (i, cblk)),
            pl.BlockSpec((SUBLANE, LRU_W), lambda i: (jnp.maximum(i * t8 - 1, 0), cblk)),
            pl.BlockSpec((SUBLANE, LRU_W), lambda i: (jnp.minimum((i + 1) * t8, last8), cblk)),
            pl.BlockSpec((tt, LRU_W), lambda i: (rev(i), cblk)),
            pl.BlockSpec((SUBLANE, LRU_W), lambda i: (jnp.maximum(rev(i) * t8 - 1, 0), cblk)),
            pl.BlockSpec((SUBLANE, LRU_W), lambda i: (jnp.minimum((rev(i) + 1) * t8, last8), cblk)),
            full((4, LRU_W)),
            full((1, LRU_W)),
            full((2, LRU_BLOCKS, LRU_BW, LRU_BW)),
            full((2, LRU_W)),
            full((2, LRU_BLOCKS, LRU_BW, LRU_BW)),
            full((2, LRU_W)),
            full((2, LRU_W)),
        ],
        out_specs=[
            pl.BlockSpec((tt, LRU_W), lambda i: (i, 0)),
            pl.BlockSpec((tt, LRU_W), lambda i: (rev(i), 0)),
        ],
        out_shape=[jax.ShapeDtypeStruct((n, LRU_W), F32), jax.ShapeDtypeStruct((n, LRU_W), F32)],
        scratch_shapes=[pltpu.VMEM((tt, LRU_W), F32)] * 4 + [pltpu.VMEM((2, SUBLANE, LRU_W), F32)],
        compiler_params=_params("arbitrary"),
    )(rest, rest, rest, rest, rest, rest, cw, cb.reshape(1, -1), wr, br, wi, bi, lam)


def _merge_kernel(ya_ref, yb_ref, hf_ref, hr_ref, gate_ref, yd_ref, wbr_ref, g0_ref, g1_ref, g2_ref, g3_ref, o_ref):
    yc = (jax.nn.gelu(gate_ref[...]) * (hf_ref[...] + hr_ref[...])).astype(BF16)
    ys = (ya_ref[...], yb_ref[...], yc, yd_ref[...])
    gls = (g0_ref, g1_ref, g2_ref, g3_ref)
    acc = None
    for k in range(N_BRANCH):
        t = jax.nn.sigmoid(gls[k][...]) * jnp.dot(ys[k], wbr_ref[k], preferred_element_type=F32)
        acc = t if acc is None else acc + t
    o_ref[...] = acc.astype(o_ref.dtype)


def _merge(ya, yb, hf, hr, rest, yd, wbr, seq):
    n = ya.shape[0]
    d = wbr.shape[-1]
    tm = min(512, seq)
    tn = 512
    row = lambda w: pl.BlockSpec((tm, w), lambda i, j: (i, 0))

    def gl_spec(k):
        base = (R_GL + k * d) // tn
        return pl.BlockSpec((tm, tn), lambda i, j: (i, base + j))

    return pl.pallas_call(
        _merge_kernel,
        grid=(n // tm, d // tn),
        in_specs=[
            row(BRANCH_W), row(BRANCH_W), row(LRU_W), row(LRU_W),
            pl.BlockSpec((tm, LRU_W), lambda i, j: (i, R_GATE // LRU_W)),
            row(BRANCH_W),
            pl.BlockSpec((N_BRANCH, BRANCH_W, tn), lambda i, j: (0, 0, j)),
            gl_spec(0), gl_spec(1), gl_spec(2), gl_spec(3),
        ],
        out_specs=pl.BlockSpec((tm, tn), lambda i, j: (i, j)),
        out_shape=jax.ShapeDtypeStruct((n, d), BF16),
        compiler_params=_params("parallel", "parallel"),
    )(ya, yb, hf, hr, rest, yd, wbr, rest, rest, rest, rest)


def _outproj_kernel(m_ref, w_ref, x_ref, g_ref, o_ref):
    o_ref[...] = x_ref[...] + g_ref[...] * jnp.dot(m_ref[...], w_ref[...], preferred_element_type=F32)


def _outproj(merged, w, x, g1, seq):
    n, d = x.shape
    tm = min(512, seq)
    tps = seq // tm
    return pl.pallas_call(
        _outproj_kernel,
        grid=(n // tm,),
        in_specs=[
            pl.BlockSpec((tm, d), lambda i: (i, 0)),
            pl.BlockSpec((d, d), lambda i: (0, 0)),
            pl.BlockSpec((tm, d), lambda i: (i, 0)),
            pl.BlockSpec((None, 1, d), lambda i: (i // tps, 0, 0)),
        ],
        out_specs=pl.BlockSpec((tm, d), lambda i: (i, 0)),
        out_shape=jax.ShapeDtypeStruct((n, d), F32),
        compiler_params=_params("parallel"),
    )(merged, w, x, g1)


def _router_kernel(x_ref, g_ref, mod_ref, wh_ref, wl_ref, rb_ref, tri_ref,
                   h2_ref, h2g_ref, e_ref, w_ref, r_ref, cnt_ref, run_ref, *, tm, d):
    @pl.when(pl.program_id(0) == 0)
    def _():
        run_ref[...] = jnp.zeros(run_ref.shape, F32)

    x = x_ref[...]
    y = x * lax.rsqrt(jnp.mean(x * x, axis=-1, keepdims=True) + EPS) * g_ref[...]
    h = y * (1.0 + mod_ref[0:1, :]) + mod_ref[1:2, :]
    hb = h.astype(BF16)
    h2_ref[...] = hb
    slabs = d // LANE
    for s in range(slabs):
        h2g_ref[pl.ds(s, tm, stride=slabs), :] = h[:, s * LANE:(s + 1) * LANE]

    hl = (h - hb.astype(F32)).astype(BF16)
    logits = (lax.dot_general(wh_ref[...], hb, NT_DIMS, preferred_element_type=F32)
              + lax.dot_general(wl_ref[...], hb, NT_DIMS, preferred_element_type=F32)
              + lax.dot_general(wh_ref[...], hl, NT_DIMS, preferred_element_type=F32))
    sc = jax.nn.sigmoid(logits)
    sel = sc + rb_ref[...]

    neg = -jnp.inf
    i8 = lax.broadcasted_iota(I32, (GROUP_SZ, tm), 0).astype(F32)
    rows = []
    for g in range(N_GROUPS):
        blk = sel[g * GROUP_SZ:(g + 1) * GROUP_SZ, :]
        m1 = jnp.max(blk, axis=0, keepdims=True)
        i1 = jnp.min(jnp.where(blk == m1, i8, float(GROUP_SZ)), axis=0, keepdims=True)
        m2 = jnp.max(jnp.where(i8 == i1, neg, blk), axis=0, keepdims=True)
        rows.append(m1 + m2)
    gs = jnp.concatenate(rows, axis=0)
    gi8 = lax.broadcasted_iota(I32, (N_GROUPS, tm), 0).astype(F32)
    gsel = jnp.zeros((N_GROUPS, tm), F32)
    for _ in range(TOPK_GROUPS):
        gm = jnp.max(gs, axis=0, keepdims=True)
        gi = jnp.min(jnp.where(gs == gm, gi8, float(N_GROUPS)), axis=0, keepdims=True)
        hit = gi8 == gi
        gsel = jnp.where(hit, 1.0, gsel)
        gs = jnp.where(hit, neg, gs)
    emask = jnp.concatenate(
        [jnp.broadcast_to(gsel[g:g + 1, :], (GROUP_SZ, tm)) for g in range(N_GROUPS)], axis=0)
    selm = jnp.where(emask > 0.0, sel, neg)

    i64 = lax.broadcasted_iota(I32, (N_EXPERTS, tm), 0).astype(F32)
    chosen = jnp.zeros((N_EXPERTS, tm), F32)
    idxs, wts = [], []
    for _ in range(TOP_K):
        mx = jnp.max(selm, axis=0, keepdims=True)
        ix = jnp.min(jnp.where(selm == mx, i64, float(N_EXPERTS)), axis=0, keepdims=True)
        oh = i64 == ix
        wts.append(jnp.sum(jnp.where(oh, sc, 0.0), axis=0, keepdims=True))
        idxs.append(ix)
        selm = jnp.where(oh, neg, selm)
        chosen = jnp.where(oh, 1.0, chosen)
    wk = jnp.concatenate(wts, axis=0)
    w_ref[...] = wk / jnp.sum(wk, axis=0, keepdims=True) * ROUTE_SCALE
    e_ref[...] = jnp.concatenate(idxs, axis=0).astype(I32)

    before = jnp.dot(chosen.astype(BF16), tri_ref[...], preferred_element_type=F32)
    rank = run_ref[...] + before
    r_ref[...] = jnp.concatenate(
        [jnp.sum(jnp.where(i64 == idxs[k], rank, 0.0), axis=0, keepdims=True) for k in range(TOP_K)],
        axis=0).astype(I32)
    run_ref[...] = run_ref[...] + jnp.sum(chosen, axis=1, keepdims=True)
    cnt_ref[...] = jnp.broadcast_to(run_ref[...], cnt_ref.shape).astype(I32)


def _router(x, g, mod, wh, wl, rb, seq):
    n, d = x.shape
    tm = min(512, seq)
    tps = seq // tm
    slabs = d // LANE
    tri = jnp.triu(jnp.ones((tm, tm), F32), 1).astype(BF16)
    kern = functools.partial(_router_kernel, tm=tm, d=d)
    return pl.pallas_call(
        kern,
        grid=(n // tm,),
        in_specs=[
            pl.BlockSpec((tm, d), lambda i: (i, 0)),
            pl.BlockSpec((1, d), lambda i: (0, 0)),
            pl.BlockSpec((None, 2, d), lambda i: (i // tps, 0, 0)),
            pl.BlockSpec((N_EXPERTS, d), lambda i: (0, 0)),
            pl.BlockSpec((N_EXPERTS, d), lambda i: (0, 0)),
            pl.BlockSpec((N_EXPERTS, 1), lambda i: (0, 0)),
            pl.BlockSpec((tm, tm), lambda i: (0, 0)),
        ],
        out_specs=[
            pl.BlockSpec((tm, d), lambda i: (i, 0)),
            pl.BlockSpec((tm * slabs, LANE), lambda i: (i, 0)),
            pl.BlockSpec((TOP_K, tm), lambda i: (0, i)),
            pl.BlockSpec((TOP_K, tm), lambda i: (0, i)),
            pl.BlockSpec((TOP_K, tm), lambda i: (0, i)),
            pl.BlockSpec((N_EXPERTS, LANE), lambda i: (0, 0)),
        ],
        out_shape=[
            jax.ShapeDtypeStruct((n, d), BF16),
            jax.ShapeDtypeStruct((n * slabs, LANE), F32),
            jax.ShapeDtypeStruct((TOP_K, n), I32),
            jax.ShapeDtypeStruct((TOP_K, n), F32),
            jax.ShapeDtypeStruct((TOP_K, n), I32),
            jax.ShapeDtypeStruct((N_EXPERTS, LANE), I32),
        ],
        scratch_shapes=[pltpu.VMEM((N_EXPERTS, 1), F32)],
        compiler_params=_params("arbitrary"),
    )(x, g.reshape(1, d), mod, wh, wl, rb.reshape(N_EXPERTS, 1), tri)


def _dispatch_kernel(dest_ref, pend_ref, pc_ref, h_ref, xs_ref, zero_ref, sem, zsem, *, tm, n, slabs):
    i = pl.program_id(0)
    rows_blk = MOE_BLK * slabs

    def tail_copy(e):
        start = pl.multiple_of((pend_ref[e] - MOE_BLK) * slabs, rows_blk)
        return pltpu.make_async_copy(zero_ref, xs_ref.at[pl.ds(start, rows_blk), :], zsem)

    @pl.when(i == 0)
    def _():
        zero_ref[...] = jnp.zeros(zero_ref.shape, F32)

        def zstart(e, c):
            @pl.when(pc_ref[e] > 0)
            def _():
                tail_copy(e).start()
            return c

        def zwait(e, c):
            @pl.when(pc_ref[e] > 0)
            def _():
                tail_copy(e).wait()
            return c

        lax.fori_loop(0, N_EXPERTS, zstart, 0)
        lax.fori_loop(0, N_EXPERTS, zwait, 0)

    def row_copy(r, k):
        src = h_ref.at[pl.ds(pl.multiple_of(r * slabs, slabs), slabs), :]
        dst_row = dest_ref[k * n + i * tm + r]
        dst = xs_ref.at[pl.ds(pl.multiple_of(dst_row * slabs, slabs), slabs), :]
        return pltpu.make_async_copy(src, dst, sem)

    def start(r, c):
        for k in range(TOP_K):
            row_copy(r, k).start()
        return c

    lax.fori_loop(0, tm, start, 0)
    for _ in range(TOP_K):
        pltpu.make_async_copy(h_ref, xs_ref.at[pl.ds(0, tm * slabs), :], sem).wait()


def _dispatch(dest, pend, pc, h2g, n_rows, seq):
    n = dest.shape[0] // TOP_K
    slabs = h2g.shape[0] // n
    tm = min(256, seq)
    kern = functools.partial(_dispatch_kernel, tm=tm, n=n, slabs=slabs)
    return pl.pallas_call(
        kern,
        grid_spec=pltpu.PrefetchScalarGridSpec(
            num_scalar_prefetch=3,
            grid=(n // tm,),
            in_specs=[pl.BlockSpec((tm * slabs, LANE), lambda i, *_: (i, 0))],
            out_specs=pl.BlockSpec(memory_space=pl.ANY),
            scratch_shapes=[
                pltpu.VMEM((MOE_BLK * slabs, LANE), F32),
                pltpu.SemaphoreType.DMA(()),
                pltpu.SemaphoreType.DMA(()),
            ],
        ),
        out_shape=jax.ShapeDtypeStruct((n_rows * slabs, LANE), F32),
        compiler_params=_params("arbitrary"),
    )(dest, pend, pc, h2g)


def _gmm_kernel(be_ref, nu_ref, x_ref, wg_ref, wu_ref, wd_ref, o_ref, *, slabs):
    @pl.when(pl.program_id(0) < nu_ref[0])
    def _():
        x = jnp.concatenate(
            [x_ref[pl.ds(s, MOE_BLK, stride=slabs), :].astype(BF16) for s in range(slabs)], axis=1)
        hg = jnp.dot(x, wg_ref[...], preferred_element_type=F32)
        hu = jnp.dot(x, wu_ref[...], preferred_element_type=F32)
        hb = (jax.nn.silu(hg) * hu).astype(BF16)
        y = jnp.dot(hb, wd_ref[...], preferred_element_type=F32)
        for s in range(slabs):
            o_ref[pl.ds(s, MOE_BLK, stride=slabs), :] = y[:, s * LANE:(s + 1) * LANE]


def _gmm(blk_e, nused, xs, wg, wu, wd):
    _, d, de = wg.shape
    slabs = d // LANE
    nblk = xs.shape[0] // (MOE_BLK * slabs)
    kern = functools.partial(_gmm_kernel, slabs=slabs)

    def blk(b, be, nu):
        return jnp.minimum(b, nu[0] - 1)

    return pl.pallas_call(
        kern,
        grid_spec=pltpu.PrefetchScalarGridSpec(
            num_scalar_prefetch=2,
            grid=(nblk,),
            in_specs=[
                pl.BlockSpec((MOE_BLK * slabs, LANE), lambda b, be, nu: (blk(b, be, nu), 0)),
                pl.BlockSpec((None, d, de), lambda b, be, nu: (be[blk(b, be, nu)], 0, 0)),
                pl.BlockSpec((None, d, de), lambda b, be, nu: (be[blk(b, be, nu)], 0, 0)),
                pl.BlockSpec((None, de, d), lambda b, be, nu: (be[blk(b, be, nu)], 0, 0)),
            ],
            out_specs=pl.BlockSpec((MOE_BLK * slabs, LANE), lambda b, be, nu: (blk(b, be, nu), 0)),
        ),
        out_shape=jax.ShapeDtypeStruct(xs.shape, F32),
        compiler_params=_params("arbitrary"),
    )(blk_e, nused, xs, wg, wu, wd)


def _combine_kernel(dest_ref, wt_ref, x_ref, h2_ref, g2_ref, swg_ref, swu_ref, swd_ref, ys_ref, o_ref,
                    buf_ref, sem, *, tm, n, slabs, nt):
    i = pl.program_id(0)
    slot = i % 2
    rows = tm * slabs

    def gather(tile, sl):
        def start(r, c):
            for k in range(TOP_K):
                src_row = dest_ref[k * n + tile * tm + r]
                src = ys_ref.at[pl.ds(pl.multiple_of(src_row * slabs, slabs), slabs), :]
                dst = buf_ref.at[sl, pl.ds(pl.multiple_of(k * rows + r * slabs, slabs), slabs), :]
                pltpu.make_async_copy(src, dst, sem.at[sl]).start()
            return c

        lax.fori_loop(0, tm, start, 0)

    @pl.when(i == 0)
    def _():
        gather(0, 0)

    @pl.when(i + 1 < nt)
    def _():
        gather(i + 1, 1 - slot)

    h2 = h2_ref[...]
    hs = (jax.nn.silu(jnp.dot(h2, swg_ref[...], preferred_element_type=F32))
          * jnp.dot(h2, swu_ref[...], preferred_element_type=F32)).astype(BF16)
    shared = jnp.dot(hs, swd_ref[...], preferred_element_type=F32)
    pltpu.make_async_copy(ys_ref.at[pl.ds(0, TOP_K * rows), :], buf_ref.at[slot], sem.at[slot]).wait()

    wt = wt_ref[...]
    for s in range(slabs):
        c0 = s * LANE
        routed = None
        for k in range(TOP_K):
            t = wt[:, k:k + 1] * buf_ref[slot, pl.ds(k * rows + s, tm, stride=slabs), :]
            routed = t if routed is None else routed + t
        o_ref[:, c0:c0 + LANE] = (x_ref[:, c0:c0 + LANE]
                                  + g2_ref[:, c0:c0 + LANE] * (routed + shared[:, c0:c0 + LANE]))


def _combine(dest, wt, x, h2, g2, swg, swu, swd, ys, seq):
    n, d = x.shape
    slabs = d // LANE
    ds_ = swg.shape[1]
    tm = min(128, seq)
    tps = seq // tm
    kern = functools.partial(_combine_kernel, tm=tm, n=n, slabs=slabs, nt=n // tm)
    return pl.pallas_call(
        kern,
        grid_spec=pltpu.PrefetchScalarGridSpec(
            num_scalar_prefetch=1,
            grid=(n // tm,),
            in_specs=[
                pl.BlockSpec((tm, TOP_K), lambda i, *_: (i, 0)),
                pl.BlockSpec((tm, d), lambda i, *_: (i, 0)),
                pl.BlockSpec((tm, d), lambda i, *_: (i, 0)),
                pl.BlockSpec((None, 1, d), lambda i, *_: (i // tps, 0, 0)),
                pl.BlockSpec((d, ds_), lambda i, *_: (0, 0)),
                pl.BlockSpec((d, ds_), lambda i, *_: (0, 0)),
                pl.BlockSpec((ds_, d), lambda i, *_: (0, 0)),
                pl.BlockSpec(memory_space=pl.ANY),
            ],
            out_specs=pl.BlockSpec((tm, d), lambda i, *_: (i, 0)),
            scratch_shapes=[
                pltpu.VMEM((2, TOP_K * tm * slabs, LANE), F32),
                pltpu.SemaphoreType.DMA((2,)),
            ],
        ),
        out_shape=jax.ShapeDtypeStruct((n, d), F32),
        compiler_params=_params("arbitrary"),
    )(dest, wt, x, h2, g2, swg, swu, swd, ys)


def _final_norm_kernel(x_ref, g_ref, o_ref):
    x = x_ref[...]
    o_ref[...] = x * lax.rsqrt(jnp.mean(x * x, axis=-1, keepdims=True) + EPS) * g_ref[...]


def _final_norm(x, g, seq):
    n, d = x.shape
    tm = min(512, seq)
    return pl.pallas_call(
        _final_norm_kernel,
        grid=(n // tm,),
        in_specs=[pl.BlockSpec((tm, d), lambda i: (i, 0)), pl.BlockSpec((1, d), lambda i: (0, 0))],
        out_specs=pl.BlockSpec((tm, d), lambda i: (i, 0)),
        out_shape=jax.ShapeDtypeStruct((n, d), F32),
        compiler_params=_params("parallel"),
    )(x, g.reshape(1, d))


def _swap_halves(w):
    half = w.shape[-1] // 2
    return jnp.concatenate([w[..., half:], w[..., :half]], axis=-1)


def _prep_layer(l, p):
    d = p['w_in'].shape[1]
    sizes = [HA * 2 * DHA, HA * 2 * DHA, HA * DVA, 2 * BRANCH_W, 2 * LRU_W, Q_LORA, KV_LORA, D_ROPE, N_BRANCH * d]
    offs = np.cumsum([0] + sizes)
    w = p['w_in'][l]
    qa, ka, va, uv, xg, cq, ckv, kr, gl = [w[:, offs[i]:offs[i + 1]] for i in range(len(sizes))]
    pad = jnp.zeros((d, 512 - KV_LORA - 2 * D_ROPE), w.dtype)
    w_att = jnp.concatenate([qa * (DHA ** -0.5 * LOG2E), ka, va], axis=1).astype(BF16)
    w_rest = jnp.concatenate([uv, xg, cq, ckv, kr, _swap_halves(kr), pad, gl], axis=1).astype(BF16)

    wuq = p['w_uq'][l].reshape(Q_LORA, HD, D_NOPE + D_ROPE)
    wuq_r = wuq[:, :, D_NOPE:]
    wuq_ext = jnp.concatenate([wuq[:, :, :D_NOPE], wuq_r, _swap_halves(wuq_r)], axis=-1)
    wuq_ext = wuq_ext.reshape(Q_LORA, HD * 2 * LANE).astype(BF16)

    lam_init = 0.8 - 0.6 * math.exp(-0.3 * l)
    lam = (jnp.exp(jnp.sum(p['lam_q1'][l] * p['lam_k1'][l])) - jnp.exp(jnp.sum(p['lam_q2'][l] * p['lam_k2'][l]))
           + lam_init)
    slopes = jnp.asarray(2.0 ** (-8.0 * np.arange(1, HA + 1) / HA) * LOG2E, F32)
    par = jnp.concatenate([slopes, lam.reshape(1), jnp.full((1,), 1.0 - lam_init, F32), jnp.zeros((2,), F32)])

    rw = p['router_w'][l].T
    rw_hi = rw.astype(BF16)
    rw_lo = (rw - rw_hi.astype(F32)).astype(BF16)
    bs = jnp.repeat(p['sgu_b'][l].T, LANE, axis=1)
    return dict(
        w_att=w_att, w_rest=w_rest, wuq=wuq_ext, wukv=p['w_ukv'][l].astype(BF16), par=par,
        sgu_w=p['sgu_w'][l].astype(BF16), sgu_bs=bs,
        rg_wr=p['rg_wr'][l].astype(BF16), rg_wi=p['rg_wi'][l].astype(BF16),
        w_branch=p['w_branch'][l].astype(BF16), w_out=p['w_out'][l].astype(BF16),
        rw_hi=rw_hi, rw_lo=rw_lo,
        exp_wg=p['exp_wg'][l].astype(BF16), exp_wu=p['exp_wu'][l].astype(BF16), exp_wd=p['exp_wd'][l].astype(BF16),
        sh_wg=p['sh_wg'][l].astype(BF16), sh_wu=p['sh_wu'][l].astype(BF16), sh_wd=p['sh_wd'][l].astype(BF16),
    )


def _rope_table(n_seq, seq):
    inv_freq = ROPE_THETA ** (-jnp.arange(0, D_ROPE, 2, dtype=F32) / D_ROPE)
    ang = jnp.arange(seq, dtype=F32)[:, None] * inv_freq[None, :]
    cos, sin = jnp.cos(ang), jnp.sin(ang)
    tab = jnp.concatenate([cos, cos, -sin, sin], axis=1)
    return jnp.tile(tab, (n_seq, 1))


def _trunk(x3, mods, p, preps):
    n_seq, seq, d = x3.shape
    n = n_seq * seq
    x = x3.reshape(n, d)
    tab = _rope_table(n_seq, seq)
    depth = len(preps)
    for l in range(depth):
        w = preps[l]
        mod = mods[l]
        mod1 = jnp.stack([mod[:, 1], mod[:, 0]], axis=1)
        mod2 = jnp.stack([mod[:, 4], mod[:, 3]], axis=1)
        g1 = mod[:, 2:3]
        g2 = mod[:, 5:6]

        att = _inproj(x, p['norm1_g'][l], mod1, w['w_att'], BF16, seq)
        rest = _inproj(x, p['norm1_g'][l], mod1, w['w_rest'], F32, seq)
        ya = _diff_attn(att, w['par'], p['subln_g'][l], n_seq, seq)
        yb = _sgu(rest, p['sgu_ln_g'][l], p['sgu_ln_b'][l], w['sgu_w'], w['sgu_bs'], seq)
        hf, hr = _lru(rest, p['conv_w'][l], p['conv_b'][l], w['rg_wr'], p['rg_br'][l], w['rg_wi'],
                      p['rg_bi'][l], p['rg_lam'][l], seq)
        q, k, v = _mla_prep(rest, tab, p['q_norm_g'][l], p['kv_norm_g'][l], w['wuq'], w['wukv'], seq)
        yd = _mla_attn(q, k, v, n_seq, seq)
        merged = _merge(ya, yb, hf, hr, rest, yd, w['w_branch'], seq)
        x = _outproj(merged, w['w_out'], x, g1, seq)

        h2, h2g, eidx, wgt, rank, cnt = _router(x, p['norm2_g'][l], mod2, w['rw_hi'], w['rw_lo'],
                                                p['router_bias'][l], seq)
        counts = cnt[:, 0]
        pc = ((counts + MOE_BLK - 1) // MOE_BLK) * MOE_BLK
        pend = jnp.cumsum(pc).astype(I32)
        pstart = pend - pc
        eid = jnp.arange(N_EXPERTS, dtype=I32)
        dest = (jnp.sum(jnp.where(eidx[..., None] == eid, pstart, 0), axis=-1) + rank).reshape(-1).astype(I32)
        nblk = n * TOP_K // MOE_BLK + N_EXPERTS
        blk_row = jnp.arange(nblk, dtype=I32)[:, None] * MOE_BLK
        blk_e = jnp.minimum(jnp.sum((pend[None, :] <= blk_row).astype(I32), axis=1), N_EXPERTS - 1)
        nused = (pend[-1:] // MOE_BLK).astype(I32)
        xs = _dispatch(dest, pend, pc.astype(I32), h2g, nblk * MOE_BLK, seq)
        ys = _gmm(blk_e, nused, xs, w['exp_wg'], w['exp_wu'], w['exp_wd'])
        x = _combine(dest, wgt.T, x, h2, g2, w['sh_wg'], w['sh_wu'], w['sh_wd'], ys, seq)
    return _final_norm(x, p['final_g'], seq).reshape(n_seq, seq, d)


def kernel(x_prompt, x_sample, c_prompt, c_sample, ada_w, ada_b, norm1_g, norm2_g, w_in, lam_q1, lam_k1, lam_q2, lam_k2, subln_g, sgu_ln_g, sgu_ln_b, sgu_w, sgu_b, conv_w, conv_b, rg_wr, rg_br, rg_wi, rg_bi, rg_lam, q_norm_g, kv_norm_g, w_uq, w_ukv, w_branch, w_out, router_w, router_bias, exp_wg, exp_wu, exp_wd, sh_wg, sh_wu, sh_wd, final_g):
    p = dict(ada_w=ada_w, ada_b=ada_b, norm1_g=norm1_g, norm2_g=norm2_g, w_in=w_in,
             lam_q1=lam_q1, lam_k1=lam_k1, lam_q2=lam_q2, lam_k2=lam_k2, subln_g=subln_g,
             sgu_ln_g=sgu_ln_g, sgu_ln_b=sgu_ln_b, sgu_w=sgu_w, sgu_b=sgu_b,
             conv_w=conv_w, conv_b=conv_b, rg_wr=rg_wr, rg_br=rg_br, rg_wi=rg_wi, rg_bi=rg_bi,
             rg_lam=rg_lam, q_norm_g=q_norm_g, kv_norm_g=kv_norm_g, w_uq=w_uq, w_ukv=w_ukv,
             w_branch=w_branch, w_out=w_out, router_w=router_w, router_bias=router_bias,
             exp_wg=exp_wg, exp_wu=exp_wu, exp_wd=exp_wd, sh_wg=sh_wg, sh_wu=sh_wu, sh_wd=sh_wd,
             final_g=final_g)
    depth, d, _ = ada_w.shape
    bp, bs = x_prompt.shape[0], x_sample.shape[0]
    assert bp + bs <= 8
    c8 = jnp.concatenate([c_prompt, c_sample, jnp.zeros((8 - bp - bs, d), F32)], axis=0)
    mod = _ada_mod(c8, ada_w, ada_b).reshape(depth, 8, 6, d)
    preps = [_prep_layer(l, p) for l in range(depth)]
    y_prompt = _trunk(x_prompt, mod[:, :bp], p, preps)
    y_sample = _trunk(x_sample, mod[:, bp:bp + bs], p, preps)
    return (y_prompt, y_sample)
```

```python
import functools
import math

import numpy as np
import jax
import jax.numpy as jnp
from jax import lax
from jax.experimental import pallas as pl
from jax.experimental.pallas import tpu as pltpu

F32 = jnp.float32
BF16 = jnp.bfloat16
I32 = jnp.int32

EPS = 1e-6
LOG2E = math.log2(math.e)
LANE = 128
SUBLANE = 8
VMEM_LIMIT = 48 * 1024 * 1024

HA = 4
DHA = 64
DVA = 2 * DHA
BRANCH_W = 512
SGU_CHUNK = 128
SGU_GROUPS = 4
LRU_W = 512
LRU_BLOCKS = 4
LRU_BW = LRU_W // LRU_BLOCKS
LRU_C = 8.0
HD = 4
Q_LORA = 512
KV_LORA = 256
D_NOPE = 128
D_ROPE = 64
DV_D = 128
ROPE_THETA = 10000.0
N_EXPERTS = 64
TOP_K = 8
N_GROUPS = 8
TOPK_GROUPS = 4
GROUP_SZ = N_EXPERTS // N_GROUPS
ROUTE_SCALE = 2.5
MOE_BLK = 256
N_BRANCH = 4
COMBINE_PITCH = 24

R_UV = 0
R_XC = 1024
R_GATE = 1536
R_CQ = 2048
R_CKV = 2560
R_GL = 3072

NT_DIMS = (((1,), (1,)), ((), ()))


def _params(*sem):
    return pltpu.CompilerParams(dimension_semantics=sem, vmem_limit_bytes=VMEM_LIMIT)


def _ada_kernel(c_ref, w_ref, b_ref, o_ref):
    a = jax.nn.silu(c_ref[...]).astype(BF16)
    o_ref[...] = jnp.dot(a, w_ref[...].astype(BF16), preferred_element_type=F32) + b_ref[...]


def _ada_mod(c8, ada_w, ada_b):
    depth, d, n = ada_w.shape
    tn = 1024
    return pl.pallas_call(
        _ada_kernel,
        grid=(depth, n // tn),
        in_specs=[
            pl.BlockSpec((8, d), lambda l, j: (0, 0)),
            pl.BlockSpec((None, d, tn), lambda l, j: (l, 0, j)),
            pl.BlockSpec((None, 1, tn), lambda l, j: (l, 0, j)),
        ],
        out_specs=pl.BlockSpec((None, 8, tn), lambda l, j: (l, 0, j)),
        out_shape=jax.ShapeDtypeStruct((depth, 8, n), F32),
        compiler_params=_params("parallel", "parallel"),
    )(c8, ada_w, ada_b.reshape(depth, 1, n))


def _inproj_kernel(x_ref, g_ref, mod_ref, w_ref, o_ref, h_ref):
    @pl.when(pl.program_id(1) == 0)
    def _():
        x = x_ref[...]
        y = x * lax.rsqrt(jnp.mean(x * x, axis=-1, keepdims=True) + EPS) * g_ref[...]
        h_ref[...] = (y * (1.0 + mod_ref[0:1, :]) + mod_ref[1:2, :]).astype(BF16)

    o_ref[...] = jnp.dot(h_ref[...], w_ref[...], preferred_element_type=F32).astype(o_ref.dtype)


def _inproj(x, g, mod, w, out_dtype, seq):
    n, d = x.shape
    nc = w.shape[1]
    tm = min(1024, seq)
    tn = 512
    tps = seq // tm
    return pl.pallas_call(
        _inproj_kernel,
        grid=(n // tm, nc // tn),
        in_specs=[
            pl.BlockSpec((tm, d), lambda i, j: (i, 0)),
            pl.BlockSpec((1, d), lambda i, j: (0, 0)),
            pl.BlockSpec((None, 2, d), lambda i, j: (i // tps, 0, 0)),
            pl.BlockSpec((d, tn), lambda i, j: (0, j)),
        ],
        out_specs=pl.BlockSpec((tm, tn), lambda i, j: (i, j)),
        out_shape=jax.ShapeDtypeStruct((n, nc), out_dtype),
        scratch_shapes=[pltpu.VMEM((tm, d), BF16)],
        compiler_params=_params("parallel", "arbitrary"),
    )(x, g.reshape(1, d), mod, w)


def _online_softmax_step(s, vx, m_ref, acc_ref, idx, tk):
    m_prev = m_ref[idx]
    m_new = jnp.maximum(m_prev, jnp.max(s, axis=-1, keepdims=True))
    alpha = jnp.exp2(m_prev - m_new)
    p = jnp.exp2(s - jnp.concatenate([m_new] * (tk // LANE), axis=1))
    pv = jnp.dot(p.astype(BF16), vx, preferred_element_type=F32)
    acc_ref[idx] = jnp.concatenate([alpha, alpha], axis=1) * acc_ref[idx] + pv
    m_ref[idx] = m_new


def _diff_attn_kernel(par_ref, q_ref, k_ref, v_ref, g_ref, o_ref, m_ref, acc_ref, *, tq, tk, nk):
    qi = pl.program_id(1)
    ki = pl.program_id(2)

    @pl.when(ki == 0)
    def _():
        m_ref[...] = jnp.full(m_ref.shape, -jnp.inf, F32)
        acc_ref[...] = jnp.zeros(acc_ref.shape, F32)

    lane = lax.broadcasted_iota(I32, (tq, LANE), 1)
    row = lax.broadcasted_iota(I32, (tq, tk), 0)
    col = lax.broadcasted_iota(I32, (tq, tk), 1)
    dist = jnp.abs(row - col + (qi * tq - ki * tk)).astype(F32)
    ones = jnp.ones((tk, LANE), BF16)
    for h in range(HA):
        q = q_ref[:, h * LANE:(h + 1) * LANE]
        k = k_ref[:, h * LANE:(h + 1) * LANE]
        vx = jnp.concatenate([v_ref[:, h * LANE:(h + 1) * LANE], ones], axis=1)
        bias = dist * par_ref[h]
        zero = jnp.zeros_like(q)
        for mi, qm in enumerate((jnp.where(lane < DHA, q, zero), jnp.where(lane < DHA, zero, q))):
            s = lax.dot_general(qm, k, NT_DIMS, preferred_element_type=F32) - bias
            _online_softmax_step(s, vx, m_ref, acc_ref, 2 * h + mi, tk)

    @pl.when(ki == nk - 1)
    def _():
        lam = par_ref[HA]
        for h in range(HA):
            a1 = acc_ref[2 * h]
            a2 = acc_ref[2 * h + 1]
            o = a1[:, :LANE] / a1[:, LANE:] - lam * (a2[:, :LANE] / a2[:, LANE:])
            y = o * lax.rsqrt(jnp.mean(o * o, axis=-1, keepdims=True) + EPS) * g_ref[...]
            o_ref[:, h * LANE:(h + 1) * LANE] = (y * par_ref[HA + 1]).astype(o_ref.dtype)


def _diff_attn(att, par, subln_g, n_seq, seq):
    n = att.shape[0]
    tq = min(512, seq)
    tk = min(1024, seq)
    nq, nk = seq // tq, seq // tk
    w = HA * LANE
    kern = functools.partial(_diff_attn_kernel, tq=tq, tk=tk, nk=nk)
    return pl.pallas_call(
        kern,
        grid=(n_seq, nq, nk),
        in_specs=[
            pl.BlockSpec(memory_space=pltpu.SMEM),
            pl.BlockSpec((tq, w), lambda b, qi, ki: (b * nq + qi, 0)),
            pl.BlockSpec((tk, w), lambda b, qi, ki: (b * nk + ki, 1)),
            pl.BlockSpec((tk, w), lambda b, qi, ki: (b * nk + ki, 2)),
            pl.BlockSpec((1, DVA), lambda b, qi, ki: (0, 0)),
        ],
        out_specs=pl.BlockSpec((tq, w), lambda b, qi, ki: (b * nq + qi, 0)),
        out_shape=jax.ShapeDtypeStruct((n, HA * DVA), BF16),
        scratch_shapes=[
            pltpu.VMEM((2 * HA, tq, LANE), F32),
            pltpu.VMEM((2 * HA, tq, 2 * LANE), F32),
        ],
        compiler_params=_params("parallel", "parallel", "arbitrary"),
    )(par, att, att, att, subln_g.reshape(1, DVA))


def _mla_prep_kernel(cq_ref, ckv_ref, tab_ref, gq_ref, gkv_ref, wuq_ref, wukv_ref, q_ref, k_ref, v_ref, *, scale):
    cq = cq_ref[...]
    cqn = (cq * lax.rsqrt(jnp.mean(cq * cq, axis=-1, keepdims=True) + EPS) * gq_ref[...]).astype(BF16)
    c = ckv_ref[...]
    ckv = c[:, :KV_LORA]
    ckvn = (ckv * lax.rsqrt(jnp.mean(ckv * ckv, axis=-1, keepdims=True) + EPS) * gkv_ref[...]).astype(BF16)
    tab = tab_ref[...]
    lane = lax.broadcasted_iota(I32, tab.shape, 1)

    def rope(pair):
        pr = pair * tab
        return jnp.where(lane < D_ROPE, pr + pltpu.roll(pr, D_ROPE, 1), 0.0)

    kr = rope(c[:, KV_LORA:KV_LORA + LANE]).astype(BF16)
    qf = jnp.dot(cqn, wuq_ref[...], preferred_element_type=F32)
    kvf = jnp.dot(ckvn, wukv_ref[...], preferred_element_type=F32)
    for h in range(HD):
        b0 = h * 2 * LANE
        q_ref[:, b0:b0 + LANE] = (qf[:, b0:b0 + LANE] * scale).astype(BF16)
        q_ref[:, b0 + LANE:b0 + 2 * LANE] = (rope(qf[:, b0 + LANE:b0 + 2 * LANE]) * scale).astype(BF16)
        k_ref[:, b0:b0 + LANE] = kvf[:, b0:b0 + LANE].astype(BF16)
        k_ref[:, b0 + LANE:b0 + 2 * LANE] = kr
        v_ref[:, h * LANE:(h + 1) * LANE] = kvf[:, b0 + LANE:b0 + 2 * LANE].astype(BF16)


def _mla_prep(rest, tab, gq, gkv, wuq, wukv, seq):
    n = rest.shape[0]
    tm = min(512, seq)
    scale = (D_NOPE + D_ROPE) ** -0.5 * LOG2E
    kern = functools.partial(_mla_prep_kernel, scale=scale)
    wq = HD * 2 * LANE
    return pl.pallas_call(
        kern,
        grid=(n // tm,),
        in_specs=[
            pl.BlockSpec((tm, Q_LORA), lambda i: (i, R_CQ // Q_LORA)),
            pl.BlockSpec((tm, 512), lambda i: (i, R_CKV // 512)),
            pl.BlockSpec((tm, LANE), lambda i: (i, 0)),
            pl.BlockSpec((1, Q_LORA), lambda i: (0, 0)),
            pl.BlockSpec((1, KV_LORA), lambda i: (0, 0)),
            pl.BlockSpec((Q_LORA, wq), lambda i: (0, 0)),
            pl.BlockSpec((KV_LORA, wq), lambda i: (0, 0)),
        ],
        out_specs=[
            pl.BlockSpec((tm, wq), lambda i: (i, 0)),
            pl.BlockSpec((tm, wq), lambda i: (i, 0)),
            pl.BlockSpec((tm, HD * DV_D), lambda i: (i, 0)),
        ],
        out_shape=[
            jax.ShapeDtypeStruct((n, wq), BF16),
            jax.ShapeDtypeStruct((n, wq), BF16),
            jax.ShapeDtypeStruct((n, HD * DV_D), BF16),
        ],
        compiler_params=_params("parallel"),
    )(rest, rest, tab, gq.reshape(1, -1), gkv.reshape(1, -1), wuq, wukv)


def _mla_attn_kernel(q_ref, k_ref, v_ref, o_ref, m_ref, acc_ref, *, tk, nk):
    ki = pl.program_id(2)

    @pl.when(ki == 0)
    def _():
        m_ref[...] = jnp.full(m_ref.shape, -jnp.inf, F32)
        acc_ref[...] = jnp.zeros(acc_ref.shape, F32)

    ones = jnp.ones((tk, LANE), BF16)
    for h in range(HD):
        q = q_ref[:, h * 2 * LANE:(h + 1) * 2 * LANE]
        k = k_ref[:, h * 2 * LANE:(h + 1) * 2 * LANE]
        vx = jnp.concatenate([v_ref[:, h * LANE:(h + 1) * LANE], ones], axis=1)
        s = lax.dot_general(q, k, NT_DIMS, preferred_element_type=F32)
        _online_softmax_step(s, vx, m_ref, acc_ref, h, tk)

    @pl.when(ki == nk - 1)
    def _():
        for h in range(HD):
            a = acc_ref[h]
            o_ref[:, h * LANE:(h + 1) * LANE] = (a[:, :LANE] / a[:, LANE:]).astype(o_ref.dtype)


def _mla_attn(q, k, v, n_seq, seq):
    n = q.shape[0]
    tq = min(512, seq)
    tk = min(1024, seq)
    nq, nk = seq // tq, seq // tk
    wq = HD * 2 * LANE
    kern = functools.partial(_mla_attn_kernel, tk=tk, nk=nk)
    return pl.pallas_call(
        kern,
        grid=(n_seq, nq, nk),
        in_specs=[
            pl.BlockSpec((tq, wq), lambda b, qi, ki: (b * nq + qi, 0)),
            pl.BlockSpec((tk, wq), lambda b, qi, ki: (b * nk + ki, 0)),
            pl.BlockSpec((tk, HD * DV_D), lambda b, qi, ki: (b * nk + ki, 0)),
        ],
        out_specs=pl.BlockSpec((tq, HD * DV_D), lambda b, qi, ki: (b * nq + qi, 0)),
        out_shape=jax.ShapeDtypeStruct((n, HD * DV_D), BF16),
        scratch_shapes=[
            pltpu.VMEM((HD, tq, LANE), F32),
            pltpu.VMEM((HD, tq, 2 * LANE), F32),
        ],
        compiler_params=_params("parallel", "parallel", "arbitrary"),
    )(q, k, v)


def _sgu_kernel(uv_ref, g_ref, b_ref, ws_ref, bs_ref, o_ref, *, tm):
    z = jax.nn.gelu(uv_ref[...])
    u = z[:, :BRANCH_W]
    v = z[:, BRANCH_W:]
    mu = jnp.mean(v, axis=-1, keepdims=True)
    vc = v - mu
    var = jnp.mean(vc * vc, axis=-1, keepdims=True)
    vn = (vc * lax.rsqrt(var + EPS) * g_ref[...] + b_ref[...]).astype(BF16)
    for c in range(tm // SGU_CHUNK):
        r0 = c * SGU_CHUNK
        for g in range(SGU_GROUPS):
            c0 = g * LANE
            vm = jnp.dot(ws_ref[g], vn[r0:r0 + SGU_CHUNK, c0:c0 + LANE], preferred_element_type=F32)
            vm = vm + bs_ref[:, c0:c0 + LANE]
            o_ref[r0:r0 + SGU_CHUNK, c0:c0 + LANE] = (u[r0:r0 + SGU_CHUNK, c0:c0 + LANE] * vm).astype(o_ref.dtype)


def _sgu(rest, ln_g, ln_b, ws, bs, seq):
    n = rest.shape[0]
    tm = min(512, seq)
    kern = functools.partial(_sgu_kernel, tm=tm)
    return pl.pallas_call(
        kern,
        grid=(n // tm,),
        in_specs=[
            pl.BlockSpec((tm, 2 * BRANCH_W), lambda i: (i, R_UV // (2 * BRANCH_W))),
            pl.BlockSpec((1, BRANCH_W), lambda i: (0, 0)),
            pl.BlockSpec((1, BRANCH_W), lambda i: (0, 0)),
            pl.BlockSpec((SGU_GROUPS, SGU_CHUNK, SGU_CHUNK), lambda i: (0, 0, 0)),
            pl.BlockSpec((SGU_CHUNK, BRANCH_W), lambda i: (0, 0)),
        ],
        out_specs=pl.BlockSpec((tm, BRANCH_W), lambda i: (i, 0)),
        out_shape=jax.ShapeDtypeStruct((n, BRANCH_W), BF16),
        compiler_params=_params("parallel"),
    )(rest, ln_g.reshape(1, -1), ln_b.reshape(1, -1), ws, bs)


def _lru_kernel(xf_ref, xfp_ref, xfn_ref, xr_ref, xrp_ref, xrn_ref, cw_ref, cb_ref, wr_ref, br_ref, wi_ref,
                bi_ref, lam_ref, hf_ref, hr_ref, af_ref, uf_ref, ar_ref, ur_ref, hc_ref, *, tt, tps):
    j = pl.program_id(0) % tps
    rowi = lax.broadcasted_iota(I32, (tt, LRU_W), 0)

    def gates(x_ref, xp_ref, xn_ref, at_start, at_end, d, a_ref, u_ref):
        x = x_ref[...]
        prev = jnp.where(at_start, 0.0, xp_ref[SUBLANE - 1:SUBLANE, :])
        nxt = jnp.where(at_end, 0.0, xn_ref[0:2, :])
        xm1 = jnp.where(rowi == 0, prev, pltpu.roll(x, 1, 0))
        xp1 = jnp.where(rowi == tt - 1, nxt[0:1, :], pltpu.roll(x, tt - 1, 0))
        xp2 = jnp.where(rowi == tt - 2, nxt[0:1, :],
                        jnp.where(rowi == tt - 1, nxt[1:2, :], pltpu.roll(x, tt - 2, 0)))
        xc = xm1 * cw_ref[0:1, :] + x * cw_ref[1:2, :] + xp1 * cw_ref[2:3, :] + xp2 * cw_ref[3:4, :] + cb_ref[...]
        xcb = xc.astype(BF16)

        def blockdiag(w_ref):
            return jnp.concatenate(
                [jnp.dot(xcb[:, g * LRU_BW:(g + 1) * LRU_BW], w_ref[d, g], preferred_element_type=F32)
                 for g in range(LRU_BLOCKS)], axis=1)

        r = jax.nn.sigmoid(blockdiag(wr_ref) + br_ref[d:d + 1, :])
        ig = jax.nn.sigmoid(blockdiag(wi_ref) + bi_ref[d:d + 1, :])
        log_a = (-LRU_C * r) * jax.nn.softplus(-lam_ref[d:d + 1, :])
        a = jnp.exp(log_a)
        u = jnp.sqrt(jnp.tanh(-log_a) * (1.0 + a * a)) * (ig * xc)
        a_ref[...] = a
        u_ref[...] = u

    gates(xf_ref, xfp_ref, xfn_ref, j == 0, j == tps - 1, 0, af_ref, uf_ref)
    gates(xr_ref, xrp_ref, xrn_ref, j == tps - 1, j == 0, 1, ar_ref, ur_ref)

    @pl.when(j == 0)
    def _():
        hc_ref[...] = jnp.zeros(hc_ref.shape, F32)

    ng = tt // SUBLANE
    sub = lax.broadcasted_iota(I32, (SUBLANE, LRU_W), 0)

    def body(g, carry):
        hf_prev, hr_prev = carry
        r0 = pl.multiple_of(g * SUBLANE, SUBLANE)
        a = af_ref[pl.ds(r0, SUBLANE), :]
        u = uf_ref[pl.ds(r0, SUBLANE), :]
        for s in (1, 2, 4):
            a_s = jnp.where(sub >= s, pltpu.roll(a, s, 0), 1.0)
            u_s = jnp.where(sub >= s, pltpu.roll(u, s, 0), 0.0)
            u = a * u_s + u
            a = a * a_s
        h = a * hf_prev + u
        hf_ref[pl.ds(r0, SUBLANE), :] = h
        hf_new = jnp.broadcast_to(h[SUBLANE - 1:SUBLANE, :], (SUBLANE, LRU_W))

        r1 = pl.multiple_of((ng - 1 - g) * SUBLANE, SUBLANE)
        a = ar_ref[pl.ds(r1, SUBLANE), :]
        u = ur_ref[pl.ds(r1, SUBLANE), :]
        for s in (1, 2, 4):
            a_s = jnp.where(sub < SUBLANE - s, pltpu.roll(a, SUBLANE - s, 0), 1.0)
            u_s = jnp.where(sub < SUBLANE - s, pltpu.roll(u, SUBLANE - s, 0), 0.0)
            u = a * u_s + u
            a = a * a_s
        h = a * hr_prev + u
        hr_ref[pl.ds(r1, SUBLANE), :] = h
        hr_new = jnp.broadcast_to(h[0:1, :], (SUBLANE, LRU_W))
        return hf_new, hr_new

    hf_c, hr_c = lax.fori_loop(0, ng, body, (hc_ref[0], hc_ref[1]))
    hc_ref[0] = hf_c
    hc_ref[1] = hr_c


def _lru(rest, cw, cb, wr, br, wi, bi, lam, seq):
    n = rest.shape[0]
    tt = min(512, seq)
    tps = seq // tt
    nt = n // tt
    t8 = tt // SUBLANE
    last8 = n // SUBLANE - 1
    cblk = R_XC // LRU_W

    def rev(i):
        return (i // tps) * tps + (tps - 1 - i % tps)

    kern = functools.partial(_lru_kernel, tt=tt, tps=tps)
    full = lambda shape: pl.BlockSpec(shape, lambda i: (0,) * len(shape))
    return pl.pallas_call(
        kern,
        grid=(nt,),
        in_specs=[
            pl.BlockSpec((tt, LRU_W), lambda i: (i, cblk)),
            pl.BlockSpec((SUBLANE, LRU_W), lambda i: (jnp.maximum(i * t8 - 1, 0), cblk)),
            pl.BlockSpec((SUBLANE, LRU_W), lambda i: (jnp.minimum((i + 1) * t8, last8), cblk)),
            pl.BlockSpec((tt, LRU_W), lambda i: (rev(i), cblk)),
            pl.BlockSpec((SUBLANE, LRU_W), lambda i: (jnp.maximum(rev(i) * t8 - 1, 0), cblk)),
            pl.BlockSpec((SUBLANE, LRU_W), lambda i: (jnp.minimum((rev(i) + 1) * t8, last8), cblk)),
            full((4, LRU_W)),
            full((1, LRU_W)),
            full((2, LRU_BLOCKS, LRU_BW, LRU_BW)),
            full((2, LRU_W)),
            full((2, LRU_BLOCKS, LRU_BW, LRU_BW)),
            full((2, LRU_W)),
            full((2, LRU_W)),
        ],
        out_specs=[
            pl.BlockSpec((tt, LRU_W), lambda i: (i, 0)),
            pl.BlockSpec((tt, LRU_W), lambda i: (rev(i), 0)),
        ],
        out_shape=[jax.ShapeDtypeStruct((n, LRU_W), F32), jax.ShapeDtypeStruct((n, LRU_W), F32)],
        scratch_shapes=[pltpu.VMEM((tt, LRU_W), F32)] * 4 + [pltpu.VMEM((2, SUBLANE, LRU_W), F32)],
        compiler_params=_params("arbitrary"),
    )(rest, rest, rest, rest, rest, rest, cw, cb.reshape(1, -1), wr, br, wi, bi, lam)


def _merge_kernel(ya_ref, yb_ref, hf_ref, hr_ref, gate_ref, yd_ref, wbr_ref, g0_ref, g1_ref, g2_ref, g3_ref, o_ref):
    yc = (jax.nn.gelu(gate_ref[...]) * (hf_ref[...] + hr_ref[...])).astype(BF16)
    ys = (ya_ref[...], yb_ref[...], yc, yd_ref[...])
    gls = (g0_ref, g1_ref, g2_ref, g3_ref)
    acc = None
    for k in range(N_BRANCH):
        t = jax.nn.sigmoid(gls[k][...]) * jnp.dot(ys[k], wbr_ref[k], preferred_element_type=F32)
        acc = t if acc is None else acc + t
    o_ref[...] = acc.astype(o_ref.dtype)


def _merge(ya, yb, hf, hr, rest, yd, wbr, seq):
    n = ya.shape[0]
    d = wbr.shape[-1]
    tm = min(512, seq)
    tn = 512
    row = lambda w: pl.BlockSpec((tm, w), lambda i, j: (i, 0))

    def gl_spec(k):
        base = (R_GL + k * d) // tn
        return pl.BlockSpec((tm, tn), lambda i, j: (i, base + j))

    return pl.pallas_call(
        _merge_kernel,
        grid=(n // tm, d // tn),
        in_specs=[
            row(BRANCH_W), row(BRANCH_W), row(LRU_W), row(LRU_W),
            pl.BlockSpec((tm, LRU_W), lambda i, j: (i, R_GATE // LRU_W)),
            row(BRANCH_W),
            pl.BlockSpec((N_BRANCH, BRANCH_W, tn), lambda i, j: (0, 0, j)),
            gl_spec(0), gl_spec(1), gl_spec(2), gl_spec(3),
        ],
        out_specs=pl.BlockSpec((tm, tn), lambda i, j: (i, j)),
        out_shape=jax.ShapeDtypeStruct((n, d), BF16),
        compiler_params=_params("parallel", "parallel"),
    )(ya, yb, hf, hr, rest, yd, wbr, rest, rest, rest, rest)


def _outproj_kernel(m_ref, w_ref, x_ref, g_ref, o_ref):
    o_ref[...] = x_ref[...] + g_ref[...] * jnp.dot(m_ref[...], w_ref[...], preferred_element_type=F32)


def _outproj(merged, w, x, g1, seq):
    n, d = x.shape
    tm = min(512, seq)
    tps = seq // tm
    return pl.pallas_call(
        _outproj_kernel,
        grid=(n // tm,),
        in_specs=[
            pl.BlockSpec((tm, d), lambda i: (i, 0)),
            pl.BlockSpec((d, d), lambda i: (0, 0)),
            pl.BlockSpec((tm, d), lambda i: (i, 0)),
            pl.BlockSpec((None, 1, d), lambda i: (i // tps, 0, 0)),
        ],
        out_specs=pl.BlockSpec((tm, d), lambda i: (i, 0)),
        out_shape=jax.ShapeDtypeStruct((n, d), F32),
        compiler_params=_params("parallel"),
    )(merged, w, x, g1)


def _router_kernel(x_ref, g_ref, mod_ref, wh_ref, wl_ref, rb_ref, tri_ref,
                   h2_ref, h2g_ref, e_ref, w_ref, r_ref, cnt_ref, run_ref, *, tm, d):
    @pl.when(pl.program_id(0) == 0)
    def _():
        run_ref[...] = jnp.zeros(run_ref.shape, F32)

    x = x_ref[...]
    y = x * lax.rsqrt(jnp.mean(x * x, axis=-1, keepdims=True) + EPS) * g_ref[...]
    h = y * (1.0 + mod_ref[0:1, :]) + mod_ref[1:2, :]
    hb = h.astype(BF16)
    h2_ref[...] = hb
    slabs = d // LANE
    for s in range(slabs):
        h2g_ref[pl.ds(s, tm, stride=slabs), :] = h[:, s * LANE:(s + 1) * LANE]

    hl = (h - hb.astype(F32)).astype(BF16)
    logits = (lax.dot_general(wh_ref[...], hb, NT_DIMS, preferred_element_type=F32)
              + lax.dot_general(wl_ref[...], hb, NT_DIMS, preferred_element_type=F32)
              + lax.dot_general(wh_ref[...], hl, NT_DIMS, preferred_element_type=F32))
    sc = jax.nn.sigmoid(logits)
    sel = sc + rb_ref[...]

    neg = -jnp.inf
    i8 = lax.broadcasted_iota(I32, (GROUP_SZ, tm), 0).astype(F32)
    rows = []
    for g in range(N_GROUPS):
        blk = sel[g * GROUP_SZ:(g + 1) * GROUP_SZ, :]
        m1 = jnp.max(blk, axis=0, keepdims=True)
        i1 = jnp.min(jnp.where(blk == m1, i8, float(GROUP_SZ)), axis=0, keepdims=True)
        m2 = jnp.max(jnp.where(i8 == i1, neg, blk), axis=0, keepdims=True)
        rows.append(m1 + m2)
    gs = jnp.concatenate(rows, axis=0)
    gi8 = lax.broadcasted_iota(I32, (N_GROUPS, tm), 0).astype(F32)
    gsel = jnp.zeros((N_GROUPS, tm), F32)
    for _ in range(TOPK_GROUPS):
        gm = jnp.max(gs, axis=0, keepdims=True)
        gi = jnp.min(jnp.where(gs == gm, gi8, float(N_GROUPS)), axis=0, keepdims=True)
        hit = gi8 == gi
        gsel = jnp.where(hit, 1.0, gsel)
        gs = jnp.where(hit, neg, gs)
    emask = jnp.concatenate(
        [jnp.broadcast_to(gsel[g:g + 1, :], (GROUP_SZ, tm)) for g in range(N_GROUPS)], axis=0)
    selm = jnp.where(emask > 0.0, sel, neg)

    i64 = lax.broadcasted_iota(I32, (N_EXPERTS, tm), 0).astype(F32)
    chosen = jnp.zeros((N_EXPERTS, tm), F32)
    idxs, wts = [], []
    for _ in range(TOP_K):
        mx = jnp.max(selm, axis=0, keepdims=True)
        ix = jnp.min(jnp.where(selm == mx, i64, float(N_EXPERTS)), axis=0, keepdims=True)
        oh = i64 == ix
        wts.append(jnp.sum(jnp.where(oh, sc, 0.0), axis=0, keepdims=True))
        idxs.append(ix)
        selm = jnp.where(oh, neg, selm)
        chosen = jnp.where(oh, 1.0, chosen)
    wk = jnp.concatenate(wts, axis=0)
    w_ref[...] = wk / jnp.sum(wk, axis=0, keepdims=True) * ROUTE_SCALE
    e_ref[...] = jnp.concatenate(idxs, axis=0).astype(I32)

    before = jnp.dot(chosen.astype(BF16), tri_ref[...], preferred_element_type=F32)
    rank = run_ref[...] + before
    r_ref[...] = jnp.concatenate(
        [jnp.sum(jnp.where(i64 == idxs[k], rank, 0.0), axis=0, keepdims=True) for k in range(TOP_K)],
        axis=0).astype(I32)
    run_ref[...] = run_ref[...] + jnp.sum(chosen, axis=1, keepdims=True)
    cnt_ref[...] = jnp.broadcast_to(run_ref[...], cnt_ref.shape).astype(I32)


def _router(x, g, mod, wh, wl, rb, seq):
    n, d = x.shape
    tm = min(512, seq)
    tps = seq // tm
    slabs = d // LANE
    tri = jnp.triu(jnp.ones((tm, tm), F32), 1).astype(BF16)
    kern = functools.partial(_router_kernel, tm=tm, d=d)
    return pl.pallas_call(
        kern,
        grid=(n // tm,),
        in_specs=[
            pl.BlockSpec((tm, d), lambda i: (i, 0)),
            pl.BlockSpec((1, d), lambda i: (0, 0)),
            pl.BlockSpec((None, 2, d), lambda i: (i // tps, 0, 0)),
            pl.BlockSpec((N_EXPERTS, d), lambda i: (0, 0)),
            pl.BlockSpec((N_EXPERTS, d), lambda i: (0, 0)),
            pl.BlockSpec((N_EXPERTS, 1), lambda i: (0, 0)),
            pl.BlockSpec((tm, tm), lambda i: (0, 0)),
        ],
        out_specs=[
            pl.BlockSpec((tm, d), lambda i: (i, 0)),
            pl.BlockSpec((tm * slabs, LANE), lambda i: (i, 0)),
            pl.BlockSpec((TOP_K, tm), lambda i: (0, i)),
            pl.BlockSpec((TOP_K, tm), lambda i: (0, i)),
            pl.BlockSpec((TOP_K, tm), lambda i: (0, i)),
            pl.BlockSpec((N_EXPERTS, LANE), lambda i: (0, 0)),
        ],
        out_shape=[
            jax.ShapeDtypeStruct((n, d), BF16),
            jax.ShapeDtypeStruct((n * slabs, LANE), F32),
            jax.ShapeDtypeStruct((TOP_K, n), I32),
            jax.ShapeDtypeStruct((TOP_K, n), F32),
            jax.ShapeDtypeStruct((TOP_K, n), I32),
            jax.ShapeDtypeStruct((N_EXPERTS, LANE), I32),
        ],
        scratch_shapes=[pltpu.VMEM((N_EXPERTS, 1), F32)],
        compiler_params=_params("arbitrary"),
    )(x, g.reshape(1, d), mod, wh, wl, rb.reshape(N_EXPERTS, 1), tri)


def _dispatch_kernel(dest_ref, pend_ref, pc_ref, h_ref, xs_ref, zero_ref, sem, zsem, *, tm, n, slabs):
    i = pl.program_id(0)
    rows_blk = MOE_BLK * slabs

    def tail_copy(e):
        start = pl.multiple_of((pend_ref[e] - MOE_BLK) * slabs, rows_blk)
        return pltpu.make_async_copy(zero_ref, xs_ref.at[pl.ds(start, rows_blk), :], zsem)

    @pl.when(i == 0)
    def _():
        zero_ref[...] = jnp.zeros(zero_ref.shape, F32)

        def zstart(e, c):
            @pl.when(pc_ref[e] > 0)
            def _():
                tail_copy(e).start()
            return c

        def zwait(e, c):
            @pl.when(pc_ref[e] > 0)
            def _():
                tail_copy(e).wait()
            return c

        lax.fori_loop(0, N_EXPERTS, zstart, 0)
        lax.fori_loop(0, N_EXPERTS, zwait, 0)

    def row_copy(r, k):
        src = h_ref.at[pl.ds(pl.multiple_of(r * slabs, slabs), slabs), :]
        dst_row = dest_ref[k * n + i * tm + r]
        dst = xs_ref.at[pl.ds(pl.multiple_of(dst_row * slabs, slabs), slabs), :]
        return pltpu.make_async_copy(src, dst, sem)

    def start(r, c):
        for k in range(TOP_K):
            row_copy(r, k).start(priority=k % 2)
        return c

    lax.fori_loop(0, tm, start, 0)
    for _ in range(TOP_K):
        pltpu.make_async_copy(h_ref, xs_ref.at[pl.ds(0, tm * slabs), :], sem).wait()


def _dispatch(dest, pend, pc, h2g, n_rows, seq):
    n = dest.shape[0] // TOP_K
    slabs = h2g.shape[0] // n
    tm = min(256, seq)
    kern = functools.partial(_dispatch_kernel, tm=tm, n=n, slabs=slabs)
    return pl.pallas_call(
        kern,
        grid_spec=pltpu.PrefetchScalarGridSpec(
            num_scalar_prefetch=3,
            grid=(n // tm,),
            in_specs=[pl.BlockSpec((tm * slabs, LANE), lambda i, *_: (i, 0))],
            out_specs=pl.BlockSpec(memory_space=pl.ANY),
            scratch_shapes=[
                pltpu.VMEM((MOE_BLK * slabs, LANE), F32),
                pltpu.SemaphoreType.DMA(()),
                pltpu.SemaphoreType.DMA(()),
            ],
        ),
        out_shape=jax.ShapeDtypeStruct((n_rows * slabs, LANE), F32),
        compiler_params=_params("arbitrary"),
    )(dest, pend, pc, h2g)


def _gmm_kernel(be_ref, nu_ref, x_ref, wg_ref, wu_ref, wd_ref, o_ref, *, slabs):
    @pl.when(pl.program_id(0) < nu_ref[0])
    def _():
        x = jnp.concatenate(
            [x_ref[pl.ds(s, MOE_BLK, stride=slabs), :].astype(BF16) for s in range(slabs)], axis=1)
        hg = jnp.dot(x, wg_ref[...], preferred_element_type=F32)
        hu = jnp.dot(x, wu_ref[...], preferred_element_type=F32)
        hb = (jax.nn.silu(hg) * hu).astype(BF16)
        y = jnp.dot(hb, wd_ref[...], preferred_element_type=F32)
        for s in range(slabs):
            o_ref[pl.ds(s, MOE_BLK, stride=slabs), :] = y[:, s * LANE:(s + 1) * LANE]


def _gmm(blk_e, nused, xs, wg, wu, wd):
    _, d, de = wg.shape
    slabs = d // LANE
    nblk = xs.shape[0] // (MOE_BLK * slabs)
    kern = functools.partial(_gmm_kernel, slabs=slabs)

    def blk(b, be, nu):
        return jnp.minimum(b, nu[0] - 1)

    return pl.pallas_call(
        kern,
        grid_spec=pltpu.PrefetchScalarGridSpec(
            num_scalar_prefetch=2,
            grid=(nblk,),
            in_specs=[
                pl.BlockSpec((MOE_BLK * slabs, LANE), lambda b, be, nu: (blk(b, be, nu), 0)),
                pl.BlockSpec((None, d, de), lambda b, be, nu: (be[blk(b, be, nu)], 0, 0)),
                pl.BlockSpec((None, d, de), lambda b, be, nu: (be[blk(b, be, nu)], 0, 0)),
                pl.BlockSpec((None, de, d), lambda b, be, nu: (be[blk(b, be, nu)], 0, 0)),
            ],
            out_specs=pl.BlockSpec((MOE_BLK * slabs, LANE), lambda b, be, nu: (blk(b, be, nu), 0)),
        ),
        out_shape=jax.ShapeDtypeStruct(xs.shape, F32),
        compiler_params=_params("arbitrary"),
    )(blk_e, nused, xs, wg, wu, wd)


def _combine_kernel(dest_ref, wt_ref, x_ref, g2_ref, ys_ref, o_ref, buf_ref, wb_ref, sem, *, tm, n, slabs, nt):
    i = pl.program_id(0)
    slot = i % 2
    rows = tm * COMBINE_PITCH

    def gather(tile, sl):
        def start(r, c):
            for k in range(TOP_K):
                src_row = dest_ref[k * n + tile * tm + r]
                src = ys_ref.at[pl.ds(pl.multiple_of(src_row * slabs, slabs), slabs), :]
                dst = buf_ref.at[sl, pl.ds(pl.multiple_of(k * rows + r * COMBINE_PITCH, SUBLANE), slabs), :]
                pltpu.make_async_copy(src, dst, sem.at[sl]).start(priority=k % 2)
            return c

        lax.fori_loop(0, tm, start, 0)

    @pl.when(i == 0)
    def _():
        gather(0, 0)

    @pl.when(i + 1 < nt)
    def _():
        gather(i + 1, 1 - slot)

    moved = TOP_K * tm * slabs
    pltpu.make_async_copy(ys_ref.at[pl.ds(0, moved), :], buf_ref.at[slot, pl.ds(0, moved), :], sem.at[slot]).wait()

    wt = wt_ref[...]
    for k in range(TOP_K):
        wb_ref[k] = jnp.broadcast_to(wt[:, k:k + 1], (tm, LANE))
    for s in range(slabs):
        c0 = s * LANE
        routed = None
        for k in range(TOP_K):
            t = wb_ref[k] * buf_ref[slot, pl.ds(k * rows + s, tm, stride=COMBINE_PITCH), :]
            routed = t if routed is None else routed + t
        o_ref[:, c0:c0 + LANE] = x_ref[:, c0:c0 + LANE] + g2_ref[:, c0:c0 + LANE] * routed


def _combine(dest, wt, x, g2, ys, seq):
    n, d = x.shape
    slabs = d // LANE
    tm = min(128, seq)
    tps = seq // tm
    assert COMBINE_PITCH >= slabs
    kern = functools.partial(_combine_kernel, tm=tm, n=n, slabs=slabs, nt=n // tm)
    return pl.pallas_call(
        kern,
        grid_spec=pltpu.PrefetchScalarGridSpec(
            num_scalar_prefetch=1,
            grid=(n // tm,),
            in_specs=[
                pl.BlockSpec((tm, TOP_K), lambda i, *_: (i, 0)),
                pl.BlockSpec((tm, d), lambda i, *_: (i, 0)),
                pl.BlockSpec((None, 1, d), lambda i, *_: (i // tps, 0, 0)),
                pl.BlockSpec(memory_space=pl.ANY),
            ],
            out_specs=pl.BlockSpec((tm, d), lambda i, *_: (i, 0)),
            scratch_shapes=[
                pltpu.VMEM((2, TOP_K * tm * COMBINE_PITCH, LANE), F32),
                pltpu.VMEM((TOP_K, tm, LANE), F32),
                pltpu.SemaphoreType.DMA((2,)),
            ],
        ),
        out_shape=jax.ShapeDtypeStruct((n, d), F32),
        compiler_params=_params("arbitrary"),
    )(dest, wt, x, g2, ys)


def _shared_kernel(h2_ref, x_ref, g2_ref, swg_ref, swu_ref, swd_ref, o_ref):
    h2 = h2_ref[...]
    hs = (jax.nn.silu(jnp.dot(h2, swg_ref[...], preferred_element_type=F32))
          * jnp.dot(h2, swu_ref[...], preferred_element_type=F32)).astype(BF16)
    o_ref[...] = x_ref[...] + g2_ref[...] * jnp.dot(hs, swd_ref[...], preferred_element_type=F32)


def _shared_expert(h2, x, g2, swg, swu, swd, seq):
    n, d = x.shape
    ds_ = swg.shape[1]
    tm = min(512, seq)
    tps = seq // tm
    return pl.pallas_call(
        _shared_kernel,
        grid=(n // tm,),
        in_specs=[
            pl.BlockSpec((tm, d), lambda i: (i, 0)),
            pl.BlockSpec((tm, d), lambda i: (i, 0)),
            pl.BlockSpec((None, 1, d), lambda i: (i // tps, 0, 0)),
            pl.BlockSpec((d, ds_), lambda i: (0, 0)),
            pl.BlockSpec((d, ds_), lambda i: (0, 0)),
            pl.BlockSpec((ds_, d), lambda i: (0, 0)),
        ],
        out_specs=pl.BlockSpec((tm, d), lambda i: (i, 0)),
        out_shape=jax.ShapeDtypeStruct((n, d), F32),
        compiler_params=_params("parallel"),
    )(h2, x, g2, swg, swu, swd)


def _final_norm_kernel(x_ref, g_ref, o_ref):
    x = x_ref[...]
    o_ref[...] = x * lax.rsqrt(jnp.mean(x * x, axis=-1, keepdims=True) + EPS) * g_ref[...]


def _final_norm(x, g, seq):
    n, d = x.shape
    tm = min(512, seq)
    return pl.pallas_call(
        _final_norm_kernel,
        grid=(n // tm,),
        in_specs=[pl.BlockSpec((tm, d), lambda i: (i, 0)), pl.BlockSpec((1, d), lambda i: (0, 0))],
        out_specs=pl.BlockSpec((tm, d), lambda i: (i, 0)),
        out_shape=jax.ShapeDtypeStruct((n, d), F32),
        compiler_params=_params("parallel"),
    )(x, g.reshape(1, d))


def _swap_halves(w):
    half = w.shape[-1] // 2
    return jnp.concatenate([w[..., half:], w[..., :half]], axis=-1)


def _prep_layer(l, p):
    d = p['w_in'].shape[1]
    sizes = [HA * 2 * DHA, HA * 2 * DHA, HA * DVA, 2 * BRANCH_W, 2 * LRU_W, Q_LORA, KV_LORA, D_ROPE, N_BRANCH * d]
    offs = np.cumsum([0] + sizes)
    w = p['w_in'][l]
    qa, ka, va, uv, xg, cq, ckv, kr, gl = [w[:, offs[i]:offs[i + 1]] for i in range(len(sizes))]
    pad = jnp.zeros((d, 512 - KV_LORA - 2 * D_ROPE), w.dtype)
    w_att = jnp.concatenate([qa * (DHA ** -0.5 * LOG2E), ka, va], axis=1).astype(BF16)
    w_rest = jnp.concatenate([uv, xg, cq, ckv, kr, _swap_halves(kr), pad, gl], axis=1).astype(BF16)

    wuq = p['w_uq'][l].reshape(Q_LORA, HD, D_NOPE + D_ROPE)
    wuq_r = wuq[:, :, D_NOPE:]
    wuq_ext = jnp.concatenate([wuq[:, :, :D_NOPE], wuq_r, _swap_halves(wuq_r)], axis=-1)
    wuq_ext = wuq_ext.reshape(Q_LORA, HD * 2 * LANE).astype(BF16)

    lam_init = 0.8 - 0.6 * math.exp(-0.3 * l)
    lam = (jnp.exp(jnp.sum(p['lam_q1'][l] * p['lam_k1'][l])) - jnp.exp(jnp.sum(p['lam_q2'][l] * p['lam_k2'][l]))
           + lam_init)
    slopes = jnp.asarray(2.0 ** (-8.0 * np.arange(1, HA + 1) / HA) * LOG2E, F32)
    par = jnp.concatenate([slopes, lam.reshape(1), jnp.full((1,), 1.0 - lam_init, F32), jnp.zeros((2,), F32)])

    rw = p['router_w'][l].T
    rw_hi = rw.astype(BF16)
    rw_lo = (rw - rw_hi.astype(F32)).astype(BF16)
    bs = jnp.repeat(p['sgu_b'][l].T, LANE, axis=1)
    return dict(
        w_att=w_att, w_rest=w_rest, wuq=wuq_ext, wukv=p['w_ukv'][l].astype(BF16), par=par,
        sgu_w=p['sgu_w'][l].astype(BF16), sgu_bs=bs,
        rg_wr=p['rg_wr'][l].astype(BF16), rg_wi=p['rg_wi'][l].astype(BF16),
        w_branch=p['w_branch'][l].astype(BF16), w_out=p['w_out'][l].astype(BF16),
        rw_hi=rw_hi, rw_lo=rw_lo,
        exp_wg=p['exp_wg'][l].astype(BF16), exp_wu=p['exp_wu'][l].astype(BF16), exp_wd=p['exp_wd'][l].astype(BF16),
        sh_wg=p['sh_wg'][l].astype(BF16), sh_wu=p['sh_wu'][l].astype(BF16), sh_wd=p['sh_wd'][l].astype(BF16),
    )


def _rope_table(n_seq, seq):
    inv_freq = ROPE_THETA ** (-jnp.arange(0, D_ROPE, 2, dtype=F32) / D_ROPE)
    ang = jnp.arange(seq, dtype=F32)[:, None] * inv_freq[None, :]
    cos, sin = jnp.cos(ang), jnp.sin(ang)
    tab = jnp.concatenate([cos, cos, -sin, sin], axis=1)
    return jnp.tile(tab, (n_seq, 1))


def _trunk(x3, mods, p, preps):
    n_seq, seq, d = x3.shape
    n = n_seq * seq
    x = x3.reshape(n, d)
    tab = _rope_table(n_seq, seq)
    depth = len(preps)
    for l in range(depth):
        w = preps[l]
        mod = mods[l]
        mod1 = jnp.stack([mod[:, 1], mod[:, 0]], axis=1)
        mod2 = jnp.stack([mod[:, 4], mod[:, 3]], axis=1)
        g1 = mod[:, 2:3]
        g2 = mod[:, 5:6]

        att = _inproj(x, p['norm1_g'][l], mod1, w['w_att'], BF16, seq)
        rest = _inproj(x, p['norm1_g'][l], mod1, w['w_rest'], F32, seq)
        ya = _diff_attn(att, w['par'], p['subln_g'][l], n_seq, seq)
        yb = _sgu(rest, p['sgu_ln_g'][l], p['sgu_ln_b'][l], w['sgu_w'], w['sgu_bs'], seq)
        hf, hr = _lru(rest, p['conv_w'][l], p['conv_b'][l], w['rg_wr'], p['rg_br'][l], w['rg_wi'],
                      p['rg_bi'][l], p['rg_lam'][l], seq)
        q, k, v = _mla_prep(rest, tab, p['q_norm_g'][l], p['kv_norm_g'][l], w['wuq'], w['wukv'], seq)
        yd = _mla_attn(q, k, v, n_seq, seq)
        merged = _merge(ya, yb, hf, hr, rest, yd, w['w_branch'], seq)
        x = _outproj(merged, w['w_out'], x, g1, seq)

        h2, h2g, eidx, wgt, rank, cnt = _router(x, p['norm2_g'][l], mod2, w['rw_hi'], w['rw_lo'],
                                                p['router_bias'][l], seq)
        counts = cnt[:, 0]
        pc = ((counts + MOE_BLK - 1) // MOE_BLK) * MOE_BLK
        pend = jnp.cumsum(pc).astype(I32)
        pstart = pend - pc
        eid = jnp.arange(N_EXPERTS, dtype=I32)
        dest = (jnp.sum(jnp.where(eidx[..., None] == eid, pstart, 0), axis=-1) + rank).reshape(-1).astype(I32)
        nblk = n * TOP_K // MOE_BLK + N_EXPERTS
        blk_row = jnp.arange(nblk, dtype=I32)[:, None] * MOE_BLK
        blk_e = jnp.minimum(jnp.sum((pend[None, :] <= blk_row).astype(I32), axis=1), N_EXPERTS - 1)
        nused = (pend[-1:] // MOE_BLK).astype(I32)
        xs = _dispatch(dest, pend, pc.astype(I32), h2g, nblk * MOE_BLK, seq)
        ys = _gmm(blk_e, nused, xs, w['exp_wg'], w['exp_wu'], w['exp_wd'])
        x = _shared_expert(h2, x, g2, w['sh_wg'], w['sh_wu'], w['sh_wd'], seq)
        x = _combine(dest, wgt.T, x, g2, ys, seq)
    return _final_norm(x, p['final_g'], seq).reshape(n_seq, seq, d)


def kernel(x_prompt, x_sample, c_prompt, c_sample, ada_w, ada_b, norm1_g, norm2_g, w_in, lam_q1, lam_k1, lam_q2, lam_k2, subln_g, sgu_ln_g, sgu_ln_b, sgu_w, sgu_b, conv_w, conv_b, rg_wr, rg_br, rg_wi, rg_bi, rg_lam, q_norm_g, kv_norm_g, w_uq, w_ukv, w_branch, w_out, router_w, router_bias, exp_wg, exp_wu, exp_wd, sh_wg, sh_wu, sh_wd, final_g):
    p = dict(ada_w=ada_w, ada_b=ada_b, norm1_g=norm1_g, norm2_g=norm2_g, w_in=w_in,
             lam_q1=lam_q1, lam_k1=lam_k1, lam_q2=lam_q2, lam_k2=lam_k2, subln_g=subln_g,
             sgu_ln_g=sgu_ln_g, sgu_ln_b=sgu_ln_b, sgu_w=sgu_w, sgu_b=sgu_b,
             conv_w=conv_w, conv_b=conv_b, rg_wr=rg_wr, rg_br=rg_br, rg_wi=rg_wi, rg_bi=rg_bi,
             rg_lam=rg_lam, q_norm_g=q_norm_g, kv_norm_g=kv_norm_g, w_uq=w_uq, w_ukv=w_ukv,
             w_branch=w_branch, w_out=w_out, router_w=router_w, router_bias=router_bias,
             exp_wg=exp_wg, exp_wu=exp_wu, exp_wd=exp_wd, sh_wg=sh_wg, sh_wu=sh_wu, sh_wd=sh_wd,
             final_g=final_g)
    depth, d, _ = ada_w.shape
    bp, bs = x_prompt.shape[0], x_sample.shape[0]
    assert bp + bs <= 8
    c8 = jnp.concatenate([c_prompt, c_sample, jnp.zeros((8 - bp - bs, d), F32)], axis=0)
    mod = _ada_mod(c8, ada_w, ada_b).reshape(depth, 8, 6, d)
    preps = [_prep_layer(l, p) for l in range(depth)]
    y_prompt = _trunk(x_prompt, mod[:, :bp], p, preps)
    y_sample = _trunk(x_sample, mod[:, bp:bp + bs], p, preps)
    return (y_prompt, y_sample)
```

```python
import functools
import math

import numpy as np
import jax
import jax.numpy as jnp
from jax import lax
from jax.experimental import pallas as pl
from jax.experimental.pallas import tpu as pltpu

F32 = jnp.float32
BF16 = jnp.bfloat16
I32 = jnp.int32

EPS = 1e-6
LOG2E = math.log2(math.e)
LANE = 128
SUBLANE = 8
VMEM_LIMIT = 48 * 1024 * 1024

HA = 4
DHA = 64
DVA = 2 * DHA
BRANCH_W = 512
SGU_CHUNK = 128
SGU_GROUPS = 4
LRU_W = 512
LRU_BLOCKS = 4
LRU_BW = LRU_W // LRU_BLOCKS
LRU_C = 8.0
HALO = 16
HD = 4
Q_LORA = 512
KV_LORA = 256
D_NOPE = 128
D_ROPE = 64
DV_D = 128
ROPE_THETA = 10000.0
N_EXPERTS = 64
TOP_K = 8
N_GROUPS = 8
TOPK_GROUPS = 4
GROUP_SZ = N_EXPERTS // N_GROUPS
ROUTE_SCALE = 2.5
MOE_BLK = 256
N_BRANCH = 4
COMBINE_PITCH = 24

R_UV = 0
R_XC = 1024
R_GATE = 1536
R_CQ = 2048
R_CKV = 2560
R_GL = 3072

NT_DIMS = (((1,), (1,)), ((), ()))


def _params(*sem):
    return pltpu.CompilerParams(dimension_semantics=sem, vmem_limit_bytes=VMEM_LIMIT)


def _ada_kernel(c_ref, w_ref, b_ref, o_ref):
    a = jax.nn.silu(c_ref[...]).astype(BF16)
    o_ref[...] = jnp.dot(a, w_ref[...].astype(BF16), preferred_element_type=F32) + b_ref[...]


def _ada_mod(c8, ada_w, ada_b):
    depth, d, n = ada_w.shape
    tn = 1024
    return pl.pallas_call(
        _ada_kernel,
        grid=(depth, n // tn),
        in_specs=[
            pl.BlockSpec((8, d), lambda l, j: (0, 0)),
            pl.BlockSpec((None, d, tn), lambda l, j: (l, 0, j)),
            pl.BlockSpec((None, 1, tn), lambda l, j: (l, 0, j)),
        ],
        out_specs=pl.BlockSpec((None, 8, tn), lambda l, j: (l, 0, j)),
        out_shape=jax.ShapeDtypeStruct((depth, 8, n), F32),
        compiler_params=_params("parallel", "parallel"),
    )(c8, ada_w, ada_b.reshape(depth, 1, n))


def _inproj_kernel(x_ref, g_ref, mod_ref, w_ref, o_ref, h_ref):
    @pl.when(pl.program_id(1) == 0)
    def _():
        x = x_ref[...]
        y = x * lax.rsqrt(jnp.mean(x * x, axis=-1, keepdims=True) + EPS) * g_ref[...]
        h_ref[...] = (y * (1.0 + mod_ref[0:1, :]) + mod_ref[1:2, :]).astype(BF16)

    o_ref[...] = jnp.dot(h_ref[...], w_ref[...], preferred_element_type=F32).astype(o_ref.dtype)


def _inproj(x, g, mod, w, out_dtype, seq):
    n, d = x.shape
    nc = w.shape[1]
    tm = min(1024, seq)
    tn = 1024 if nc % 1024 == 0 else 512
    tps = seq // tm
    return pl.pallas_call(
        _inproj_kernel,
        grid=(n // tm, nc // tn),
        in_specs=[
            pl.BlockSpec((tm, d), lambda i, j: (i, 0)),
            pl.BlockSpec((1, d), lambda i, j: (0, 0)),
            pl.BlockSpec((None, 2, d), lambda i, j: (i // tps, 0, 0)),
            pl.BlockSpec((d, tn), lambda i, j: (0, j)),
        ],
        out_specs=pl.BlockSpec((tm, tn), lambda i, j: (i, j)),
        out_shape=jax.ShapeDtypeStruct((n, nc), out_dtype),
        scratch_shapes=[pltpu.VMEM((tm, d), BF16)],
        compiler_params=_params("parallel", "arbitrary"),
    )(x, g.reshape(1, d), mod, w)


def _online_softmax_step(s, vx, m_ref, acc_ref, idx, tk):
    m_prev = m_ref[idx]
    m_new = jnp.maximum(m_prev, jnp.max(s, axis=-1, keepdims=True))
    alpha = jnp.exp2(m_prev - m_new)
    p = jnp.exp2(s - jnp.concatenate([m_new] * (tk // LANE), axis=1))
    pv = jnp.dot(p.astype(BF16), vx, preferred_element_type=F32)
    acc_ref[idx] = jnp.concatenate([alpha, alpha], axis=1) * acc_ref[idx] + pv
    m_ref[idx] = m_new


def _diff_attn_kernel(par_ref, q_ref, k_ref, v_ref, g_ref, o_ref, m_ref, acc_ref, *, tq, tk, nk):
    qi = pl.program_id(1)
    ki = pl.program_id(2)

    @pl.when(ki == 0)
    def _():
        m_ref[...] = jnp.full(m_ref.shape, -jnp.inf, F32)
        acc_ref[...] = jnp.zeros(acc_ref.shape, F32)

    lane = lax.broadcasted_iota(I32, (tq, LANE), 1)
    row = lax.broadcasted_iota(I32, (tq, tk), 0)
    col = lax.broadcasted_iota(I32, (tq, tk), 1)
    dist = jnp.abs(row - col + (qi * tq - ki * tk)).astype(F32)
    ones = jnp.ones((tk, LANE), BF16)
    for h in range(HA):
        q = q_ref[:, h * LANE:(h + 1) * LANE]
        k = k_ref[:, h * LANE:(h + 1) * LANE]
        vx = jnp.concatenate([v_ref[:, h * LANE:(h + 1) * LANE], ones], axis=1)
        bias = dist * par_ref[h]
        zero = jnp.zeros_like(q)
        for mi, qm in enumerate((jnp.where(lane < DHA, q, zero), jnp.where(lane < DHA, zero, q))):
            s = lax.dot_general(qm, k, NT_DIMS, preferred_element_type=F32) - bias
            _online_softmax_step(s, vx, m_ref, acc_ref, 2 * h + mi, tk)

    @pl.when(ki == nk - 1)
    def _():
        lam = par_ref[HA]
        for h in range(HA):
            a1 = acc_ref[2 * h]
            a2 = acc_ref[2 * h + 1]
            o = a1[:, :LANE] / a1[:, LANE:] - lam * (a2[:, :LANE] / a2[:, LANE:])
            y = o * lax.rsqrt(jnp.mean(o * o, axis=-1, keepdims=True) + EPS) * g_ref[...]
            o_ref[:, h * LANE:(h + 1) * LANE] = (y * par_ref[HA + 1]).astype(o_ref.dtype)


def _diff_attn(att, par, subln_g, n_seq, seq):
    n = att.shape[0]
    tq = min(512, seq)
    tk = min(512, seq)
    nq, nk = seq // tq, seq // tk
    w = HA * LANE
    kern = functools.partial(_diff_attn_kernel, tq=tq, tk=tk, nk=nk)
    return pl.pallas_call(
        kern,
        grid=(n_seq, nq, nk),
        in_specs=[
            pl.BlockSpec(memory_space=pltpu.SMEM),
            pl.BlockSpec((tq, w), lambda b, qi, ki: (b * nq + qi, 0)),
            pl.BlockSpec((tk, w), lambda b, qi, ki: (b * nk + ki, 1)),
            pl.BlockSpec((tk, w), lambda b, qi, ki: (b * nk + ki, 2)),
            pl.BlockSpec((1, DVA), lambda b, qi, ki: (0, 0)),
        ],
        out_specs=pl.BlockSpec((tq, w), lambda b, qi, ki: (b * nq + qi, 0)),
        out_shape=jax.ShapeDtypeStruct((n, HA * DVA), BF16),
        scratch_shapes=[
            pltpu.VMEM((2 * HA, tq, LANE), F32),
            pltpu.VMEM((2 * HA, tq, 2 * LANE), F32),
        ],
        compiler_params=_params("parallel", "parallel", "arbitrary"),
    )(par, att, att, att, subln_g.reshape(1, DVA))


def _mla_prep_kernel(cq_ref, ckv_ref, tab_ref, gq_ref, gkv_ref, wuq_ref, wukv_ref, q_ref, k_ref, v_ref, *, scale):
    cq = cq_ref[...].astype(F32)
    cqn = (cq * lax.rsqrt(jnp.mean(cq * cq, axis=-1, keepdims=True) + EPS) * gq_ref[...]).astype(BF16)
    c = ckv_ref[...].astype(F32)
    ckv = c[:, :KV_LORA]
    ckvn = (ckv * lax.rsqrt(jnp.mean(ckv * ckv, axis=-1, keepdims=True) + EPS) * gkv_ref[...]).astype(BF16)
    tab = tab_ref[...]
    lane = lax.broadcasted_iota(I32, tab.shape, 1)

    def rope(pair):
        pr = pair * tab
        return jnp.where(lane < D_ROPE, pr + pltpu.roll(pr, D_ROPE, 1), 0.0)

    kr = rope(c[:, KV_LORA:KV_LORA + LANE]).astype(BF16)
    qf = jnp.dot(cqn, wuq_ref[...], preferred_element_type=F32)
    kvf = jnp.dot(ckvn, wukv_ref[...], preferred_element_type=F32)
    for h in range(HD):
        b0 = h * 2 * LANE
        q_ref[:, b0:b0 + LANE] = (qf[:, b0:b0 + LANE] * scale).astype(BF16)
        q_ref[:, b0 + LANE:b0 + 2 * LANE] = (rope(qf[:, b0 + LANE:b0 + 2 * LANE]) * scale).astype(BF16)
        k_ref[:, b0:b0 + LANE] = kvf[:, b0:b0 + LANE].astype(BF16)
        k_ref[:, b0 + LANE:b0 + 2 * LANE] = kr
        v_ref[:, h * LANE:(h + 1) * LANE] = kvf[:, b0 + LANE:b0 + 2 * LANE].astype(BF16)


def _mla_prep(rest, tab, gq, gkv, wuq, wukv, seq):
    n = rest.shape[0]
    tm = min(512, seq)
    scale = (D_NOPE + D_ROPE) ** -0.5 * LOG2E
    kern = functools.partial(_mla_prep_kernel, scale=scale)
    wq = HD * 2 * LANE
    return pl.pallas_call(
        kern,
        grid=(n // tm,),
        in_specs=[
            pl.BlockSpec((tm, Q_LORA), lambda i: (i, R_CQ // Q_LORA)),
            pl.BlockSpec((tm, 512), lambda i: (i, R_CKV // 512)),
            pl.BlockSpec((tm, LANE), lambda i: (i, 0)),
            pl.BlockSpec((1, Q_LORA), lambda i: (0, 0)),
            pl.BlockSpec((1, KV_LORA), lambda i: (0, 0)),
            pl.BlockSpec((Q_LORA, wq), lambda i: (0, 0)),
            pl.BlockSpec((KV_LORA, wq), lambda i: (0, 0)),
        ],
        out_specs=[
            pl.BlockSpec((tm, wq), lambda i: (i, 0)),
            pl.BlockSpec((tm, wq), lambda i: (i, 0)),
            pl.BlockSpec((tm, HD * DV_D), lambda i: (i, 0)),
        ],
        out_shape=[
            jax.ShapeDtypeStruct((n, wq), BF16),
            jax.ShapeDtypeStruct((n, wq), BF16),
            jax.ShapeDtypeStruct((n, HD * DV_D), BF16),
        ],
        compiler_params=_params("parallel"),
    )(rest, rest, tab, gq.reshape(1, -1), gkv.reshape(1, -1), wuq, wukv)


def _mla_attn_kernel(q_ref, k_ref, v_ref, o_ref, m_ref, acc_ref, *, tk, nk):
    ki = pl.program_id(2)

    @pl.when(ki == 0)
    def _():
        m_ref[...] = jnp.full(m_ref.shape, -jnp.inf, F32)
        acc_ref[...] = jnp.zeros(acc_ref.shape, F32)

    ones = jnp.ones((tk, LANE), BF16)
    for h in range(HD):
        q = q_ref[:, h * 2 * LANE:(h + 1) * 2 * LANE]
        k = k_ref[:, h * 2 * LANE:(h + 1) * 2 * LANE]
        vx = jnp.concatenate([v_ref[:, h * LANE:(h + 1) * LANE], ones], axis=1)
        s = lax.dot_general(q, k, NT_DIMS, preferred_element_type=F32)
        _online_softmax_step(s, vx, m_ref, acc_ref, h, tk)

    @pl.when(ki == nk - 1)
    def _():
        for h in range(HD):
            a = acc_ref[h]
            o_ref[:, h * LANE:(h + 1) * LANE] = (a[:, :LANE] / a[:, LANE:]).astype(o_ref.dtype)


def _mla_attn(q, k, v, n_seq, seq):
    n = q.shape[0]
    tq = min(512, seq)
    tk = min(1024, seq)
    nq, nk = seq // tq, seq // tk
    wq = HD * 2 * LANE
    kern = functools.partial(_mla_attn_kernel, tk=tk, nk=nk)
    return pl.pallas_call(
        kern,
        grid=(n_seq, nq, nk),
        in_specs=[
            pl.BlockSpec((tq, wq), lambda b, qi, ki: (b * nq + qi, 0)),
            pl.BlockSpec((tk, wq), lambda b, qi, ki: (b * nk + ki, 0)),
            pl.BlockSpec((tk, HD * DV_D), lambda b, qi, ki: (b * nk + ki, 0)),
        ],
        out_specs=pl.BlockSpec((tq, HD * DV_D), lambda b, qi, ki: (b * nq + qi, 0)),
        out_shape=jax.ShapeDtypeStruct((n, HD * DV_D), BF16),
        scratch_shapes=[
            pltpu.VMEM((HD, tq, LANE), F32),
            pltpu.VMEM((HD, tq, 2 * LANE), F32),
        ],
        compiler_params=_params("parallel", "parallel", "arbitrary"),
    )(q, k, v)


def _sgu_kernel(uv_ref, g_ref, b_ref, ws_ref, bs_ref, o_ref, *, tm):
    z = jax.nn.gelu(uv_ref[...].astype(F32))
    u = z[:, :BRANCH_W]
    v = z[:, BRANCH_W:]
    mu = jnp.mean(v, axis=-1, keepdims=True)
    vc = v - mu
    var = jnp.mean(vc * vc, axis=-1, keepdims=True)
    vn = (vc * lax.rsqrt(var + EPS) * g_ref[...] + b_ref[...]).astype(BF16)
    for c in range(tm // SGU_CHUNK):
        r0 = c * SGU_CHUNK
        for g in range(SGU_GROUPS):
            c0 = g * LANE
            vm = jnp.dot(ws_ref[g], vn[r0:r0 + SGU_CHUNK, c0:c0 + LANE], preferred_element_type=F32)
            vm = vm + bs_ref[:, c0:c0 + LANE]
            o_ref[r0:r0 + SGU_CHUNK, c0:c0 + LANE] = (u[r0:r0 + SGU_CHUNK, c0:c0 + LANE] * vm).astype(o_ref.dtype)


def _sgu(rest, ln_g, ln_b, ws, bs, seq):
    n = rest.shape[0]
    tm = min(512, seq)
    kern = functools.partial(_sgu_kernel, tm=tm)
    return pl.pallas_call(
        kern,
        grid=(n // tm,),
        in_specs=[
            pl.BlockSpec((tm, 2 * BRANCH_W), lambda i: (i, R_UV // (2 * BRANCH_W))),
            pl.BlockSpec((1, BRANCH_W), lambda i: (0, 0)),
            pl.BlockSpec((1, BRANCH_W), lambda i: (0, 0)),
            pl.BlockSpec((SGU_GROUPS, SGU_CHUNK, SGU_CHUNK), lambda i: (0, 0, 0)),
            pl.BlockSpec((SGU_CHUNK, BRANCH_W), lambda i: (0, 0)),
        ],
        out_specs=pl.BlockSpec((tm, BRANCH_W), lambda i: (i, 0)),
        out_shape=jax.ShapeDtypeStruct((n, BRANCH_W), BF16),
        compiler_params=_params("parallel"),
    )(rest, ln_g.reshape(1, -1), ln_b.reshape(1, -1), ws, bs)


def _lru_kernel(xf_ref, xfp_ref, xfn_ref, xr_ref, xrp_ref, xrn_ref, cw_ref, cb_ref, wr_ref, br_ref, wi_ref,
                bi_ref, lam_ref, hf_ref, hr_ref, af_ref, uf_ref, ar_ref, ur_ref, hc_ref, *, tt, tps):
    j = pl.program_id(0) % tps
    rowi = lax.broadcasted_iota(I32, (tt, LRU_W), 0)

    def gates(x_ref, xp_ref, xn_ref, at_start, at_end, d, a_ref, u_ref):
        x = x_ref[...].astype(F32)
        prev = jnp.where(at_start, 0.0, xp_ref[...].astype(F32)[HALO - 1:HALO, :])
        nxt = jnp.where(at_end, 0.0, xn_ref[...].astype(F32)[0:2, :])
        xm1 = jnp.where(rowi == 0, prev, pltpu.roll(x, 1, 0))
        xp1 = jnp.where(rowi == tt - 1, nxt[0:1, :], pltpu.roll(x, tt - 1, 0))
        xp2 = jnp.where(rowi == tt - 2, nxt[0:1, :],
                        jnp.where(rowi == tt - 1, nxt[1:2, :], pltpu.roll(x, tt - 2, 0)))
        xc = xm1 * cw_ref[0:1, :] + x * cw_ref[1:2, :] + xp1 * cw_ref[2:3, :] + xp2 * cw_ref[3:4, :] + cb_ref[...]
        xcb = xc.astype(BF16)

        def blockdiag(w_ref):
            return jnp.concatenate(
                [jnp.dot(xcb[:, g * LRU_BW:(g + 1) * LRU_BW], w_ref[d, g], preferred_element_type=F32)
                 for g in range(LRU_BLOCKS)], axis=1)

        r = jax.nn.sigmoid(blockdiag(wr_ref) + br_ref[d:d + 1, :])
        ig = jax.nn.sigmoid(blockdiag(wi_ref) + bi_ref[d:d + 1, :])
        log_a = (-LRU_C * r) * jax.nn.softplus(-lam_ref[d:d + 1, :])
        a = jnp.exp(log_a)
        u = jnp.sqrt(jnp.tanh(-log_a) * (1.0 + a * a)) * (ig * xc)
        a_ref[...] = a
        u_ref[...] = u

    gates(xf_ref, xfp_ref, xfn_ref, j == 0, j == tps - 1, 0, af_ref, uf_ref)
    gates(xr_ref, xrp_ref, xrn_ref, j == tps - 1, j == 0, 1, ar_ref, ur_ref)

    @pl.when(j == 0)
    def _():
        hc_ref[...] = jnp.zeros(hc_ref.shape, F32)

    ng = tt // SUBLANE
    sub = lax.broadcasted_iota(I32, (SUBLANE, LRU_W), 0)

    def body(g, carry):
        hf_prev, hr_prev = carry
        r0 = pl.multiple_of(g * SUBLANE, SUBLANE)
        a = af_ref[pl.ds(r0, SUBLANE), :]
        u = uf_ref[pl.ds(r0, SUBLANE), :]
        for s in (1, 2, 4):
            a_s = jnp.where(sub >= s, pltpu.roll(a, s, 0), 1.0)
            u_s = jnp.where(sub >= s, pltpu.roll(u, s, 0), 0.0)
            u = a * u_s + u
            a = a * a_s
        h = a * hf_prev + u
        hf_ref[pl.ds(r0, SUBLANE), :] = h
        hf_new = jnp.broadcast_to(h[SUBLANE - 1:SUBLANE, :], (SUBLANE, LRU_W))

        r1 = pl.multiple_of((ng - 1 - g) * SUBLANE, SUBLANE)
        a = ar_ref[pl.ds(r1, SUBLANE), :]
        u = ur_ref[pl.ds(r1, SUBLANE), :]
        for s in (1, 2, 4):
            a_s = jnp.where(sub < SUBLANE - s, pltpu.roll(a, SUBLANE - s, 0), 1.0)
            u_s = jnp.where(sub < SUBLANE - s, pltpu.roll(u, SUBLANE - s, 0), 0.0)
            u = a * u_s + u
            a = a * a_s
        h = a * hr_prev + u
        hr_ref[pl.ds(r1, SUBLANE), :] = h
        hr_new = jnp.broadcast_to(h[0:1, :], (SUBLANE, LRU_W))
        return hf_new, hr_new

    hf_c, hr_c = lax.fori_loop(0, ng, body, (hc_ref[0], hc_ref[1]))
    hc_ref[0] = hf_c
    hc_ref[1] = hr_c


def _lru(rest, cw, cb, wr, br, wi, bi, lam, seq):
    n = rest.shape[0]
    tt = min(512, seq)
    tps = seq // tt
    nt = n // tt
    t8 = tt // HALO
    last8 = n // HALO - 1
    cblk = R_XC // LRU_W

    def rev(i):
        return (i // tps) * tps + (tps - 1 - i % tps)

    kern = functools.partial(_lru_kernel, tt=tt, tps=tps)
    full = lambda shape: pl.BlockSpec(shape, lambda i: (0,) * len(shape))
    return pl.pallas_call(
        kern,
        grid=(nt,),
        in_specs=[
            pl.BlockSpec((tt, LRU_W), lambda i: (i, cblk)),
            pl.BlockSpec((HALO, LRU_W), lambda i: (jnp.maximum(i * t8 - 1, 0), cblk)),
            pl.BlockSpec((HALO, LRU_W), lambda i: (jnp.minimum((i + 1) * t8, last8), cblk)),
            pl.BlockSpec((tt, LRU_W), lambda i: (rev(i), cblk)),
            pl.BlockSpec((HALO, LRU_W), lambda i: (jnp.maximum(rev(i) * t8 - 1, 0), cblk)),
            pl.BlockSpec((HALO, LRU_W), lambda i: (jnp.minimum((rev(i) + 1) * t8, last8), cblk)),
            full((4, LRU_W)),
            full((1, LRU_W)),
            full((2, LRU_BLOCKS, LRU_BW, LRU_BW)),
            full((2, LRU_W)),
            full((2, LRU_BLOCKS, LRU_BW, LRU_BW)),
            full((2, LRU_W)),
            full((2, LRU_W)),
        ],
        out_specs=[
            pl.BlockSpec((tt, LRU_W), lambda i: (i, 0)),
            pl.BlockSpec((tt, LRU_W), lambda i: (rev(i), 0)),
        ],
        out_shape=[jax.ShapeDtypeStruct((n, LRU_W), F32), jax.ShapeDtypeStruct((n, LRU_W), F32)],
        scratch_shapes=[pltpu.VMEM((tt, LRU_W), F32)] * 4 + [pltpu.VMEM((2, SUBLANE, LRU_W), F32)],
        compiler_params=_params("arbitrary"),
    )(rest, rest, rest, rest, rest, rest, cw, cb.reshape(1, -1), wr, br, wi, bi, lam)


def _merge_kernel(ya_ref, yb_ref, hf_ref, hr_ref, gate_ref, yd_ref, wbr_ref, g0_ref, g1_ref, g2_ref, g3_ref, o_ref):
    yc = (jax.nn.gelu(gate_ref[...].astype(F32)) * (hf_ref[...] + hr_ref[...])).astype(BF16)
    ys = (ya_ref[...], yb_ref[...], yc, yd_ref[...])
    gls = (g0_ref, g1_ref, g2_ref, g3_ref)
    acc = None
    for k in range(N_BRANCH):
        t = jax.nn.sigmoid(gls[k][...].astype(F32)) * jnp.dot(ys[k], wbr_ref[k], preferred_element_type=F32)
        acc = t if acc is None else acc + t
    o_ref[...] = acc.astype(o_ref.dtype)


def _merge(ya, yb, hf, hr, rest, yd, wbr, seq):
    n = ya.shape[0]
    d = wbr.shape[-1]
    tm = min(512, seq)
    tn = 512
    row = lambda w: pl.BlockSpec((tm, w), lambda i, j: (i, 0))

    def gl_spec(k):
        base = (R_GL + k * d) // tn
        return pl.BlockSpec((tm, tn), lambda i, j: (i, base + j))

    return pl.pallas_call(
        _merge_kernel,
        grid=(n // tm, d // tn),
        in_specs=[
            row(BRANCH_W), row(BRANCH_W), row(LRU_W), row(LRU_W),
            pl.BlockSpec((tm, LRU_W), lambda i, j: (i, R_GATE // LRU_W)),
            row(BRANCH_W),
            pl.BlockSpec((N_BRANCH, BRANCH_W, tn), lambda i, j: (0, 0, j)),
            gl_spec(0), gl_spec(1), gl_spec(2), gl_spec(3),
        ],
        out_specs=pl.BlockSpec((tm, tn), lambda i, j: (i, j)),
        out_shape=jax.ShapeDtypeStruct((n, d), BF16),
        compiler_params=_params("parallel", "parallel"),
    )(ya, yb, hf, hr, rest, yd, wbr, rest, rest, rest, rest)


def _outproj_kernel(m_ref, w_ref, x_ref, g_ref, o_ref):
    o_ref[...] = x_ref[...] + g_ref[...] * jnp.dot(m_ref[...], w_ref[...], preferred_element_type=F32)


def _outproj(merged, w, x, g1, seq):
    n, d = x.shape
    tm = min(512, seq)
    tps = seq // tm
    return pl.pallas_call(
        _outproj_kernel,
        grid=(n // tm,),
        in_specs=[
            pl.BlockSpec((tm, d), lambda i: (i, 0)),
            pl.BlockSpec((d, d), lambda i: (0, 0)),
            pl.BlockSpec((tm, d), lambda i: (i, 0)),
            pl.BlockSpec((None, 1, d), lambda i: (i // tps, 0, 0)),
        ],
        out_specs=pl.BlockSpec((tm, d), lambda i: (i, 0)),
        out_shape=jax.ShapeDtypeStruct((n, d), F32),
        compiler_params=_params("parallel"),
    )(merged, w, x, g1)


def _router_kernel(x_ref, g_ref, mod_ref, wh_ref, wl_ref, rb_ref, tri_ref,
                   h2_ref, h2g_ref, e_ref, w_ref, r_ref, cnt_ref, run_ref, *, tm, d):
    @pl.when(pl.program_id(0) == 0)
    def _():
        run_ref[...] = jnp.zeros(run_ref.shape, F32)

    x = x_ref[...]
    y = x * lax.rsqrt(jnp.mean(x * x, axis=-1, keepdims=True) + EPS) * g_ref[...]
    h = y * (1.0 + mod_ref[0:1, :]) + mod_ref[1:2, :]
    hb = h.astype(BF16)
    h2_ref[...] = hb
    slabs = d // LANE
    for s in range(slabs):
        h2g_ref[pl.ds(s, tm, stride=slabs), :] = h[:, s * LANE:(s + 1) * LANE]

    hl = (h - hb.astype(F32)).astype(BF16)
    logits = (lax.dot_general(wh_ref[...], hb, NT_DIMS, preferred_element_type=F32)
              + lax.dot_general(wl_ref[...], hb, NT_DIMS, preferred_element_type=F32)
              + lax.dot_general(wh_ref[...], hl, NT_DIMS, preferred_element_type=F32))
    sc = jax.nn.sigmoid(logits)
    sel = sc + rb_ref[...]

    neg = -jnp.inf
    i8 = lax.broadcasted_iota(I32, (GROUP_SZ, tm), 0).astype(F32)
    rows = []
    for g in range(N_GROUPS):
        blk = sel[g * GROUP_SZ:(g + 1) * GROUP_SZ, :]
        m1 = jnp.max(blk, axis=0, keepdims=True)
        i1 = jnp.min(jnp.where(blk == m1, i8, float(GROUP_SZ)), axis=0, keepdims=True)
        m2 = jnp.max(jnp.where(i8 == i1, neg, blk), axis=0, keepdims=True)
        rows.append(m1 + m2)
    gs = jnp.concatenate(rows, axis=0)
    gi8 = lax.broadcasted_iota(I32, (N_GROUPS, tm), 0).astype(F32)
    gsel = jnp.zeros((N_GROUPS, tm), F32)
    for _ in range(TOPK_GROUPS):
        gm = jnp.max(gs, axis=0, keepdims=True)
        gi = jnp.min(jnp.where(gs == gm, gi8, float(N_GROUPS)), axis=0, keepdims=True)
        hit = gi8 == gi
        gsel = jnp.where(hit, 1.0, gsel)
        gs = jnp.where(hit, neg, gs)
    emask = jnp.concatenate(
        [jnp.broadcast_to(gsel[g:g + 1, :], (GROUP_SZ, tm)) for g in range(N_GROUPS)], axis=0)
    selm = jnp.where(emask > 0.0, sel, neg)

    i64 = lax.broadcasted_iota(I32, (N_EXPERTS, tm), 0).astype(F32)
    chosen = jnp.zeros((N_EXPERTS, tm), F32)
    idxs, wts = [], []
    for _ in range(TOP_K):
        mx = jnp.max(selm, axis=0, keepdims=True)
        ix = jnp.min(jnp.where(selm == mx, i64, float(N_EXPERTS)), axis=0, keepdims=True)
        oh = i64 == ix
        wts.append(jnp.sum(jnp.where(oh, sc, 0.0), axis=0, keepdims=True))
        idxs.append(ix)
        selm = jnp.where(oh, neg, selm)
        chosen = jnp.where(oh, 1.0, chosen)
    wk = jnp.concatenate(wts, axis=0)
    w_ref[...] = wk / jnp.sum(wk, axis=0, keepdims=True) * ROUTE_SCALE
    e_ref[...] = jnp.concatenate(idxs, axis=0).astype(I32)

    before = jnp.dot(chosen.astype(BF16), tri_ref[...], preferred_element_type=F32)
    rank = run_ref[...] + before
    r_ref[...] = jnp.concatenate(
        [jnp.sum(jnp.where(i64 == idxs[k], rank, 0.0), axis=0, keepdims=True) for k in range(TOP_K)],
        axis=0).astype(I32)
    run_ref[...] = run_ref[...] + jnp.sum(chosen, axis=1, keepdims=True)
    cnt_ref[...] = jnp.broadcast_to(run_ref[...], cnt_ref.shape).astype(I32)


def _router(x, g, mod, wh, wl, rb, seq):
    n, d = x.shape
    tm = min(512, seq)
    tps = seq // tm
    slabs = d // LANE
    tri = jnp.triu(jnp.ones((tm, tm), F32), 1).astype(BF16)
    kern = functools.partial(_router_kernel, tm=tm, d=d)
    return pl.pallas_call(
        kern,
        grid=(n // tm,),
        in_specs=[
            pl.BlockSpec((tm, d), lambda i: (i, 0)),
            pl.BlockSpec((1, d), lambda i: (0, 0)),
            pl.BlockSpec((None, 2, d), lambda i: (i // tps, 0, 0)),
            pl.BlockSpec((N_EXPERTS, d), lambda i: (0, 0)),
            pl.BlockSpec((N_EXPERTS, d), lambda i: (0, 0)),
            pl.BlockSpec((N_EXPERTS, 1), lambda i: (0, 0)),
            pl.BlockSpec((tm, tm), lambda i: (0, 0)),
        ],
        out_specs=[
            pl.BlockSpec((tm, d), lambda i: (i, 0)),
            pl.BlockSpec((tm * slabs, LANE), lambda i: (i, 0)),
            pl.BlockSpec((TOP_K, tm), lambda i: (0, i)),
            pl.BlockSpec((TOP_K, tm), lambda i: (0, i)),
            pl.BlockSpec((TOP_K, tm), lambda i: (0, i)),
            pl.BlockSpec((N_EXPERTS, LANE), lambda i: (0, 0)),
        ],
        out_shape=[
            jax.ShapeDtypeStruct((n, d), BF16),
            jax.ShapeDtypeStruct((n * slabs, LANE), F32),
            jax.ShapeDtypeStruct((TOP_K, n), I32),
            jax.ShapeDtypeStruct((TOP_K, n), F32),
            jax.ShapeDtypeStruct((TOP_K, n), I32),
            jax.ShapeDtypeStruct((N_EXPERTS, LANE), I32),
        ],
        scratch_shapes=[pltpu.VMEM((N_EXPERTS, 1), F32)],
        compiler_params=_params("arbitrary"),
    )(x, g.reshape(1, d), mod, wh, wl, rb.reshape(N_EXPERTS, 1), tri)


def _dispatch_kernel(dest_ref, pend_ref, pc_ref, h_ref, xs_ref, zero_ref, sem, zsem, *, tm, n, slabs):
    i = pl.program_id(0)
    rows_blk = MOE_BLK * slabs

    def tail_copy(e):
        start = pl.multiple_of((pend_ref[e] - MOE_BLK) * slabs, rows_blk)
        return pltpu.make_async_copy(zero_ref, xs_ref.at[pl.ds(start, rows_blk), :], zsem)

    @pl.when(i == 0)
    def _():
        zero_ref[...] = jnp.zeros(zero_ref.shape, F32)

        def zstart(e, c):
            @pl.when(pc_ref[e] > 0)
            def _():
                tail_copy(e).start()
            return c

        def zwait(e, c):
            @pl.when(pc_ref[e] > 0)
            def _():
                tail_copy(e).wait()
            return c

        lax.fori_loop(0, N_EXPERTS, zstart, 0)
        lax.fori_loop(0, N_EXPERTS, zwait, 0)

    def row_copy(r, k):
        src = h_ref.at[pl.ds(pl.multiple_of(r * slabs, slabs), slabs), :]
        dst_row = dest_ref[k * n + i * tm + r]
        dst = xs_ref.at[pl.ds(pl.multiple_of(dst_row * slabs, slabs), slabs), :]
        return pltpu.make_async_copy(src, dst, sem)

    def start(r, c):
        for k in range(TOP_K):
            row_copy(r, k).start(priority=k % 2)
        return c

    lax.fori_loop(0, tm, start, 0)
    for _ in range(TOP_K):
        pltpu.make_async_copy(h_ref, xs_ref.at[pl.ds(0, tm * slabs), :], sem).wait()


def _dispatch(dest, pend, pc, h2g, n_rows, seq):
    n = dest.shape[0] // TOP_K
    slabs = h2g.shape[0] // n
    tm = min(256, seq)
    kern = functools.partial(_dispatch_kernel, tm=tm, n=n, slabs=slabs)
    return pl.pallas_call(
        kern,
        grid_spec=pltpu.PrefetchScalarGridSpec(
            num_scalar_prefetch=3,
            grid=(n // tm,),
            in_specs=[pl.BlockSpec((tm * slabs, LANE), lambda i, *_: (i, 0))],
            out_specs=pl.BlockSpec(memory_space=pl.ANY),
            scratch_shapes=[
                pltpu.VMEM((MOE_BLK * slabs, LANE), F32),
                pltpu.SemaphoreType.DMA(()),
                pltpu.SemaphoreType.DMA(()),
            ],
        ),
        out_shape=jax.ShapeDtypeStruct((n_rows * slabs, LANE), F32),
        compiler_params=_params("arbitrary"),
    )(dest, pend, pc, h2g)


def _gmm_kernel(be_ref, nu_ref, x_ref, wg_ref, wu_ref, wd_ref, o_ref, *, slabs):
    @pl.when(pl.program_id(0) < nu_ref[0])
    def _():
        x = jnp.concatenate(
            [x_ref[pl.ds(s, MOE_BLK, stride=slabs), :].astype(BF16) for s in range(slabs)], axis=1)
        hg = jnp.dot(x, wg_ref[...], preferred_element_type=F32)
        hu = jnp.dot(x, wu_ref[...], preferred_element_type=F32)
        hb = (jax.nn.silu(hg) * hu).astype(BF16)
        per = 4
        for c in range(slabs // per):
            y = jnp.dot(hb, wd_ref[:, c * per * LANE:(c + 1) * per * LANE], preferred_element_type=F32)
            for j in range(per):
                o_ref[pl.ds(c * per + j, MOE_BLK, stride=slabs), :] = y[:, j * LANE:(j + 1) * LANE]


def _gmm(blk_e, nused, xs, wg, wu, wd):
    _, d, de = wg.shape
    slabs = d // LANE
    nblk = xs.shape[0] // (MOE_BLK * slabs)
    kern = functools.partial(_gmm_kernel, slabs=slabs)

    def blk(b, be, nu):
        return jnp.minimum(b, nu[0] - 1)

    return pl.pallas_call(
        kern,
        grid_spec=pltpu.PrefetchScalarGridSpec(
            num_scalar_prefetch=2,
            grid=(nblk,),
            in_specs=[
                pl.BlockSpec((MOE_BLK * slabs, LANE), lambda b, be, nu: (blk(b, be, nu), 0)),
                pl.BlockSpec((None, d, de), lambda b, be, nu: (be[blk(b, be, nu)], 0, 0)),
                pl.BlockSpec((None, d, de), lambda b, be, nu: (be[blk(b, be, nu)], 0, 0)),
                pl.BlockSpec((None, de, d), lambda b, be, nu: (be[blk(b, be, nu)], 0, 0)),
            ],
            out_specs=pl.BlockSpec((MOE_BLK * slabs, LANE), lambda b, be, nu: (blk(b, be, nu), 0)),
        ),
        out_shape=jax.ShapeDtypeStruct(xs.shape, F32),
        compiler_params=_params("arbitrary"),
    )(blk_e, nused, xs, wg, wu, wd)


def _combine_kernel(dest_ref, wt_ref, x_ref, g2_ref, ys_ref, o_ref, buf_ref, wb_ref, sem, *, tm, n, slabs, nt):
    i = pl.program_id(0)
    slot = i % 2
    rows = tm * COMBINE_PITCH

    def gather(tile, sl):
        def start(r, c):
            for k in range(TOP_K):
                src_row = dest_ref[k * n + tile * tm + r]
                src = ys_ref.at[pl.ds(pl.multiple_of(src_row * slabs, slabs), slabs), :]
                dst = buf_ref.at[sl, pl.ds(pl.multiple_of(k * rows + r * COMBINE_PITCH, SUBLANE), slabs), :]
                pltpu.make_async_copy(src, dst, sem.at[sl]).start(priority=k % 2)
            return c

        lax.fori_loop(0, tm, start, 0)

    @pl.when(i == 0)
    def _():
        gather(0, 0)

    @pl.when(i + 1 < nt)
    def _():
        gather(i + 1, 1 - slot)

    moved = TOP_K * tm * slabs
    pltpu.make_async_copy(ys_ref.at[pl.ds(0, moved), :], buf_ref.at[slot, pl.ds(0, moved), :], sem.at[slot]).wait()

    wt = wt_ref[...]
    for k in range(TOP_K):
        wb_ref[k] = jnp.broadcast_to(wt[:, k:k + 1], (tm, LANE))
    for s in range(slabs):
        c0 = s * LANE
        routed = None
        for k in range(TOP_K):
            t = wb_ref[k] * buf_ref[slot, pl.ds(k * rows + s, tm, stride=COMBINE_PITCH), :]
            routed = t if routed is None else routed + t
        o_ref[:, c0:c0 + LANE] = x_ref[:, c0:c0 + LANE] + g2_ref[:, c0:c0 + LANE] * routed


def _combine(dest, wt, x, g2, ys, seq):
    n, d = x.shape
    slabs = d // LANE
    tm = min(128, seq)
    tps = seq // tm
    assert COMBINE_PITCH >= slabs
    kern = functools.partial(_combine_kernel, tm=tm, n=n, slabs=slabs, nt=n // tm)
    return pl.pallas_call(
        kern,
        grid_spec=pltpu.PrefetchScalarGridSpec(
            num_scalar_prefetch=1,
            grid=(n // tm,),
            in_specs=[
                pl.BlockSpec((tm, TOP_K), lambda i, *_: (i, 0)),
                pl.BlockSpec((tm, d), lambda i, *_: (i, 0)),
                pl.BlockSpec((None, 1, d), lambda i, *_: (i // tps, 0, 0)),
                pl.BlockSpec(memory_space=pl.ANY),
            ],
            out_specs=pl.BlockSpec((tm, d), lambda i, *_: (i, 0)),
            scratch_shapes=[
                pltpu.VMEM((2, TOP_K * tm * COMBINE_PITCH, LANE), F32),
                pltpu.VMEM((TOP_K, tm, LANE), F32),
                pltpu.SemaphoreType.DMA((2,)),
            ],
        ),
        out_shape=jax.ShapeDtypeStruct((n, d), F32),
        compiler_params=_params("arbitrary"),
    )(dest, wt, x, g2, ys)


def _shared_kernel(h2_ref, x_ref, g2_ref, swg_ref, swu_ref, swd_ref, o_ref):
    h2 = h2_ref[...]
    hs = (jax.nn.silu(jnp.dot(h2, swg_ref[...], preferred_element_type=F32))
          * jnp.dot(h2, swu_ref[...], preferred_element_type=F32)).astype(BF16)
    o_ref[...] = x_ref[...] + g2_ref[...] * jnp.dot(hs, swd_ref[...], preferred_element_type=F32)


def _shared_expert(h2, x, g2, swg, swu, swd, seq):
    n, d = x.shape
    ds_ = swg.shape[1]
    tm = min(512, seq)
    tps = seq // tm
    return pl.pallas_call(
        _shared_kernel,
        grid=(n // tm,),
        in_specs=[
            pl.BlockSpec((tm, d), lambda i: (i, 0)),
            pl.BlockSpec((tm, d), lambda i: (i, 0)),
            pl.BlockSpec((None, 1, d), lambda i: (i // tps, 0, 0)),
            pl.BlockSpec((d, ds_), lambda i: (0, 0)),
            pl.BlockSpec((d, ds_), lambda i: (0, 0)),
            pl.BlockSpec((ds_, d), lambda i: (0, 0)),
        ],
        out_specs=pl.BlockSpec((tm, d), lambda i: (i, 0)),
        out_shape=jax.ShapeDtypeStruct((n, d), F32),
        compiler_params=_params("parallel"),
    )(h2, x, g2, swg, swu, swd)


def _final_norm_kernel(x_ref, g_ref, o_ref):
    x = x_ref[...]
    o_ref[...] = x * lax.rsqrt(jnp.mean(x * x, axis=-1, keepdims=True) + EPS) * g_ref[...]


def _final_norm(x, g, seq):
    n, d = x.shape
    tm = min(512, seq)
    return pl.pallas_call(
        _final_norm_kernel,
        grid=(n // tm,),
        in_specs=[pl.BlockSpec((tm, d), lambda i: (i, 0)), pl.BlockSpec((1, d), lambda i: (0, 0))],
        out_specs=pl.BlockSpec((tm, d), lambda i: (i, 0)),
        out_shape=jax.ShapeDtypeStruct((n, d), F32),
        compiler_params=_params("parallel"),
    )(x, g.reshape(1, d))


def _swap_halves(w):
    half = w.shape[-1] // 2
    return jnp.concatenate([w[..., half:], w[..., :half]], axis=-1)


def _prep_layer(l, p):
    d = p['w_in'].shape[1]
    sizes = [HA * 2 * DHA, HA * 2 * DHA, HA * DVA, 2 * BRANCH_W, 2 * LRU_W, Q_LORA, KV_LORA, D_ROPE, N_BRANCH * d]
    offs = np.cumsum([0] + sizes)
    w = p['w_in'][l]
    qa, ka, va, uv, xg, cq, ckv, kr, gl = [w[:, offs[i]:offs[i + 1]] for i in range(len(sizes))]
    pad = jnp.zeros((d, 512 - KV_LORA - 2 * D_ROPE), w.dtype)
    w_att = jnp.concatenate([qa * (DHA ** -0.5 * LOG2E), ka, va], axis=1).astype(BF16)
    w_rest = jnp.concatenate([uv, xg, cq, ckv, kr, _swap_halves(kr), pad, gl], axis=1).astype(BF16)

    wuq = p['w_uq'][l].reshape(Q_LORA, HD, D_NOPE + D_ROPE)
    wuq_r = wuq[:, :, D_NOPE:]
    wuq_ext = jnp.concatenate([wuq[:, :, :D_NOPE], wuq_r, _swap_halves(wuq_r)], axis=-1)
    wuq_ext = wuq_ext.reshape(Q_LORA, HD * 2 * LANE).astype(BF16)

    lam_init = 0.8 - 0.6 * math.exp(-0.3 * l)
    lam = (jnp.exp(jnp.sum(p['lam_q1'][l] * p['lam_k1'][l])) - jnp.exp(jnp.sum(p['lam_q2'][l] * p['lam_k2'][l]))
           + lam_init)
    slopes = jnp.asarray(2.0 ** (-8.0 * np.arange(1, HA + 1) / HA) * LOG2E, F32)
    par = jnp.concatenate([slopes, lam.reshape(1), jnp.full((1,), 1.0 - lam_init, F32), jnp.zeros((2,), F32)])

    rw = p['router_w'][l].T
    rw_hi = rw.astype(BF16)
    rw_lo = (rw - rw_hi.astype(F32)).astype(BF16)
    bs = jnp.repeat(p['sgu_b'][l].T, LANE, axis=1)
    return dict(
        w_att=w_att, w_rest=w_rest, wuq=wuq_ext, wukv=p['w_ukv'][l].astype(BF16), par=par,
        sgu_w=p['sgu_w'][l].astype(BF16), sgu_bs=bs,
        rg_wr=p['rg_wr'][l].astype(BF16), rg_wi=p['rg_wi'][l].astype(BF16),
        w_branch=p['w_branch'][l].astype(BF16), w_out=p['w_out'][l].astype(BF16),
        rw_hi=rw_hi, rw_lo=rw_lo,
        exp_wg=p['exp_wg'][l].astype(BF16), exp_wu=p['exp_wu'][l].astype(BF16), exp_wd=p['exp_wd'][l].astype(BF16),
        sh_wg=p['sh_wg'][l].astype(BF16), sh_wu=p['sh_wu'][l].astype(BF16), sh_wd=p['sh_wd'][l].astype(BF16),
    )


def _rope_table(n_seq, seq):
    inv_freq = ROPE_THETA ** (-jnp.arange(0, D_ROPE, 2, dtype=F32) / D_ROPE)
    ang = jnp.arange(seq, dtype=F32)[:, None] * inv_freq[None, :]
    cos, sin = jnp.cos(ang), jnp.sin(ang)
    tab = jnp.concatenate([cos, cos, -sin, sin], axis=1)
    return jnp.tile(tab, (n_seq, 1))


def _trunk(x3, mods, p, preps):
    n_seq, seq, d = x3.shape
    n = n_seq * seq
    x = x3.reshape(n, d)
    tab = _rope_table(n_seq, seq)
    depth = len(preps)
    for l in range(depth):
        w = preps[l]
        mod = mods[l]
        mod1 = jnp.stack([mod[:, 1], mod[:, 0]], axis=1)
        mod2 = jnp.stack([mod[:, 4], mod[:, 3]], axis=1)
        g1 = mod[:, 2:3]
        g2 = mod[:, 5:6]

        att = _inproj(x, p['norm1_g'][l], mod1, w['w_att'], BF16, seq)
        rest = _inproj(x, p['norm1_g'][l], mod1, w['w_rest'], BF16, seq)
        ya = _diff_attn(att, w['par'], p['subln_g'][l], n_seq, seq)
        yb = _sgu(rest, p['sgu_ln_g'][l], p['sgu_ln_b'][l], w['sgu_w'], w['sgu_bs'], seq)
        hf, hr = _lru(rest, p['conv_w'][l], p['conv_b'][l], w['rg_wr'], p['rg_br'][l], w['rg_wi'],
                      p['rg_bi'][l], p['rg_lam'][l], seq)
        q, k, v = _mla_prep(rest, tab, p['q_norm_g'][l], p['kv_norm_g'][l], w['wuq'], w['wukv'], seq)
        yd = _mla_attn(q, k, v, n_seq, seq)
        merged = _merge(ya, yb, hf, hr, rest, yd, w['w_branch'], seq)
        x = _outproj(merged, w['w_out'], x, g1, seq)

        h2, h2g, eidx, wgt, rank, cnt = _router(x, p['norm2_g'][l], mod2, w['rw_hi'], w['rw_lo'],
                                                p['router_bias'][l], seq)
        counts = cnt[:, 0]
        pc = ((counts + MOE_BLK - 1) // MOE_BLK) * MOE_BLK
        pend = jnp.cumsum(pc).astype(I32)
        pstart = pend - pc
        eid = jnp.arange(N_EXPERTS, dtype=I32)
        dest = (jnp.sum(jnp.where(eidx[..., None] == eid, pstart, 0), axis=-1) + rank).reshape(-1).astype(I32)
        nblk = n * TOP_K // MOE_BLK + N_EXPERTS
        blk_row = jnp.arange(nblk, dtype=I32)[:, None] * MOE_BLK
        blk_e = jnp.minimum(jnp.sum((pend[None, :] <= blk_row).astype(I32), axis=1), N_EXPERTS - 1)
        nused = (pend[-1:] // MOE_BLK).astype(I32)
        xs = _dispatch(dest, pend, pc.astype(I32), h2g, nblk * MOE_BLK, seq)
        ys = _gmm(blk_e, nused, xs, w['exp_wg'], w['exp_wu'], w['exp_wd'])
        x = _shared_expert(h2, x, g2, w['sh_wg'], w['sh_wu'], w['sh_wd'], seq)
        x = _combine(dest, wgt.T, x, g2, ys, seq)
    return _final_norm(x, p['final_g'], seq).reshape(n_seq, seq, d)


def kernel(x_prompt, x_sample, c_prompt, c_sample, ada_w, ada_b, norm1_g, norm2_g, w_in, lam_q1, lam_k1, lam_q2, lam_k2, subln_g, sgu_ln_g, sgu_ln_b, sgu_w, sgu_b, conv_w, conv_b, rg_wr, rg_br, rg_wi, rg_bi, rg_lam, q_norm_g, kv_norm_g, w_uq, w_ukv, w_branch, w_out, router_w, router_bias, exp_wg, exp_wu, exp_wd, sh_wg, sh_wu, sh_wd, final_g):
    p = dict(ada_w=ada_w, ada_b=ada_b, norm1_g=norm1_g, norm2_g=norm2_g, w_in=w_in,
             lam_q1=lam_q1, lam_k1=lam_k1, lam_q2=lam_q2, lam_k2=lam_k2, subln_g=subln_g,
             sgu_ln_g=sgu_ln_g, sgu_ln_b=sgu_ln_b, sgu_w=sgu_w, sgu_b=sgu_b,
             conv_w=conv_w, conv_b=conv_b, rg_wr=rg_wr, rg_br=rg_br, rg_wi=rg_wi, rg_bi=rg_bi,
             rg_lam=rg_lam, q_norm_g=q_norm_g, kv_norm_g=kv_norm_g, w_uq=w_uq, w_ukv=w_ukv,
             w_branch=w_branch, w_out=w_out, router_w=router_w, router_bias=router_bias,
             exp_wg=exp_wg, exp_wu=exp_wu, exp_wd=exp_wd, sh_wg=sh_wg, sh_wu=sh_wu, sh_wd=sh_wd,
             final_g=final_g)
    depth, d, _ = ada_w.shape
    bp, bs = x_prompt.shape[0], x_sample.shape[0]
    assert bp + bs <= 8
    c8 = jnp.concatenate([c_prompt, c_sample, jnp.zeros((8 - bp - bs, d), F32)], axis=0)
    mod = _ada_mod(c8, ada_w, ada_b).reshape(depth, 8, 6, d)
    preps = [_prep_layer(l, p) for l in range(depth)]
    y_prompt = _trunk(x_prompt, mod[:, :bp], p, preps)
    y_sample = _trunk(x_sample, mod[:, bp:bp + bs], p, preps)
    return (y_prompt, y_sample)
```

```python
import functools
import math

import numpy as np
import jax
import jax.numpy as jnp
from jax import lax
from jax.experimental import pallas as pl
from jax.experimental.pallas import tpu as pltpu

F32 = jnp.float32
BF16 = jnp.bfloat16
I32 = jnp.int32

EPS = 1e-6
LOG2E = math.log2(math.e)
LANE = 128
SUBLANE = 8
VMEM_LIMIT = 48 * 1024 * 1024

HA = 4
DHA = 64
DVA = 2 * DHA
BRANCH_W = 512
SGU_CHUNK = 128
SGU_GROUPS = 4
LRU_W = 512
LRU_BLOCKS = 4
LRU_BW = LRU_W // LRU_BLOCKS
LRU_C = 8.0
HALO = 16
HD = 4
Q_LORA = 512
KV_LORA = 256
D_NOPE = 128
D_ROPE = 64
DV_D = 128
ROPE_THETA = 10000.0
N_EXPERTS = 64
TOP_K = 8
N_GROUPS = 8
TOPK_GROUPS = 4
GROUP_SZ = N_EXPERTS // N_GROUPS
ROUTE_SCALE = 2.5
MOE_BLK = 256
N_BRANCH = 4
COMBINE_PITCH = 24

R_UV = 0
R_XC = 1024
R_GATE = 1536
R_CQ = 2048
R_CKV = 2560
R_GL = 3072

NT_DIMS = (((1,), (1,)), ((), ()))


def _params(*sem):
    return pltpu.CompilerParams(dimension_semantics=sem, vmem_limit_bytes=VMEM_LIMIT)


def _ada_kernel(c_ref, w_ref, b_ref, o_ref):
    a = jax.nn.silu(c_ref[...]).astype(BF16)
    o_ref[...] = jnp.dot(a, w_ref[...].astype(BF16), preferred_element_type=F32) + b_ref[...]


def _ada_mod(c8, ada_w, ada_b):
    depth, d, n = ada_w.shape
    tn = 1024
    return pl.pallas_call(
        _ada_kernel,
        grid=(depth, n // tn),
        in_specs=[
            pl.BlockSpec((8, d), lambda l, j: (0, 0)),
            pl.BlockSpec((None, d, tn), lambda l, j: (l, 0, j)),
            pl.BlockSpec((None, 1, tn), lambda l, j: (l, 0, j)),
        ],
        out_specs=pl.BlockSpec((None, 8, tn), lambda l, j: (l, 0, j)),
        out_shape=jax.ShapeDtypeStruct((depth, 8, n), F32),
        compiler_params=_params("parallel", "parallel"),
    )(c8, ada_w, ada_b.reshape(depth, 1, n))


def _inproj_kernel(x_ref, g_ref, mod_ref, w_ref, o_ref, h_ref):
    @pl.when(pl.program_id(1) == 0)
    def _():
        x = x_ref[...]
        y = x * lax.rsqrt(jnp.mean(x * x, axis=-1, keepdims=True) + EPS) * g_ref[...]
        h_ref[...] = (y * (1.0 + mod_ref[0:1, :]) + mod_ref[1:2, :]).astype(BF16)

    o_ref[...] = jnp.dot(h_ref[...], w_ref[...], preferred_element_type=F32).astype(o_ref.dtype)


def _inproj(x, g, mod, w, out_dtype, seq):
    n, d = x.shape
    nc = w.shape[1]
    tm = min(1024, seq)
    tn = 1024 if nc % 1024 == 0 else 512
    tps = seq // tm
    return pl.pallas_call(
        _inproj_kernel,
        grid=(n // tm, nc // tn),
        in_specs=[
            pl.BlockSpec((tm, d), lambda i, j: (i, 0)),
            pl.BlockSpec((1, d), lambda i, j: (0, 0)),
            pl.BlockSpec((None, 2, d), lambda i, j: (i // tps, 0, 0)),
            pl.BlockSpec((d, tn), lambda i, j: (0, j)),
        ],
        out_specs=pl.BlockSpec((tm, tn), lambda i, j: (i, j)),
        out_shape=jax.ShapeDtypeStruct((n, nc), out_dtype),
        scratch_shapes=[pltpu.VMEM((tm, d), BF16)],
        compiler_params=_params("parallel", "arbitrary"),
    )(x, g.reshape(1, d), mod, w)


def _online_softmax_step(s, vx, m_ref, acc_ref, idx, tk):
    m_prev = m_ref[idx]
    m_new = jnp.maximum(m_prev, jnp.max(s, axis=-1, keepdims=True))
    alpha = jnp.exp2(m_prev - m_new)
    p = jnp.exp2(s - jnp.concatenate([m_new] * (tk // LANE), axis=1))
    pv = jnp.dot(p.astype(BF16), vx, preferred_element_type=F32)
    acc_ref[idx] = jnp.concatenate([alpha, alpha], axis=1) * acc_ref[idx] + pv
    m_ref[idx] = m_new


def _diff_attn_kernel(par_ref, q_ref, k_ref, v_ref, g_ref, o_ref, m_ref, acc_ref, *, tq, tk, nk):
    qi = pl.program_id(1)
    ki = pl.program_id(2)

    @pl.when(ki == 0)
    def _():
        m_ref[...] = jnp.full(m_ref.shape, -jnp.inf, F32)
        acc_ref[...] = jnp.zeros(acc_ref.shape, F32)

    lane = lax.broadcasted_iota(I32, (tq, LANE), 1)
    row = lax.broadcasted_iota(I32, (tq, tk), 0)
    col = lax.broadcasted_iota(I32, (tq, tk), 1)
    dist = jnp.abs(row - col + (qi * tq - ki * tk)).astype(F32)
    ones = jnp.ones((tk, LANE), BF16)
    for h in range(HA):
        q = q_ref[:, h * LANE:(h + 1) * LANE]
        k = k_ref[:, h * LANE:(h + 1) * LANE]
        vx = jnp.concatenate([v_ref[:, h * LANE:(h + 1) * LANE], ones], axis=1)
        bias = dist * par_ref[h]
        zero = jnp.zeros_like(q)
        for mi, qm in enumerate((jnp.where(lane < DHA, q, zero), jnp.where(lane < DHA, zero, q))):
            s = lax.dot_general(qm, k, NT_DIMS, preferred_element_type=F32) - bias
            _online_softmax_step(s, vx, m_ref, acc_ref, 2 * h + mi, tk)

    @pl.when(ki == nk - 1)
    def _():
        lam = par_ref[HA]
        for h in range(HA):
            a1 = acc_ref[2 * h]
            a2 = acc_ref[2 * h + 1]
            o = a1[:, :LANE] / a1[:, LANE:] - lam * (a2[:, :LANE] / a2[:, LANE:])
            y = o * lax.rsqrt(jnp.mean(o * o, axis=-1, keepdims=True) + EPS) * g_ref[...]
            o_ref[:, h * LANE:(h + 1) * LANE] = (y * par_ref[HA + 1]).astype(o_ref.dtype)


def _diff_attn(att, par, subln_g, n_seq, seq):
    n = att.shape[0]
    tq = min(1024, seq)
    tk = min(512, seq)
    nq, nk = seq // tq, seq // tk
    w = HA * LANE
    kern = functools.partial(_diff_attn_kernel, tq=tq, tk=tk, nk=nk)
    return pl.pallas_call(
        kern,
        grid=(n_seq, nq, nk),
        in_specs=[
            pl.BlockSpec(memory_space=pltpu.SMEM),
            pl.BlockSpec((tq, w), lambda b, qi, ki: (b * nq + qi, 0)),
            pl.BlockSpec((tk, w), lambda b, qi, ki: (b * nk + ki, 1)),
            pl.BlockSpec((tk, w), lambda b, qi, ki: (b * nk + ki, 2)),
            pl.BlockSpec((1, DVA), lambda b, qi, ki: (0, 0)),
        ],
        out_specs=pl.BlockSpec((tq, w), lambda b, qi, ki: (b * nq + qi, 0)),
        out_shape=jax.ShapeDtypeStruct((n, HA * DVA), BF16),
        scratch_shapes=[
            pltpu.VMEM((2 * HA, tq, LANE), F32),
            pltpu.VMEM((2 * HA, tq, 2 * LANE), F32),
        ],
        compiler_params=_params("parallel", "parallel", "arbitrary"),
    )(par, att, att, att, subln_g.reshape(1, DVA))


def _mla_prep_kernel(cq_ref, ckv_ref, tab_ref, gq_ref, gkv_ref, wuq_ref, wukv_ref, q_ref, k_ref, v_ref, *, scale):
    cq = cq_ref[...].astype(F32)
    cqn = (cq * lax.rsqrt(jnp.mean(cq * cq, axis=-1, keepdims=True) + EPS) * gq_ref[...]).astype(BF16)
    c = ckv_ref[...].astype(F32)
    ckv = c[:, :KV_LORA]
    ckvn = (ckv * lax.rsqrt(jnp.mean(ckv * ckv, axis=-1, keepdims=True) + EPS) * gkv_ref[...]).astype(BF16)
    tab = tab_ref[...]
    lane = lax.broadcasted_iota(I32, tab.shape, 1)

    def rope(pair):
        pr = pair * tab
        return jnp.where(lane < D_ROPE, pr + pltpu.roll(pr, D_ROPE, 1), 0.0)

    kr = rope(c[:, KV_LORA:KV_LORA + LANE]).astype(BF16)
    qf = jnp.dot(cqn, wuq_ref[...], preferred_element_type=F32)
    kvf = jnp.dot(ckvn, wukv_ref[...], preferred_element_type=F32)
    for h in range(HD):
        b0 = h * 2 * LANE
        q_ref[:, b0:b0 + LANE] = (qf[:, b0:b0 + LANE] * scale).astype(BF16)
        q_ref[:, b0 + LANE:b0 + 2 * LANE] = (rope(qf[:, b0 + LANE:b0 + 2 * LANE]) * scale).astype(BF16)
        k_ref[:, b0:b0 + LANE] = kvf[:, b0:b0 + LANE].astype(BF16)
        k_ref[:, b0 + LANE:b0 + 2 * LANE] = kr
        v_ref[:, h * LANE:(h + 1) * LANE] = kvf[:, b0 + LANE:b0 + 2 * LANE].astype(BF16)


def _mla_prep(rest, tab, gq, gkv, wuq, wukv, seq):
    n = rest.shape[0]
    tm = min(512, seq)
    scale = (D_NOPE + D_ROPE) ** -0.5 * LOG2E
    kern = functools.partial(_mla_prep_kernel, scale=scale)
    wq = HD * 2 * LANE
    return pl.pallas_call(
        kern,
        grid=(n // tm,),
        in_specs=[
            pl.BlockSpec((tm, Q_LORA), lambda i: (i, R_CQ // Q_LORA)),
            pl.BlockSpec((tm, 512), lambda i: (i, R_CKV // 512)),
            pl.BlockSpec((tm, LANE), lambda i: (i, 0)),
            pl.BlockSpec((1, Q_LORA), lambda i: (0, 0)),
            pl.BlockSpec((1, KV_LORA), lambda i: (0, 0)),
            pl.BlockSpec((Q_LORA, wq), lambda i: (0, 0)),
            pl.BlockSpec((KV_LORA, wq), lambda i: (0, 0)),
        ],
        out_specs=[
            pl.BlockSpec((tm, wq), lambda i: (i, 0)),
            pl.BlockSpec((tm, wq), lambda i: (i, 0)),
            pl.BlockSpec((tm, HD * DV_D), lambda i: (i, 0)),
        ],
        out_shape=[
            jax.ShapeDtypeStruct((n, wq), BF16),
            jax.ShapeDtypeStruct((n, wq), BF16),
            jax.ShapeDtypeStruct((n, HD * DV_D), BF16),
        ],
        compiler_params=_params("parallel"),
    )(rest, rest, tab, gq.reshape(1, -1), gkv.reshape(1, -1), wuq, wukv)


def _mla_attn_kernel(q_ref, k_ref, v_ref, o_ref, m_ref, acc_ref, *, tk, nk):
    ki = pl.program_id(2)

    @pl.when(ki == 0)
    def _():
        m_ref[...] = jnp.full(m_ref.shape, -jnp.inf, F32)
        acc_ref[...] = jnp.zeros(acc_ref.shape, F32)

    ones = jnp.ones((tk, LANE), BF16)
    for h in range(HD):
        q = q_ref[:, h * 2 * LANE:(h + 1) * 2 * LANE]
        k = k_ref[:, h * 2 * LANE:(h + 1) * 2 * LANE]
        vx = jnp.concatenate([v_ref[:, h * LANE:(h + 1) * LANE], ones], axis=1)
        s = lax.dot_general(q, k, NT_DIMS, preferred_element_type=F32)
        _online_softmax_step(s, vx, m_ref, acc_ref, h, tk)

    @pl.when(ki == nk - 1)
    def _():
        for h in range(HD):
            a = acc_ref[h]
            o_ref[:, h * LANE:(h + 1) * LANE] = (a[:, :LANE] / a[:, LANE:]).astype(o_ref.dtype)


def _mla_attn(q, k, v, n_seq, seq):
    n = q.shape[0]
    tq = min(1024, seq)
    tk = min(1024, seq)
    nq, nk = seq // tq, seq // tk
    wq = HD * 2 * LANE
    kern = functools.partial(_mla_attn_kernel, tk=tk, nk=nk)
    return pl.pallas_call(
        kern,
        grid=(n_seq, nq, nk),
        in_specs=[
            pl.BlockSpec((tq, wq), lambda b, qi, ki: (b * nq + qi, 0)),
            pl.BlockSpec((tk, wq), lambda b, qi, ki: (b * nk + ki, 0)),
            pl.BlockSpec((tk, HD * DV_D), lambda b, qi, ki: (b * nk + ki, 0)),
        ],
        out_specs=pl.BlockSpec((tq, HD * DV_D), lambda b, qi, ki: (b * nq + qi, 0)),
        out_shape=jax.ShapeDtypeStruct((n, HD * DV_D), BF16),
        scratch_shapes=[
            pltpu.VMEM((HD, tq, LANE), F32),
            pltpu.VMEM((HD, tq, 2 * LANE), F32),
        ],
        compiler_params=_params("parallel", "parallel", "arbitrary"),
    )(q, k, v)


def _sgu_kernel(uv_ref, g_ref, b_ref, ws_ref, bs_ref, o_ref, *, tm):
    z = jax.nn.gelu(uv_ref[...].astype(F32))
    u = z[:, :BRANCH_W]
    v = z[:, BRANCH_W:]
    mu = jnp.mean(v, axis=-1, keepdims=True)
    vc = v - mu
    var = jnp.mean(vc * vc, axis=-1, keepdims=True)
    vn = (vc * lax.rsqrt(var + EPS) * g_ref[...] + b_ref[...]).astype(BF16)
    for c in range(tm // SGU_CHUNK):
        r0 = c * SGU_CHUNK
        for g in range(SGU_GROUPS):
            c0 = g * LANE
            vm = jnp.dot(ws_ref[g], vn[r0:r0 + SGU_CHUNK, c0:c0 + LANE], preferred_element_type=F32)
            vm = vm + bs_ref[:, c0:c0 + LANE]
            o_ref[r0:r0 + SGU_CHUNK, c0:c0 + LANE] = (u[r0:r0 + SGU_CHUNK, c0:c0 + LANE] * vm).astype(o_ref.dtype)


def _sgu(rest, ln_g, ln_b, ws, bs, seq):
    n = rest.shape[0]
    tm = min(512, seq)
    kern = functools.partial(_sgu_kernel, tm=tm)
    return pl.pallas_call(
        kern,
        grid=(n // tm,),
        in_specs=[
            pl.BlockSpec((tm, 2 * BRANCH_W), lambda i: (i, R_UV // (2 * BRANCH_W))),
            pl.BlockSpec((1, BRANCH_W), lambda i: (0, 0)),
            pl.BlockSpec((1, BRANCH_W), lambda i: (0, 0)),
            pl.BlockSpec((SGU_GROUPS, SGU_CHUNK, SGU_CHUNK), lambda i: (0, 0, 0)),
            pl.BlockSpec((SGU_CHUNK, BRANCH_W), lambda i: (0, 0)),
        ],
        out_specs=pl.BlockSpec((tm, BRANCH_W), lambda i: (i, 0)),
        out_shape=jax.ShapeDtypeStruct((n, BRANCH_W), BF16),
        compiler_params=_params("parallel"),
    )(rest, ln_g.reshape(1, -1), ln_b.reshape(1, -1), ws, bs)


def _lru_kernel(xf_ref, xfp_ref, xfn_ref, xr_ref, xrp_ref, xrn_ref, cw_ref, cb_ref, wr_ref, br_ref, wi_ref,
                bi_ref, lam_ref, hf_ref, hr_ref, af_ref, uf_ref, ar_ref, ur_ref, hc_ref, *, tt, tps):
    j = pl.program_id(0) % tps
    rowi = lax.broadcasted_iota(I32, (tt, LRU_W), 0)

    def gates(x_ref, xp_ref, xn_ref, at_start, at_end, d, a_ref, u_ref):
        x = x_ref[...].astype(F32)
        prev = jnp.where(at_start, 0.0, xp_ref[...].astype(F32)[HALO - 1:HALO, :])
        nxt = jnp.where(at_end, 0.0, xn_ref[...].astype(F32)[0:2, :])
        xm1 = jnp.where(rowi == 0, prev, pltpu.roll(x, 1, 0))
        xp1 = jnp.where(rowi == tt - 1, nxt[0:1, :], pltpu.roll(x, tt - 1, 0))
        xp2 = jnp.where(rowi == tt - 2, nxt[0:1, :],
                        jnp.where(rowi == tt - 1, nxt[1:2, :], pltpu.roll(x, tt - 2, 0)))
        xc = xm1 * cw_ref[0:1, :] + x * cw_ref[1:2, :] + xp1 * cw_ref[2:3, :] + xp2 * cw_ref[3:4, :] + cb_ref[...]
        xcb = xc.astype(BF16)

        def blockdiag(w_ref):
            return jnp.concatenate(
                [jnp.dot(xcb[:, g * LRU_BW:(g + 1) * LRU_BW], w_ref[d, g], preferred_element_type=F32)
                 for g in range(LRU_BLOCKS)], axis=1)

        r = jax.nn.sigmoid(blockdiag(wr_ref) + br_ref[d:d + 1, :])
        ig = jax.nn.sigmoid(blockdiag(wi_ref) + bi_ref[d:d + 1, :])
        log_a = (-LRU_C * r) * jax.nn.softplus(-lam_ref[d:d + 1, :])
        a = jnp.exp(log_a)
        u = jnp.sqrt(jnp.tanh(-log_a) * (1.0 + a * a)) * (ig * xc)
        a_ref[...] = a
        u_ref[...] = u

    gates(xf_ref, xfp_ref, xfn_ref, j == 0, j == tps - 1, 0, af_ref, uf_ref)
    gates(xr_ref, xrp_ref, xrn_ref, j == tps - 1, j == 0, 1, ar_ref, ur_ref)

    @pl.when(j == 0)
    def _():
        hc_ref[...] = jnp.zeros(hc_ref.shape, F32)

    ng = tt // SUBLANE
    sub = lax.broadcasted_iota(I32, (SUBLANE, LRU_W), 0)

    def body(g, carry):
        hf_prev, hr_prev = carry
        r0 = pl.multiple_of(g * SUBLANE, SUBLANE)
        a = af_ref[pl.ds(r0, SUBLANE), :]
        u = uf_ref[pl.ds(r0, SUBLANE), :]
        for s in (1, 2, 4):
            a_s = jnp.where(sub >= s, pltpu.roll(a, s, 0), 1.0)
            u_s = jnp.where(sub >= s, pltpu.roll(u, s, 0), 0.0)
            u = a * u_s + u
            a = a * a_s
        h = a * hf_prev + u
        hf_ref[pl.ds(r0, SUBLANE), :] = h
        hf_new = jnp.broadcast_to(h[SUBLANE - 1:SUBLANE, :], (SUBLANE, LRU_W))

        r1 = pl.multiple_of((ng - 1 - g) * SUBLANE, SUBLANE)
        a = ar_ref[pl.ds(r1, SUBLANE), :]
        u = ur_ref[pl.ds(r1, SUBLANE), :]
        for s in (1, 2, 4):
            a_s = jnp.where(sub < SUBLANE - s, pltpu.roll(a, SUBLANE - s, 0), 1.0)
            u_s = jnp.where(sub < SUBLANE - s, pltpu.roll(u, SUBLANE - s, 0), 0.0)
            u = a * u_s + u
            a = a * a_s
        h = a * hr_prev + u
        hr_ref[pl.ds(r1, SUBLANE), :] = h
        hr_new = jnp.broadcast_to(h[0:1, :], (SUBLANE, LRU_W))
        return hf_new, hr_new

    hf_c, hr_c = lax.fori_loop(0, ng, body, (hc_ref[0], hc_ref[1]))
    hc_ref[0] = hf_c
    hc_ref[1] = hr_c


def _lru(rest, cw, cb, wr, br, wi, bi, lam, seq):
    n = rest.shape[0]
    tt = min(512, seq)
    tps = seq // tt
    nt = n // tt
    t8 = tt // HALO
    last8 = n // HALO - 1
    cblk = R_XC // LRU_W

    def rev(i):
        return (i // tps) * tps + (tps - 1 - i % tps)

    kern = functools.partial(_lru_kernel, tt=tt, tps=tps)
    full = lambda shape: pl.BlockSpec(shape, lambda i: (0,) * len(shape))
    return pl.pallas_call(
        kern,
        grid=(nt,),
        in_specs=[
            pl.BlockSpec((tt, LRU_W), lambda i: (i, cblk)),
            pl.BlockSpec((HALO, LRU_W), lambda i: (jnp.maximum(i * t8 - 1, 0), cblk)),
            pl.BlockSpec((HALO, LRU_W), lambda i: (jnp.minimum((i + 1) * t8, last8), cblk)),
            pl.BlockSpec((tt, LRU_W), lambda i: (rev(i), cblk)),
            pl.BlockSpec((HALO, LRU_W), lambda i: (jnp.maximum(rev(i) * t8 - 1, 0), cblk)),
            pl.BlockSpec((HALO, LRU_W), lambda i: (jnp.minimum((rev(i) + 1) * t8, last8), cblk)),
            full((4, LRU_W)),
            full((1, LRU_W)),
            full((2, LRU_BLOCKS, LRU_BW, LRU_BW)),
            full((2, LRU_W)),
            full((2, LRU_BLOCKS, LRU_BW, LRU_BW)),
            full((2, LRU_W)),
            full((2, LRU_W)),
        ],
        out_specs=[
            pl.BlockSpec((tt, LRU_W), lambda i: (i, 0)),
            pl.BlockSpec((tt, LRU_W), lambda i: (rev(i), 0)),
        ],
        out_shape=[jax.ShapeDtypeStruct((n, LRU_W), F32), jax.ShapeDtypeStruct((n, LRU_W), F32)],
        scratch_shapes=[pltpu.VMEM((tt, LRU_W), F32)] * 4 + [pltpu.VMEM((2, SUBLANE, LRU_W), F32)],
        compiler_params=_params("arbitrary"),
    )(rest, rest, rest, rest, rest, rest, cw, cb.reshape(1, -1), wr, br, wi, bi, lam)


def _merge_kernel(ya_ref, yb_ref, hf_ref, hr_ref, gate_ref, yd_ref, wbr_ref, g0_ref, g1_ref, g2_ref, g3_ref, o_ref):
    yc = (jax.nn.gelu(gate_ref[...].astype(F32)) * (hf_ref[...] + hr_ref[...])).astype(BF16)
    ys = (ya_ref[...], yb_ref[...], yc, yd_ref[...])
    gls = (g0_ref, g1_ref, g2_ref, g3_ref)
    acc = None
    for k in range(N_BRANCH):
        t = jax.nn.sigmoid(gls[k][...].astype(F32)) * jnp.dot(ys[k], wbr_ref[k], preferred_element_type=F32)
        acc = t if acc is None else acc + t
    o_ref[...] = acc.astype(o_ref.dtype)


def _merge(ya, yb, hf, hr, rest, yd, wbr, seq):
    n = ya.shape[0]
    d = wbr.shape[-1]
    tm = min(512, seq)
    tn = 512
    row = lambda w: pl.BlockSpec((tm, w), lambda i, j: (i, 0))

    def gl_spec(k):
        base = (R_GL + k * d) // tn
        return pl.BlockSpec((tm, tn), lambda i, j: (i, base + j))

    return pl.pallas_call(
        _merge_kernel,
        grid=(n // tm, d // tn),
        in_specs=[
            row(BRANCH_W), row(BRANCH_W), row(LRU_W), row(LRU_W),
            pl.BlockSpec((tm, LRU_W), lambda i, j: (i, R_GATE // LRU_W)),
            row(BRANCH_W),
            pl.BlockSpec((N_BRANCH, BRANCH_W, tn), lambda i, j: (0, 0, j)),
            gl_spec(0), gl_spec(1), gl_spec(2), gl_spec(3),
        ],
        out_specs=pl.BlockSpec((tm, tn), lambda i, j: (i, j)),
        out_shape=jax.ShapeDtypeStruct((n, d), BF16),
        compiler_params=_params("parallel", "parallel"),
    )(ya, yb, hf, hr, rest, yd, wbr, rest, rest, rest, rest)


def _outproj_kernel(m_ref, w_ref, x_ref, g_ref, o_ref):
    o_ref[...] = x_ref[...] + g_ref[...] * jnp.dot(m_ref[...], w_ref[...], preferred_element_type=F32)


def _outproj(merged, w, x, g1, seq):
    n, d = x.shape
    tm = min(512, seq)
    tps = seq // tm
    return pl.pallas_call(
        _outproj_kernel,
        grid=(n // tm,),
        in_specs=[
            pl.BlockSpec((tm, d), lambda i: (i, 0)),
            pl.BlockSpec((d, d), lambda i: (0, 0)),
            pl.BlockSpec((tm, d), lambda i: (i, 0)),
            pl.BlockSpec((None, 1, d), lambda i: (i // tps, 0, 0)),
        ],
        out_specs=pl.BlockSpec((tm, d), lambda i: (i, 0)),
        out_shape=jax.ShapeDtypeStruct((n, d), F32),
        compiler_params=_params("parallel"),
    )(merged, w, x, g1)


def _router_kernel(x_ref, g_ref, mod_ref, wh_ref, wl_ref, rb_ref, tri_ref,
                   h2_ref, h2g_ref, e_ref, w_ref, r_ref, cnt_ref, run_ref, *, tm, d):
    @pl.when(pl.program_id(0) == 0)
    def _():
        run_ref[...] = jnp.zeros(run_ref.shape, F32)

    x = x_ref[...]
    y = x * lax.rsqrt(jnp.mean(x * x, axis=-1, keepdims=True) + EPS) * g_ref[...]
    h = y * (1.0 + mod_ref[0:1, :]) + mod_ref[1:2, :]
    hb = h.astype(BF16)
    h2_ref[...] = hb
    slabs = d // LANE
    for s in range(slabs):
        h2g_ref[pl.ds(s, tm, stride=slabs), :] = h[:, s * LANE:(s + 1) * LANE]

    hl = (h - hb.astype(F32)).astype(BF16)
    logits = (lax.dot_general(wh_ref[...], hb, NT_DIMS, preferred_element_type=F32)
              + lax.dot_general(wl_ref[...], hb, NT_DIMS, preferred_element_type=F32)
              + lax.dot_general(wh_ref[...], hl, NT_DIMS, preferred_element_type=F32))
    sc = jax.nn.sigmoid(logits)
    sel = sc + rb_ref[...]

    neg = -jnp.inf
    i8 = lax.broadcasted_iota(I32, (GROUP_SZ, tm), 0).astype(F32)
    rows = []
    for g in range(N_GROUPS):
        blk = sel[g * GROUP_SZ:(g + 1) * GROUP_SZ, :]
        m1 = jnp.max(blk, axis=0, keepdims=True)
        i1 = jnp.min(jnp.where(blk == m1, i8, float(GROUP_SZ)), axis=0, keepdims=True)
        m2 = jnp.max(jnp.where(i8 == i1, neg, blk), axis=0, keepdims=True)
        rows.append(m1 + m2)
    gs = jnp.concatenate(rows, axis=0)
    gi8 = lax.broadcasted_iota(I32, (N_GROUPS, tm), 0).astype(F32)
    gsel = jnp.zeros((N_GROUPS, tm), F32)
    for _ in range(TOPK_GROUPS):
        gm = jnp.max(gs, axis=0, keepdims=True)
        gi = jnp.min(jnp.where(gs == gm, gi8, float(N_GROUPS)), axis=0, keepdims=True)
        hit = gi8 == gi
        gsel = jnp.where(hit, 1.0, gsel)
        gs = jnp.where(hit, neg, gs)
    emask = jnp.concatenate(
        [jnp.broadcast_to(gsel[g:g + 1, :], (GROUP_SZ, tm)) for g in range(N_GROUPS)], axis=0)
    selm = jnp.where(emask > 0.0, sel, neg)

    i64 = lax.broadcasted_iota(I32, (N_EXPERTS, tm), 0).astype(F32)
    chosen = jnp.zeros((N_EXPERTS, tm), F32)
    idxs, wts = [], []
    for _ in range(TOP_K):
        mx = jnp.max(selm, axis=0, keepdims=True)
        ix = jnp.min(jnp.where(selm == mx, i64, float(N_EXPERTS)), axis=0, keepdims=True)
        oh = i64 == ix
        wts.append(jnp.sum(jnp.where(oh, sc, 0.0), axis=0, keepdims=True))
        idxs.append(ix)
        selm = jnp.where(oh, neg, selm)
        chosen = jnp.where(oh, 1.0, chosen)
    wk = jnp.concatenate(wts, axis=0)
    w_ref[...] = wk / jnp.sum(wk, axis=0, keepdims=True) * ROUTE_SCALE
    e_ref[...] = jnp.concatenate(idxs, axis=0).astype(I32)

    before = jnp.dot(chosen.astype(BF16), tri_ref[...], preferred_element_type=F32)
    rank = run_ref[...] + before
    r_ref[...] = jnp.concatenate(
        [jnp.sum(jnp.where(i64 == idxs[k], rank, 0.0), axis=0, keepdims=True) for k in range(TOP_K)],
        axis=0).astype(I32)
    run_ref[...] = run_ref[...] + jnp.sum(chosen, axis=1, keepdims=True)
    cnt_ref[...] = jnp.broadcast_to(run_ref[...], cnt_ref.shape).astype(I32)


def _router(x, g, mod, wh, wl, rb, seq):
    n, d = x.shape
    tm = min(512, seq)
    tps = seq // tm
    slabs = d // LANE
    tri = jnp.triu(jnp.ones((tm, tm), F32), 1).astype(BF16)
    kern = functools.partial(_router_kernel, tm=tm, d=d)
    return pl.pallas_call(
        kern,
        grid=(n // tm,),
        in_specs=[
            pl.BlockSpec((tm, d), lambda i: (i, 0)),
            pl.BlockSpec((1, d), lambda i: (0, 0)),
            pl.BlockSpec((None, 2, d), lambda i: (i // tps, 0, 0)),
            pl.BlockSpec((N_EXPERTS, d), lambda i: (0, 0)),
            pl.BlockSpec((N_EXPERTS, d), lambda i: (0, 0)),
            pl.BlockSpec((N_EXPERTS, 1), lambda i: (0, 0)),
            pl.BlockSpec((tm, tm), lambda i: (0, 0)),
        ],
        out_specs=[
            pl.BlockSpec((tm, d), lambda i: (i, 0)),
            pl.BlockSpec((tm * slabs, LANE), lambda i: (i, 0)),
            pl.BlockSpec((TOP_K, tm), lambda i: (0, i)),
            pl.BlockSpec((TOP_K, tm), lambda i: (0, i)),
            pl.BlockSpec((TOP_K, tm), lambda i: (0, i)),
            pl.BlockSpec((N_EXPERTS, LANE), lambda i: (0, 0)),
        ],
        out_shape=[
            jax.ShapeDtypeStruct((n, d), BF16),
            jax.ShapeDtypeStruct((n * slabs, LANE), F32),
            jax.ShapeDtypeStruct((TOP_K, n), I32),
            jax.ShapeDtypeStruct((TOP_K, n), F32),
            jax.ShapeDtypeStruct((TOP_K, n), I32),
            jax.ShapeDtypeStruct((N_EXPERTS, LANE), I32),
        ],
        scratch_shapes=[pltpu.VMEM((N_EXPERTS, 1), F32)],
        compiler_params=_params("arbitrary"),
    )(x, g.reshape(1, d), mod, wh, wl, rb.reshape(N_EXPERTS, 1), tri)


def _dispatch_kernel(dest_ref, pend_ref, pc_ref, h_ref, xs_ref, zero_ref, sem, zsem, *, tm, n, slabs):
    i = pl.program_id(0)
    rows_blk = MOE_BLK * slabs

    def tail_copy(e):
        start = pl.multiple_of((pend_ref[e] - MOE_BLK) * slabs, rows_blk)
        return pltpu.make_async_copy(zero_ref, xs_ref.at[pl.ds(start, rows_blk), :], zsem)

    @pl.when(i == 0)
    def _():
        zero_ref[...] = jnp.zeros(zero_ref.shape, F32)

        def zstart(e, c):
            @pl.when(pc_ref[e] > 0)
            def _():
                tail_copy(e).start()
            return c

        def zwait(e, c):
            @pl.when(pc_ref[e] > 0)
            def _():
                tail_copy(e).wait()
            return c

        lax.fori_loop(0, N_EXPERTS, zstart, 0)
        lax.fori_loop(0, N_EXPERTS, zwait, 0)

    def row_copy(r, k):
        src = h_ref.at[pl.ds(pl.multiple_of(r * slabs, slabs), slabs), :]
        dst_row = dest_ref[k * n + i * tm + r]
        dst = xs_ref.at[pl.ds(pl.multiple_of(dst_row * slabs, slabs), slabs), :]
        return pltpu.make_async_copy(src, dst, sem)

    def start(r, c):
        for k in range(TOP_K):
            row_copy(r, k).start(priority=k % 2)
        return c

    lax.fori_loop(0, tm, start, 0)
    for _ in range(TOP_K):
        pltpu.make_async_copy(h_ref, xs_ref.at[pl.ds(0, tm * slabs), :], sem).wait()


def _dispatch(dest, pend, pc, h2g, n_rows, seq):
    n = dest.shape[0] // TOP_K
    slabs = h2g.shape[0] // n
    tm = min(256, seq)
    kern = functools.partial(_dispatch_kernel, tm=tm, n=n, slabs=slabs)
    return pl.pallas_call(
        kern,
        grid_spec=pltpu.PrefetchScalarGridSpec(
            num_scalar_prefetch=3,
            grid=(n // tm,),
            in_specs=[pl.BlockSpec((tm * slabs, LANE), lambda i, *_: (i, 0))],
            out_specs=pl.BlockSpec(memory_space=pl.ANY),
            scratch_shapes=[
                pltpu.VMEM((MOE_BLK * slabs, LANE), F32),
                pltpu.SemaphoreType.DMA(()),
                pltpu.SemaphoreType.DMA(()),
            ],
        ),
        out_shape=jax.ShapeDtypeStruct((n_rows * slabs, LANE), F32),
        compiler_params=_params("arbitrary"),
    )(dest, pend, pc, h2g)


def _gmm_kernel(be_ref, nu_ref, x_ref, wg_ref, wu_ref, wd_ref, o_ref, *, slabs):
    @pl.when(pl.program_id(0) < nu_ref[0])
    def _():
        x = jnp.concatenate(
            [x_ref[pl.ds(s, MOE_BLK, stride=slabs), :].astype(BF16) for s in range(slabs)], axis=1)
        hg = jnp.dot(x, wg_ref[...], preferred_element_type=F32)
        hu = jnp.dot(x, wu_ref[...], preferred_element_type=F32)
        hb = (jax.nn.silu(hg) * hu).astype(BF16)
        per = 4
        for c in range(slabs // per):
            y = jnp.dot(hb, wd_ref[:, c * per * LANE:(c + 1) * per * LANE], preferred_element_type=F32)
            for j in range(per):
                o_ref[pl.ds(c * per + j, MOE_BLK, stride=slabs), :] = y[:, j * LANE:(j + 1) * LANE]


def _gmm(blk_e, nused, xs, wg, wu, wd):
    _, d, de = wg.shape
    slabs = d // LANE
    nblk = xs.shape[0] // (MOE_BLK * slabs)
    kern = functools.partial(_gmm_kernel, slabs=slabs)

    def blk(b, be, nu):
        return jnp.minimum(b, nu[0] - 1)

    return pl.pallas_call(
        kern,
        grid_spec=pltpu.PrefetchScalarGridSpec(
            num_scalar_prefetch=2,
            grid=(nblk,),
            in_specs=[
                pl.BlockSpec((MOE_BLK * slabs, LANE), lambda b, be, nu: (blk(b, be, nu), 0)),
                pl.BlockSpec((None, d, de), lambda b, be, nu: (be[blk(b, be, nu)], 0, 0)),
                pl.BlockSpec((None, d, de), lambda b, be, nu: (be[blk(b, be, nu)], 0, 0)),
                pl.BlockSpec((None, de, d), lambda b, be, nu: (be[blk(b, be, nu)], 0, 0)),
            ],
            out_specs=pl.BlockSpec((MOE_BLK * slabs, LANE), lambda b, be, nu: (blk(b, be, nu), 0)),
        ),
        out_shape=jax.ShapeDtypeStruct(xs.shape, F32),
        compiler_params=_params("arbitrary"),
    )(blk_e, nused, xs, wg, wu, wd)


def _combine_kernel(dest_ref, wt_ref, x_ref, g2_ref, ys_ref, o_ref, buf_ref, wb_ref, sem, *, tm, n, slabs, nt):
    i = pl.program_id(0)
    slot = i % 2
    rows = tm * COMBINE_PITCH

    def gather(tile, sl):
        def start(r, c):
            for k in range(TOP_K):
                src_row = dest_ref[k * n + tile * tm + r]
                src = ys_ref.at[pl.ds(pl.multiple_of(src_row * slabs, slabs), slabs), :]
                dst = buf_ref.at[sl, pl.ds(pl.multiple_of(k * rows + r * COMBINE_PITCH, SUBLANE), slabs), :]
                pltpu.make_async_copy(src, dst, sem.at[sl]).start(priority=k % 2)
            return c

        lax.fori_loop(0, tm, start, 0)

    @pl.when(i == 0)
    def _():
        gather(0, 0)

    @pl.when(i + 1 < nt)
    def _():
        gather(i + 1, 1 - slot)

    moved = TOP_K * tm * slabs
    pltpu.make_async_copy(ys_ref.at[pl.ds(0, moved), :], buf_ref.at[slot, pl.ds(0, moved), :], sem.at[slot]).wait()

    wt = wt_ref[...]
    for k in range(TOP_K):
        wb_ref[k] = jnp.broadcast_to(wt[:, k:k + 1], (tm, LANE))
    for s in range(slabs):
        c0 = s * LANE
        routed = None
        for k in range(TOP_K):
            t = wb_ref[k] * buf_ref[slot, pl.ds(k * rows + s, tm, stride=COMBINE_PITCH), :]
            routed = t if routed is None else routed + t
        o_ref[:, c0:c0 + LANE] = x_ref[:, c0:c0 + LANE] + g2_ref[:, c0:c0 + LANE] * routed


def _combine(dest, wt, x, g2, ys, seq):
    n, d = x.shape
    slabs = d // LANE
    tm = min(128, seq)
    tps = seq // tm
    assert COMBINE_PITCH >= slabs
    kern = functools.partial(_combine_kernel, tm=tm, n=n, slabs=slabs, nt=n // tm)
    return pl.pallas_call(
        kern,
        grid_spec=pltpu.PrefetchScalarGridSpec(
            num_scalar_prefetch=1,
            grid=(n // tm,),
            in_specs=[
                pl.BlockSpec((tm, TOP_K), lambda i, *_: (i, 0)),
                pl.BlockSpec((tm, d), lambda i, *_: (i, 0)),
                pl.BlockSpec((None, 1, d), lambda i, *_: (i // tps, 0, 0)),
                pl.BlockSpec(memory_space=pl.ANY),
            ],
            out_specs=pl.BlockSpec((tm, d), lambda i, *_: (i, 0)),
            scratch_shapes=[
                pltpu.VMEM((2, TOP_K * tm * COMBINE_PITCH, LANE), F32),
                pltpu.VMEM((TOP_K, tm, LANE), F32),
                pltpu.SemaphoreType.DMA((2,)),
            ],
        ),
        out_shape=jax.ShapeDtypeStruct((n, d), F32),
        compiler_params=_params("arbitrary"),
    )(dest, wt, x, g2, ys)


def _shared_kernel(h2_ref, x_ref, g2_ref, swg_ref, swu_ref, swd_ref, o_ref):
    h2 = h2_ref[...]
    hs = (jax.nn.silu(jnp.dot(h2, swg_ref[...], preferred_element_type=F32))
          * jnp.dot(h2, swu_ref[...], preferred_element_type=F32)).astype(BF16)
    o_ref[...] = x_ref[...] + g2_ref[...] * jnp.dot(hs, swd_ref[...], preferred_element_type=F32)


def _shared_expert(h2, x, g2, swg, swu, swd, seq):
    n, d = x.shape
    ds_ = swg.shape[1]
    tm = min(512, seq)
    tps = seq // tm
    return pl.pallas_call(
        _shared_kernel,
        grid=(n // tm,),
        in_specs=[
            pl.BlockSpec((tm, d), lambda i: (i, 0)),
            pl.BlockSpec((tm, d), lambda i: (i, 0)),
            pl.BlockSpec((None, 1, d), lambda i: (i // tps, 0, 0)),
            pl.BlockSpec((d, ds_), lambda i: (0, 0)),
            pl.BlockSpec((d, ds_), lambda i: (0, 0)),
            pl.BlockSpec((ds_, d), lambda i: (0, 0)),
        ],
        out_specs=pl.BlockSpec((tm, d), lambda i: (i, 0)),
        out_shape=jax.ShapeDtypeStruct((n, d), F32),
        compiler_params=_params("parallel"),
    )(h2, x, g2, swg, swu, swd)


def _final_norm_kernel(x_ref, g_ref, o_ref):
    x = x_ref[...]
    o_ref[...] = x * lax.rsqrt(jnp.mean(x * x, axis=-1, keepdims=True) + EPS) * g_ref[...]


def _final_norm(x, g, seq):
    n, d = x.shape
    tm = min(512, seq)
    return pl.pallas_call(
        _final_norm_kernel,
        grid=(n // tm,),
        in_specs=[pl.BlockSpec((tm, d), lambda i: (i, 0)), pl.BlockSpec((1, d), lambda i: (0, 0))],
        out_specs=pl.BlockSpec((tm, d), lambda i: (i, 0)),
        out_shape=jax.ShapeDtypeStruct((n, d), F32),
        compiler_params=_params("parallel"),
    )(x, g.reshape(1, d))


def _swap_halves(w):
    half = w.shape[-1] // 2
    return jnp.concatenate([w[..., half:], w[..., :half]], axis=-1)


def _prep_layer(l, p):
    d = p['w_in'].shape[1]
    sizes = [HA * 2 * DHA, HA * 2 * DHA, HA * DVA, 2 * BRANCH_W, 2 * LRU_W, Q_LORA, KV_LORA, D_ROPE, N_BRANCH * d]
    offs = np.cumsum([0] + sizes)
    w = p['w_in'][l]
    qa, ka, va, uv, xg, cq, ckv, kr, gl = [w[:, offs[i]:offs[i + 1]] for i in range(len(sizes))]
    pad = jnp.zeros((d, 512 - KV_LORA - 2 * D_ROPE), w.dtype)
    w_att = jnp.concatenate([qa * (DHA ** -0.5 * LOG2E), ka, va], axis=1).astype(BF16)
    w_rest = jnp.concatenate([uv, xg, cq, ckv, kr, _swap_halves(kr), pad, gl], axis=1).astype(BF16)

    wuq = p['w_uq'][l].reshape(Q_LORA, HD, D_NOPE + D_ROPE)
    wuq_r = wuq[:, :, D_NOPE:]
    wuq_ext = jnp.concatenate([wuq[:, :, :D_NOPE], wuq_r, _swap_halves(wuq_r)], axis=-1)
    wuq_ext = wuq_ext.reshape(Q_LORA, HD * 2 * LANE).astype(BF16)

    lam_init = 0.8 - 0.6 * math.exp(-0.3 * l)
    lam = (jnp.exp(jnp.sum(p['lam_q1'][l] * p['lam_k1'][l])) - jnp.exp(jnp.sum(p['lam_q2'][l] * p['lam_k2'][l]))
           + lam_init)
    slopes = jnp.asarray(2.0 ** (-8.0 * np.arange(1, HA + 1) / HA) * LOG2E, F32)
    par = jnp.concatenate([slopes, lam.reshape(1), jnp.full((1,), 1.0 - lam_init, F32), jnp.zeros((2,), F32)])

    rw = p['router_w'][l].T
    rw_hi = rw.astype(BF16)
    rw_lo = (rw - rw_hi.astype(F32)).astype(BF16)
    bs = jnp.repeat(p['sgu_b'][l].T, LANE, axis=1)
    return dict(
        w_att=w_att, w_rest=w_rest, wuq=wuq_ext, wukv=p['w_ukv'][l].astype(BF16), par=par,
        sgu_w=p['sgu_w'][l].astype(BF16), sgu_bs=bs,
        rg_wr=p['rg_wr'][l].astype(BF16), rg_wi=p['rg_wi'][l].astype(BF16),
        w_branch=p['w_branch'][l].astype(BF16), w_out=p['w_out'][l].astype(BF16),
        rw_hi=rw_hi, rw_lo=rw_lo,
        exp_wg=p['exp_wg'][l].astype(BF16), exp_wu=p['exp_wu'][l].astype(BF16), exp_wd=p['exp_wd'][l].astype(BF16),
        sh_wg=p['sh_wg'][l].astype(BF16), sh_wu=p['sh_wu'][l].astype(BF16), sh_wd=p['sh_wd'][l].astype(BF16),
    )


def _rope_table(n_seq, seq):
    inv_freq = ROPE_THETA ** (-jnp.arange(0, D_ROPE, 2, dtype=F32) / D_ROPE)
    ang = jnp.arange(seq, dtype=F32)[:, None] * inv_freq[None, :]
    cos, sin = jnp.cos(ang), jnp.sin(ang)
    tab = jnp.concatenate([cos, cos, -sin, sin], axis=1)
    return jnp.tile(tab, (n_seq, 1))


def _trunk(x3, mods, p, preps):
    n_seq, seq, d = x3.shape
    n = n_seq * seq
    x = x3.reshape(n, d)
    tab = _rope_table(n_seq, seq)
    depth = len(preps)
    for l in range(depth):
        w = preps[l]
        mod = mods[l]
        mod1 = jnp.stack([mod[:, 1], mod[:, 0]], axis=1)
        mod2 = jnp.stack([mod[:, 4], mod[:, 3]], axis=1)
        g1 = mod[:, 2:3]
        g2 = mod[:, 5:6]

        att = _inproj(x, p['norm1_g'][l], mod1, w['w_att'], BF16, seq)
        rest = _inproj(x, p['norm1_g'][l], mod1, w['w_rest'], BF16, seq)
        ya = _diff_attn(att, w['par'], p['subln_g'][l], n_seq, seq)
        yb = _sgu(rest, p['sgu_ln_g'][l], p['sgu_ln_b'][l], w['sgu_w'], w['sgu_bs'], seq)
        hf, hr = _lru(rest, p['conv_w'][l], p['conv_b'][l], w['rg_wr'], p['rg_br'][l], w['rg_wi'],
                      p['rg_bi'][l], p['rg_lam'][l], seq)
        q, k, v = _mla_prep(rest, tab, p['q_norm_g'][l], p['kv_norm_g'][l], w['wuq'], w['wukv'], seq)
        yd = _mla_attn(q, k, v, n_seq, seq)
        merged = _merge(ya, yb, hf, hr, rest, yd, w['w_branch'], seq)
        x = _outproj(merged, w['w_out'], x, g1, seq)

        h2, h2g, eidx, wgt, rank, cnt = _router(x, p['norm2_g'][l], mod2, w['rw_hi'], w['rw_lo'],
                                                p['router_bias'][l], seq)
        counts = cnt[:, 0]
        pc = ((counts + MOE_BLK - 1) // MOE_BLK) * MOE_BLK
        pend = jnp.cumsum(pc).astype(I32)
        pstart = pend - pc
        eid = jnp.arange(N_EXPERTS, dtype=I32)
        dest = (jnp.sum(jnp.where(eidx[..., None] == eid, pstart, 0), axis=-1) + rank).reshape(-1).astype(I32)
        nblk = n * TOP_K // MOE_BLK + N_EXPERTS
        blk_row = jnp.arange(nblk, dtype=I32)[:, None] * MOE_BLK
        blk_e = jnp.minimum(jnp.sum((pend[None, :] <= blk_row).astype(I32), axis=1), N_EXPERTS - 1)
        nused = (pend[-1:] // MOE_BLK).astype(I32)
        xs = _dispatch(dest, pend, pc.astype(I32), h2g, nblk * MOE_BLK, seq)
        ys = _gmm(blk_e, nused, xs, w['exp_wg'], w['exp_wu'], w['exp_wd'])
        x = _shared_expert(h2, x, g2, w['sh_wg'], w['sh_wu'], w['sh_wd'], seq)
        x = _combine(dest, wgt.T, x, g2, ys, seq)
    return _final_norm(x, p['final_g'], seq).reshape(n_seq, seq, d)


def kernel(x_prompt, x_sample, c_prompt, c_sample, ada_w, ada_b, norm1_g, norm2_g, w_in, lam_q1, lam_k1, lam_q2, lam_k2, subln_g, sgu_ln_g, sgu_ln_b, sgu_w, sgu_b, conv_w, conv_b, rg_wr, rg_br, rg_wi, rg_bi, rg_lam, q_norm_g, kv_norm_g, w_uq, w_ukv, w_branch, w_out, router_w, router_bias, exp_wg, exp_wu, exp_wd, sh_wg, sh_wu, sh_wd, final_g):
    p = dict(ada_w=ada_w, ada_b=ada_b, norm1_g=norm1_g, norm2_g=norm2_g, w_in=w_in,
             lam_q1=lam_q1, lam_k1=lam_k1, lam_q2=lam_q2, lam_k2=lam_k2, subln_g=subln_g,
             sgu_ln_g=sgu_ln_g, sgu_ln_b=sgu_ln_b, sgu_w=sgu_w, sgu_b=sgu_b,
             conv_w=conv_w, conv_b=conv_b, rg_wr=rg_wr, rg_br=rg_br, rg_wi=rg_wi, rg_bi=rg_bi,
             rg_lam=rg_lam, q_norm_g=q_norm_g, kv_norm_g=kv_norm_g, w_uq=w_uq, w_ukv=w_ukv,
             w_branch=w_branch, w_out=w_out, router_w=router_w, router_bias=router_bias,
             exp_wg=exp_wg, exp_wu=exp_wu, exp_wd=exp_wd, sh_wg=sh_wg, sh_wu=sh_wu, sh_wd=sh_wd,
             final_g=final_g)
    depth, d, _ = ada_w.shape
    bp, bs = x_prompt.shape[0], x_sample.shape[0]
    assert bp + bs <= 8
    c8 = jnp.concatenate([c_prompt, c_sample, jnp.zeros((8 - bp - bs, d), F32)], axis=0)
    mod = _ada_mod(c8, ada_w, ada_b).reshape(depth, 8, 6, d)
    preps = [_prep_layer(l, p) for l in range(depth)]
    y_prompt = _trunk(x_prompt, mod[:, :bp], p, preps)
    y_sample = _trunk(x_sample, mod[:, bp:bp + bs], p, preps)
    return (y_prompt, y_sample)
```

```python
import functools
import math

import numpy as np
import jax
import jax.numpy as jnp
from jax import lax
from jax.experimental import pallas as pl
from jax.experimental.pallas import tpu as pltpu

F32 = jnp.float32
BF16 = jnp.bfloat16
I32 = jnp.int32

EPS = 1e-6
LOG2E = math.log2(math.e)
LANE = 128
SUBLANE = 8
VMEM_LIMIT = 48 * 1024 * 1024

HA = 4
DHA = 64
DVA = 2 * DHA
BRANCH_W = 512
SGU_CHUNK = 128
SGU_GROUPS = 4
LRU_W = 512
LRU_BLOCKS = 4
LRU_BW = LRU_W // LRU_BLOCKS
LRU_C = 8.0
HALO = 16
HD = 4
Q_LORA = 512
KV_LORA = 256
D_NOPE = 128
D_ROPE = 64
DV_D = 128
ROPE_THETA = 10000.0
N_EXPERTS = 64
TOP_K = 8
N_GROUPS = 8
TOPK_GROUPS = 4
GROUP_SZ = N_EXPERTS // N_GROUPS
ROUTE_SCALE = 2.5
MOE_BLK = 256
N_BRANCH = 4
COMBINE_PITCH = 24

R_UV = 0
R_XC = 1024
R_GATE = 1536
R_CQ = 2048
R_CKV = 2560
R_GL = 3072

NT_DIMS = (((1,), (1,)), ((), ()))


def _params(*sem):
    return pltpu.CompilerParams(dimension_semantics=sem, vmem_limit_bytes=VMEM_LIMIT)


def _ada_kernel(c_ref, w_ref, b_ref, o_ref):
    a = jax.nn.silu(c_ref[...]).astype(BF16)
    o_ref[...] = jnp.dot(a, w_ref[...].astype(BF16), preferred_element_type=F32) + b_ref[...]


def _ada_mod(c8, ada_w, ada_b):
    depth, d, n = ada_w.shape
    tn = 1024
    return pl.pallas_call(
        _ada_kernel,
        grid=(depth, n // tn),
        in_specs=[
            pl.BlockSpec((8, d), lambda l, j: (0, 0)),
            pl.BlockSpec((None, d, tn), lambda l, j: (l, 0, j)),
            pl.BlockSpec((None, 1, tn), lambda l, j: (l, 0, j)),
        ],
        out_specs=pl.BlockSpec((None, 8, tn), lambda l, j: (l, 0, j)),
        out_shape=jax.ShapeDtypeStruct((depth, 8, n), F32),
        compiler_params=_params("parallel", "parallel"),
    )(c8, ada_w, ada_b.reshape(depth, 1, n))


def _inproj_kernel(x_ref, g_ref, mod_ref, w_ref, o_ref, h_ref):
    @pl.when(pl.program_id(1) == 0)
    def _():
        x = x_ref[...]
        y = x * lax.rsqrt(jnp.mean(x * x, axis=-1, keepdims=True) + EPS) * g_ref[...]
        h_ref[...] = (y * (1.0 + mod_ref[0:1, :]) + mod_ref[1:2, :]).astype(BF16)

    o_ref[...] = jnp.dot(h_ref[...], w_ref[...], preferred_element_type=F32).astype(o_ref.dtype)


def _inproj(x, g, mod, w, out_dtype, seq):
    n, d = x.shape
    nc = w.shape[1]
    tm = min(1024, seq)
    tn = 1024 if nc % 1024 == 0 else 512
    tps = seq // tm
    return pl.pallas_call(
        _inproj_kernel,
        grid=(n // tm, nc // tn),
        in_specs=[
            pl.BlockSpec((tm, d), lambda i, j: (i, 0)),
            pl.BlockSpec((1, d), lambda i, j: (0, 0)),
            pl.BlockSpec((None, 2, d), lambda i, j: (i // tps, 0, 0)),
            pl.BlockSpec((d, tn), lambda i, j: (0, j)),
        ],
        out_specs=pl.BlockSpec((tm, tn), lambda i, j: (i, j)),
        out_shape=jax.ShapeDtypeStruct((n, nc), out_dtype),
        scratch_shapes=[pltpu.VMEM((tm, d), BF16)],
        compiler_params=_params("parallel", "arbitrary"),
    )(x, g.reshape(1, d), mod, w)


def _online_softmax_step(s, vx, m_ref, acc_ref, idx, tk):
    m_prev = m_ref[idx]
    m_new = jnp.maximum(m_prev, jnp.max(s, axis=-1, keepdims=True))
    alpha = jnp.exp2(m_prev - m_new)
    p = jnp.exp2(s - jnp.concatenate([m_new] * (tk // LANE), axis=1))
    pv = jnp.dot(p.astype(BF16), vx, preferred_element_type=F32)
    acc_ref[idx] = jnp.concatenate([alpha, alpha], axis=1) * acc_ref[idx] + pv
    m_ref[idx] = m_new


def _diff_attn_kernel(par_ref, q_ref, k_ref, v_ref, g_ref, o_ref, m_ref, acc_ref, *, tq, tk, nk):
    qi = pl.program_id(1)
    ki = pl.program_id(2)

    @pl.when(ki == 0)
    def _():
        m_ref[...] = jnp.full(m_ref.shape, -jnp.inf, F32)
        acc_ref[...] = jnp.zeros(acc_ref.shape, F32)

    lane = lax.broadcasted_iota(I32, (tq, LANE), 1)
    row = lax.broadcasted_iota(I32, (tq, tk), 0)
    col = lax.broadcasted_iota(I32, (tq, tk), 1)
    dist = jnp.abs(row - col + (qi * tq - ki * tk)).astype(F32)
    ones = jnp.ones((tk, LANE), BF16)
    for h in range(HA):
        q = q_ref[:, h * LANE:(h + 1) * LANE]
        k = k_ref[:, h * LANE:(h + 1) * LANE]
        vx = jnp.concatenate([v_ref[:, h * LANE:(h + 1) * LANE], ones], axis=1)
        bias = dist * par_ref[h]
        zero = jnp.zeros_like(q)
        for mi, qm in enumerate((jnp.where(lane < DHA, q, zero), jnp.where(lane < DHA, zero, q))):
            s = lax.dot_general(qm, k, NT_DIMS, preferred_element_type=F32) - bias
            _online_softmax_step(s, vx, m_ref, acc_ref, 2 * h + mi, tk)

    @pl.when(ki == nk - 1)
    def _():
        lam = par_ref[HA]
        for h in range(HA):
            a1 = acc_ref[2 * h]
            a2 = acc_ref[2 * h + 1]
            o = a1[:, :LANE] / a1[:, LANE:] - lam * (a2[:, :LANE] / a2[:, LANE:])
            y = o * lax.rsqrt(jnp.mean(o * o, axis=-1, keepdims=True) + EPS) * g_ref[...]
            o_ref[:, h * LANE:(h + 1) * LANE] = (y * par_ref[HA + 1]).astype(o_ref.dtype)


def _diff_attn(att, par, subln_g, n_seq, seq):
    n = att.shape[0]
    tq = min(1024, seq)
    tk = min(1024, seq)
    nq, nk = seq // tq, seq // tk
    w = HA * LANE
    kern = functools.partial(_diff_attn_kernel, tq=tq, tk=tk, nk=nk)
    return pl.pallas_call(
        kern,
        grid=(n_seq, nq, nk),
        in_specs=[
            pl.BlockSpec(memory_space=pltpu.SMEM),
            pl.BlockSpec((tq, w), lambda b, qi, ki: (b * nq + qi, 0)),
            pl.BlockSpec((tk, w), lambda b, qi, ki: (b * nk + ki, 1)),
            pl.BlockSpec((tk, w), lambda b, qi, ki: (b * nk + ki, 2)),
            pl.BlockSpec((1, DVA), lambda b, qi, ki: (0, 0)),
        ],
        out_specs=pl.BlockSpec((tq, w), lambda b, qi, ki: (b * nq + qi, 0)),
        out_shape=jax.ShapeDtypeStruct((n, HA * DVA), BF16),
        scratch_shapes=[
            pltpu.VMEM((2 * HA, tq, LANE), F32),
            pltpu.VMEM((2 * HA, tq, 2 * LANE), F32),
        ],
        compiler_params=_params("parallel", "parallel", "arbitrary"),
    )(par, att, att, att, subln_g.reshape(1, DVA))


def _mla_prep_kernel(cq_ref, ckv_ref, tab_ref, gq_ref, gkv_ref, wuq_ref, wukv_ref, q_ref, k_ref, v_ref, *, scale):
    cq = cq_ref[...].astype(F32)
    cqn = (cq * lax.rsqrt(jnp.mean(cq * cq, axis=-1, keepdims=True) + EPS) * gq_ref[...]).astype(BF16)
    c = ckv_ref[...].astype(F32)
    ckv = c[:, :KV_LORA]
    ckvn = (ckv * lax.rsqrt(jnp.mean(ckv * ckv, axis=-1, keepdims=True) + EPS) * gkv_ref[...]).astype(BF16)
    tab = tab_ref[...]
    lane = lax.broadcasted_iota(I32, tab.shape, 1)

    def rope(pair):
        pr = pair * tab
        return jnp.where(lane < D_ROPE, pr + pltpu.roll(pr, D_ROPE, 1), 0.0)

    kr = rope(c[:, KV_LORA:KV_LORA + LANE]).astype(BF16)
    qf = jnp.dot(cqn, wuq_ref[...], preferred_element_type=F32)
    kvf = jnp.dot(ckvn, wukv_ref[...], preferred_element_type=F32)
    for h in range(HD):
        b0 = h * 2 * LANE
        q_ref[:, b0:b0 + LANE] = (qf[:, b0:b0 + LANE] * scale).astype(BF16)
        q_ref[:, b0 + LANE:b0 + 2 * LANE] = (rope(qf[:, b0 + LANE:b0 + 2 * LANE]) * scale).astype(BF16)
        k_ref[:, b0:b0 + LANE] = kvf[:, b0:b0 + LANE].astype(BF16)
        k_ref[:, b0 + LANE:b0 + 2 * LANE] = kr
        v_ref[:, h * LANE:(h + 1) * LANE] = kvf[:, b0 + LANE:b0 + 2 * LANE].astype(BF16)


def _mla_prep(rest, tab, gq, gkv, wuq, wukv, seq):
    n = rest.shape[0]
    tm = min(512, seq)
    scale = (D_NOPE + D_ROPE) ** -0.5 * LOG2E
    kern = functools.partial(_mla_prep_kernel, scale=scale)
    wq = HD * 2 * LANE
    return pl.pallas_call(
        kern,
        grid=(n // tm,),
        in_specs=[
            pl.BlockSpec((tm, Q_LORA), lambda i: (i, R_CQ // Q_LORA)),
            pl.BlockSpec((tm, 512), lambda i: (i, R_CKV // 512)),
            pl.BlockSpec((tm, LANE), lambda i: (i, 0)),
            pl.BlockSpec((1, Q_LORA), lambda i: (0, 0)),
            pl.BlockSpec((1, KV_LORA), lambda i: (0, 0)),
            pl.BlockSpec((Q_LORA, wq), lambda i: (0, 0)),
            pl.BlockSpec((KV_LORA, wq), lambda i: (0, 0)),
        ],
        out_specs=[
            pl.BlockSpec((tm, wq), lambda i: (i, 0)),
            pl.BlockSpec((tm, wq), lambda i: (i, 0)),
            pl.BlockSpec((tm, HD * DV_D), lambda i: (i, 0)),
        ],
        out_shape=[
            jax.ShapeDtypeStruct((n, wq), BF16),
            jax.ShapeDtypeStruct((n, wq), BF16),
            jax.ShapeDtypeStruct((n, HD * DV_D), BF16),
        ],
        compiler_params=_params("parallel"),
    )(rest, rest, tab, gq.reshape(1, -1), gkv.reshape(1, -1), wuq, wukv)


def _mla_attn_kernel(q_ref, k_ref, v_ref, o_ref, m_ref, acc_ref, *, tk, nk):
    ki = pl.program_id(2)

    @pl.when(ki == 0)
    def _():
        m_ref[...] = jnp.full(m_ref.shape, -jnp.inf, F32)
        acc_ref[...] = jnp.zeros(acc_ref.shape, F32)

    ones = jnp.ones((tk, LANE), BF16)
    for h in range(HD):
        q = q_ref[:, h * 2 * LANE:(h + 1) * 2 * LANE]
        k = k_ref[:, h * 2 * LANE:(h + 1) * 2 * LANE]
        vx = jnp.concatenate([v_ref[:, h * LANE:(h + 1) * LANE], ones], axis=1)
        s = lax.dot_general(q, k, NT_DIMS, preferred_element_type=F32)
        _online_softmax_step(s, vx, m_ref, acc_ref, h, tk)

    @pl.when(ki == nk - 1)
    def _():
        for h in range(HD):
            a = acc_ref[h]
            o_ref[:, h * LANE:(h + 1) * LANE] = (a[:, :LANE] / a[:, LANE:]).astype(o_ref.dtype)


def _mla_attn(q, k, v, n_seq, seq):
    n = q.shape[0]
    tq = min(1024, seq)
    tk = min(1024, seq)
    nq, nk = seq // tq, seq // tk
    wq = HD * 2 * LANE
    kern = functools.partial(_mla_attn_kernel, tk=tk, nk=nk)
    return pl.pallas_call(
        kern,
        grid=(n_seq, nq, nk),
        in_specs=[
            pl.BlockSpec((tq, wq), lambda b, qi, ki: (b * nq + qi, 0)),
            pl.BlockSpec((tk, wq), lambda b, qi, ki: (b * nk + ki, 0)),
            pl.BlockSpec((tk, HD * DV_D), lambda b, qi, ki: (b * nk + ki, 0)),
        ],
        out_specs=pl.BlockSpec((tq, HD * DV_D), lambda b, qi, ki: (b * nq + qi, 0)),
        out_shape=jax.ShapeDtypeStruct((n, HD * DV_D), BF16),
        scratch_shapes=[
            pltpu.VMEM((HD, tq, LANE), F32),
            pltpu.VMEM((HD, tq, 2 * LANE), F32),
        ],
        compiler_params=_params("parallel", "parallel", "arbitrary"),
    )(q, k, v)


def _sgu_kernel(uv_ref, g_ref, b_ref, ws_ref, bs_ref, o_ref, *, tm):
    z = jax.nn.gelu(uv_ref[...].astype(F32))
    u = z[:, :BRANCH_W]
    v = z[:, BRANCH_W:]
    mu = jnp.mean(v, axis=-1, keepdims=True)
    vc = v - mu
    var = jnp.mean(vc * vc, axis=-1, keepdims=True)
    vn = (vc * lax.rsqrt(var + EPS) * g_ref[...] + b_ref[...]).astype(BF16)
    for c in range(tm // SGU_CHUNK):
        r0 = c * SGU_CHUNK
        for g in range(SGU_GROUPS):
            c0 = g * LANE
            vm = jnp.dot(ws_ref[g], vn[r0:r0 + SGU_CHUNK, c0:c0 + LANE], preferred_element_type=F32)
            vm = vm + bs_ref[:, c0:c0 + LANE]
            o_ref[r0:r0 + SGU_CHUNK, c0:c0 + LANE] = (u[r0:r0 + SGU_CHUNK, c0:c0 + LANE] * vm).astype(o_ref.dtype)


def _sgu(rest, ln_g, ln_b, ws, bs, seq):
    n = rest.shape[0]
    tm = min(512, seq)
    kern = functools.partial(_sgu_kernel, tm=tm)
    return pl.pallas_call(
        kern,
        grid=(n // tm,),
        in_specs=[
            pl.BlockSpec((tm, 2 * BRANCH_W), lambda i: (i, R_UV // (2 * BRANCH_W))),
            pl.BlockSpec((1, BRANCH_W), lambda i: (0, 0)),
            pl.BlockSpec((1, BRANCH_W), lambda i: (0, 0)),
            pl.BlockSpec((SGU_GROUPS, SGU_CHUNK, SGU_CHUNK), lambda i: (0, 0, 0)),
            pl.BlockSpec((SGU_CHUNK, BRANCH_W), lambda i: (0, 0)),
        ],
        out_specs=pl.BlockSpec((tm, BRANCH_W), lambda i: (i, 0)),
        out_shape=jax.ShapeDtypeStruct((n, BRANCH_W), BF16),
        compiler_params=_params("parallel"),
    )(rest, ln_g.reshape(1, -1), ln_b.reshape(1, -1), ws, bs)


def _lru_kernel(xf_ref, xfp_ref, xfn_ref, xr_ref, xrp_ref, xrn_ref, cw_ref, cb_ref, wr_ref, br_ref, wi_ref,
                bi_ref, lam_ref, hf_ref, hr_ref, af_ref, uf_ref, ar_ref, ur_ref, hc_ref, *, tt, tps):
    j = pl.program_id(0) % tps
    rowi = lax.broadcasted_iota(I32, (tt, LRU_W), 0)

    def gates(x_ref, xp_ref, xn_ref, at_start, at_end, d, a_ref, u_ref):
        x = x_ref[...].astype(F32)
        prev = jnp.where(at_start, 0.0, xp_ref[...].astype(F32)[HALO - 1:HALO, :])
        nxt = jnp.where(at_end, 0.0, xn_ref[...].astype(F32)[0:2, :])
        xm1 = jnp.where(rowi == 0, prev, pltpu.roll(x, 1, 0))
        xp1 = jnp.where(rowi == tt - 1, nxt[0:1, :], pltpu.roll(x, tt - 1, 0))
        xp2 = jnp.where(rowi == tt - 2, nxt[0:1, :],
                        jnp.where(rowi == tt - 1, nxt[1:2, :], pltpu.roll(x, tt - 2, 0)))
        xc = xm1 * cw_ref[0:1, :] + x * cw_ref[1:2, :] + xp1 * cw_ref[2:3, :] + xp2 * cw_ref[3:4, :] + cb_ref[...]
        xcb = xc.astype(BF16)

        def blockdiag(w_ref):
            return jnp.concatenate(
                [jnp.dot(xcb[:, g * LRU_BW:(g + 1) * LRU_BW], w_ref[d, g], preferred_element_type=F32)
                 for g in range(LRU_BLOCKS)], axis=1)

        r = jax.nn.sigmoid(blockdiag(wr_ref) + br_ref[d:d + 1, :])
        ig = jax.nn.sigmoid(blockdiag(wi_ref) + bi_ref[d:d + 1, :])
        log_a = (-LRU_C * r) * jax.nn.softplus(-lam_ref[d:d + 1, :])
        a = jnp.exp(log_a)
        u = jnp.sqrt(jnp.tanh(-log_a) * (1.0 + a * a)) * (ig * xc)
        a_ref[...] = a
        u_ref[...] = u

    gates(xf_ref, xfp_ref, xfn_ref, j == 0, j == tps - 1, 0, af_ref, uf_ref)
    gates(xr_ref, xrp_ref, xrn_ref, j == tps - 1, j == 0, 1, ar_ref, ur_ref)

    @pl.when(j == 0)
    def _():
        hc_ref[...] = jnp.zeros(hc_ref.shape, F32)

    ng = tt // SUBLANE
    sub = lax.broadcasted_iota(I32, (SUBLANE, LRU_W), 0)

    def body(g, carry):
        hf_prev, hr_prev = carry
        r0 = pl.multiple_of(g * SUBLANE, SUBLANE)
        a = af_ref[pl.ds(r0, SUBLANE), :]
        u = uf_ref[pl.ds(r0, SUBLANE), :]
        for s in (1, 2, 4):
            a_s = jnp.where(sub >= s, pltpu.roll(a, s, 0), 1.0)
            u_s = jnp.where(sub >= s, pltpu.roll(u, s, 0), 0.0)
            u = a * u_s + u
            a = a * a_s
        h = a * hf_prev + u
        hf_ref[pl.ds(r0, SUBLANE), :] = h
        hf_new = jnp.broadcast_to(h[SUBLANE - 1:SUBLANE, :], (SUBLANE, LRU_W))

        r1 = pl.multiple_of((ng - 1 - g) * SUBLANE, SUBLANE)
        a = ar_ref[pl.ds(r1, SUBLANE), :]
        u = ur_ref[pl.ds(r1, SUBLANE), :]
        for s in (1, 2, 4):
            a_s = jnp.where(sub < SUBLANE - s, pltpu.roll(a, SUBLANE - s, 0), 1.0)
            u_s = jnp.where(sub < SUBLANE - s, pltpu.roll(u, SUBLANE - s, 0), 0.0)
            u = a * u_s + u
            a = a * a_s
        h = a * hr_prev + u
        hr_ref[pl.ds(r1, SUBLANE), :] = h
        hr_new = jnp.broadcast_to(h[0:1, :], (SUBLANE, LRU_W))
        return hf_new, hr_new

    hf_c, hr_c = lax.fori_loop(0, ng, body, (hc_ref[0], hc_ref[1]))
    hc_ref[0] = hf_c
    hc_ref[1] = hr_c


def _lru(rest, cw, cb, wr, br, wi, bi, lam, seq):
    n = rest.shape[0]
    tt = min(512, seq)
    tps = seq // tt
    nt = n // tt
    t8 = tt // HALO
    last8 = n // HALO - 1
    cblk = R_XC // LRU_W

    def rev(i):
        return (i // tps) * tps + (tps - 1 - i % tps)

    kern = functools.partial(_lru_kernel, tt=tt, tps=tps)
    full = lambda shape: pl.BlockSpec(shape, lambda i: (0,) * len(shape))
    return pl.pallas_call(
        kern,
        grid=(nt,),
        in_specs=[
            pl.BlockSpec((tt, LRU_W), lambda i: (i, cblk)),
            pl.BlockSpec((HALO, LRU_W), lambda i: (jnp.maximum(i * t8 - 1, 0), cblk)),
            pl.BlockSpec((HALO, LRU_W), lambda i: (jnp.minimum((i + 1) * t8, last8), cblk)),
            pl.BlockSpec((tt, LRU_W), lambda i: (rev(i), cblk)),
            pl.BlockSpec((HALO, LRU_W), lambda i: (jnp.maximum(rev(i) * t8 - 1, 0), cblk)),
            pl.BlockSpec((HALO, LRU_W), lambda i: (jnp.minimum((rev(i) + 1) * t8, last8), cblk)),
            full((4, LRU_W)),
            full((1, LRU_W)),
            full((2, LRU_BLOCKS, LRU_BW, LRU_BW)),
            full((2, LRU_W)),
            full((2, LRU_BLOCKS, LRU_BW, LRU_BW)),
            full((2, LRU_W)),
            full((2, LRU_W)),
        ],
        out_specs=[
            pl.BlockSpec((tt, LRU_W), lambda i: (i, 0)),
            pl.BlockSpec((tt, LRU_W), lambda i: (rev(i), 0)),
        ],
        out_shape=[jax.ShapeDtypeStruct((n, LRU_W), F32), jax.ShapeDtypeStruct((n, LRU_W), F32)],
        scratch_shapes=[pltpu.VMEM((tt, LRU_W), F32)] * 4 + [pltpu.VMEM((2, SUBLANE, LRU_W), F32)],
        compiler_params=_params("arbitrary"),
    )(rest, rest, rest, rest, rest, rest, cw, cb.reshape(1, -1), wr, br, wi, bi, lam)


def _merge_kernel(ya_ref, yb_ref, hf_ref, hr_ref, gate_ref, yd_ref, wbr_ref, g0_ref, g1_ref, g2_ref, g3_ref, o_ref):
    yc = (jax.nn.gelu(gate_ref[...].astype(F32)) * (hf_ref[...] + hr_ref[...])).astype(BF16)
    ys = (ya_ref[...], yb_ref[...], yc, yd_ref[...])
    gls = (g0_ref, g1_ref, g2_ref, g3_ref)
    acc = None
    for k in range(N_BRANCH):
        t = jax.nn.sigmoid(gls[k][...].astype(F32)) * jnp.dot(ys[k], wbr_ref[k], preferred_element_type=F32)
        acc = t if acc is None else acc + t
    o_ref[...] = acc.astype(o_ref.dtype)


def _merge(ya, yb, hf, hr, rest, yd, wbr, seq):
    n = ya.shape[0]
    d = wbr.shape[-1]
    tm = min(512, seq)
    tn = 512
    row = lambda w: pl.BlockSpec((tm, w), lambda i, j: (i, 0))

    def gl_spec(k):
        base = (R_GL + k * d) // tn
        return pl.BlockSpec((tm, tn), lambda i, j: (i, base + j))

    return pl.pallas_call(
        _merge_kernel,
        grid=(n // tm, d // tn),
        in_specs=[
            row(BRANCH_W), row(BRANCH_W), row(LRU_W), row(LRU_W),
            pl.BlockSpec((tm, LRU_W), lambda i, j: (i, R_GATE // LRU_W)),
            row(BRANCH_W),
            pl.BlockSpec((N_BRANCH, BRANCH_W, tn), lambda i, j: (0, 0, j)),
            gl_spec(0), gl_spec(1), gl_spec(2), gl_spec(3),
        ],
        out_specs=pl.BlockSpec((tm, tn), lambda i, j: (i, j)),
        out_shape=jax.ShapeDtypeStruct((n, d), BF16),
        compiler_params=_params("parallel", "parallel"),
    )(ya, yb, hf, hr, rest, yd, wbr, rest, rest, rest, rest)


def _outproj_kernel(m_ref, w_ref, x_ref, g_ref, o_ref):
    o_ref[...] = x_ref[...] + g_ref[...] * jnp.dot(m_ref[...], w_ref[...], preferred_element_type=F32)


def _outproj(merged, w, x, g1, seq):
    n, d = x.shape
    tm = min(512, seq)
    tps = seq // tm
    return pl.pallas_call(
        _outproj_kernel,
        grid=(n // tm,),
        in_specs=[
            pl.BlockSpec((tm, d), lambda i: (i, 0)),
            pl.BlockSpec((d, d), lambda i: (0, 0)),
            pl.BlockSpec((tm, d), lambda i: (i, 0)),
            pl.BlockSpec((None, 1, d), lambda i: (i // tps, 0, 0)),
        ],
        out_specs=pl.BlockSpec((tm, d), lambda i: (i, 0)),
        out_shape=jax.ShapeDtypeStruct((n, d), F32),
        compiler_params=_params("parallel"),
    )(merged, w, x, g1)


def _router_kernel(x_ref, g_ref, mod_ref, wh_ref, wl_ref, rb_ref, tri_ref, g2_ref, swg_ref, swu_ref, swd_ref,
                   x1_ref, h2g_ref, e_ref, w_ref, r_ref, cnt_ref, run_ref, *, tm, d):
    @pl.when(pl.program_id(0) == 0)
    def _():
        run_ref[...] = jnp.zeros(run_ref.shape, F32)

    x = x_ref[...]
    y = x * lax.rsqrt(jnp.mean(x * x, axis=-1, keepdims=True) + EPS) * g_ref[...]
    h = y * (1.0 + mod_ref[0:1, :]) + mod_ref[1:2, :]
    hb = h.astype(BF16)
    hs = (jax.nn.silu(jnp.dot(hb, swg_ref[...], preferred_element_type=F32))
          * jnp.dot(hb, swu_ref[...], preferred_element_type=F32)).astype(BF16)
    x1_ref[...] = x + g2_ref[...] * jnp.dot(hs, swd_ref[...], preferred_element_type=F32)
    slabs = d // LANE
    for s in range(slabs):
        h2g_ref[pl.ds(s, tm, stride=slabs), :] = h[:, s * LANE:(s + 1) * LANE]

    hl = (h - hb.astype(F32)).astype(BF16)
    logits = (lax.dot_general(wh_ref[...], hb, NT_DIMS, preferred_element_type=F32)
              + lax.dot_general(wl_ref[...], hb, NT_DIMS, preferred_element_type=F32)
              + lax.dot_general(wh_ref[...], hl, NT_DIMS, preferred_element_type=F32))
    sc = jax.nn.sigmoid(logits)
    sel = sc + rb_ref[...]

    neg = -jnp.inf
    i8 = lax.broadcasted_iota(I32, (GROUP_SZ, tm), 0).astype(F32)
    rows = []
    for g in range(N_GROUPS):
        blk = sel[g * GROUP_SZ:(g + 1) * GROUP_SZ, :]
        m1 = jnp.max(blk, axis=0, keepdims=True)
        i1 = jnp.min(jnp.where(blk == m1, i8, float(GROUP_SZ)), axis=0, keepdims=True)
        m2 = jnp.max(jnp.where(i8 == i1, neg, blk), axis=0, keepdims=True)
        rows.append(m1 + m2)
    gs = jnp.concatenate(rows, axis=0)
    gi8 = lax.broadcasted_iota(I32, (N_GROUPS, tm), 0).astype(F32)
    gsel = jnp.zeros((N_GROUPS, tm), F32)
    for _ in range(TOPK_GROUPS):
        gm = jnp.max(gs, axis=0, keepdims=True)
        gi = jnp.min(jnp.where(gs == gm, gi8, float(N_GROUPS)), axis=0, keepdims=True)
        hit = gi8 == gi
        gsel = jnp.where(hit, 1.0, gsel)
        gs = jnp.where(hit, neg, gs)
    emask = jnp.concatenate(
        [jnp.broadcast_to(gsel[g:g + 1, :], (GROUP_SZ, tm)) for g in range(N_GROUPS)], axis=0)
    selm = jnp.where(emask > 0.0, sel, neg)

    i64 = lax.broadcasted_iota(I32, (N_EXPERTS, tm), 0).astype(F32)
    chosen = jnp.zeros((N_EXPERTS, tm), F32)
    idxs, wts = [], []
    for _ in range(TOP_K):
        mx = jnp.max(selm, axis=0, keepdims=True)
        ix = jnp.min(jnp.where(selm == mx, i64, float(N_EXPERTS)), axis=0, keepdims=True)
        oh = i64 == ix
        wts.append(jnp.sum(jnp.where(oh, sc, 0.0), axis=0, keepdims=True))
        idxs.append(ix)
        selm = jnp.where(oh, neg, selm)
        chosen = jnp.where(oh, 1.0, chosen)
    wk = jnp.concatenate(wts, axis=0)
    w_ref[...] = wk / jnp.sum(wk, axis=0, keepdims=True) * ROUTE_SCALE
    e_ref[...] = jnp.concatenate(idxs, axis=0).astype(I32)

    before = jnp.dot(chosen.astype(BF16), tri_ref[...], preferred_element_type=F32)
    rank = run_ref[...] + before
    r_ref[...] = jnp.concatenate(
        [jnp.sum(jnp.where(i64 == idxs[k], rank, 0.0), axis=0, keepdims=True) for k in range(TOP_K)],
        axis=0).astype(I32)
    run_ref[...] = run_ref[...] + jnp.sum(chosen, axis=1, keepdims=True)
    cnt_ref[...] = jnp.broadcast_to(run_ref[...], cnt_ref.shape).astype(I32)


def _router(x, g, mod, wh, wl, rb, g2, swg, swu, swd, seq):
    n, d = x.shape
    ds_ = swg.shape[1]
    tm = min(512, seq)
    tps = seq // tm
    slabs = d // LANE
    tri = jnp.triu(jnp.ones((tm, tm), F32), 1).astype(BF16)
    kern = functools.partial(_router_kernel, tm=tm, d=d)
    return pl.pallas_call(
        kern,
        grid=(n // tm,),
        in_specs=[
            pl.BlockSpec((tm, d), lambda i: (i, 0)),
            pl.BlockSpec((1, d), lambda i: (0, 0)),
            pl.BlockSpec((None, 2, d), lambda i: (i // tps, 0, 0)),
            pl.BlockSpec((N_EXPERTS, d), lambda i: (0, 0)),
            pl.BlockSpec((N_EXPERTS, d), lambda i: (0, 0)),
            pl.BlockSpec((N_EXPERTS, 1), lambda i: (0, 0)),
            pl.BlockSpec((tm, tm), lambda i: (0, 0)),
            pl.BlockSpec((None, 1, d), lambda i: (i // tps, 0, 0)),
            pl.BlockSpec((d, ds_), lambda i: (0, 0)),
            pl.BlockSpec((d, ds_), lambda i: (0, 0)),
            pl.BlockSpec((ds_, d), lambda i: (0, 0)),
        ],
        out_specs=[
            pl.BlockSpec((tm, d), lambda i: (i, 0)),
            pl.BlockSpec((tm * slabs, LANE), lambda i: (i, 0)),
            pl.BlockSpec((TOP_K, tm), lambda i: (0, i)),
            pl.BlockSpec((TOP_K, tm), lambda i: (0, i)),
            pl.BlockSpec((TOP_K, tm), lambda i: (0, i)),
            pl.BlockSpec((N_EXPERTS, LANE), lambda i: (0, 0)),
        ],
        out_shape=[
            jax.ShapeDtypeStruct((n, d), F32),
            jax.ShapeDtypeStruct((n * slabs, LANE), F32),
            jax.ShapeDtypeStruct((TOP_K, n), I32),
            jax.ShapeDtypeStruct((TOP_K, n), F32),
            jax.ShapeDtypeStruct((TOP_K, n), I32),
            jax.ShapeDtypeStruct((N_EXPERTS, LANE), I32),
        ],
        scratch_shapes=[pltpu.VMEM((N_EXPERTS, 1), F32)],
        compiler_params=_params("arbitrary"),
    )(x, g.reshape(1, d), mod, wh, wl, rb.reshape(N_EXPERTS, 1), tri, g2, swg, swu, swd)


def _dispatch_kernel(dest_ref, pend_ref, pc_ref, h_ref, xs_ref, zero_ref, sem, zsem, *, tm, n, slabs):
    i = pl.program_id(0)
    rows_blk = MOE_BLK * slabs

    def tail_copy(e):
        start = pl.multiple_of((pend_ref[e] - MOE_BLK) * slabs, rows_blk)
        return pltpu.make_async_copy(zero_ref, xs_ref.at[pl.ds(start, rows_blk), :], zsem)

    @pl.when(i == 0)
    def _():
        zero_ref[...] = jnp.zeros(zero_ref.shape, F32)

        def zstart(e, c):
            @pl.when(pc_ref[e] > 0)
            def _():
                tail_copy(e).start()
            return c

        def zwait(e, c):
            @pl.when(pc_ref[e] > 0)
            def _():
                tail_copy(e).wait()
            return c

        lax.fori_loop(0, N_EXPERTS, zstart, 0)
        lax.fori_loop(0, N_EXPERTS, zwait, 0)

    def row_copy(r, k):
        src = h_ref.at[pl.ds(pl.multiple_of(r * slabs, slabs), slabs), :]
        dst_row = dest_ref[k * n + i * tm + r]
        dst = xs_ref.at[pl.ds(pl.multiple_of(dst_row * slabs, slabs), slabs), :]
        return pltpu.make_async_copy(src, dst, sem)

    def start(r, c):
        for k in range(TOP_K):
            row_copy(r, k).start(priority=k % 2)
        return c

    lax.fori_loop(0, tm, start, 0)
    for _ in range(TOP_K):
        pltpu.make_async_copy(h_ref, xs_ref.at[pl.ds(0, tm * slabs), :], sem).wait()


def _dispatch(dest, pend, pc, h2g, n_rows, seq):
    n = dest.shape[0] // TOP_K
    slabs = h2g.shape[0] // n
    tm = min(256, seq)
    kern = functools.partial(_dispatch_kernel, tm=tm, n=n, slabs=slabs)
    return pl.pallas_call(
        kern,
        grid_spec=pltpu.PrefetchScalarGridSpec(
            num_scalar_prefetch=3,
            grid=(n // tm,),
            in_specs=[pl.BlockSpec((tm * slabs, LANE), lambda i, *_: (i, 0))],
            out_specs=pl.BlockSpec(memory_space=pl.ANY),
            scratch_shapes=[
                pltpu.VMEM((MOE_BLK * slabs, LANE), F32),
                pltpu.SemaphoreType.DMA(()),
                pltpu.SemaphoreType.DMA(()),
            ],
        ),
        out_shape=jax.ShapeDtypeStruct((n_rows * slabs, LANE), F32),
        compiler_params=_params("arbitrary"),
    )(dest, pend, pc, h2g)


def _gmm_kernel(be_ref, nu_ref, x_ref, wg_ref, wu_ref, wd_ref, o_ref, *, slabs):
    @pl.when(pl.program_id(0) < nu_ref[0])
    def _():
        x = jnp.concatenate(
            [x_ref[pl.ds(s, MOE_BLK, stride=slabs), :].astype(BF16) for s in range(slabs)], axis=1)
        hg = jnp.dot(x, wg_ref[...], preferred_element_type=F32)
        hu = jnp.dot(x, wu_ref[...], preferred_element_type=F32)
        hb = (jax.nn.silu(hg) * hu).astype(BF16)
        per = 4
        for c in range(slabs // per):
            y = jnp.dot(hb, wd_ref[:, c * per * LANE:(c + 1) * per * LANE], preferred_element_type=F32)
            for j in range(per):
                o_ref[pl.ds(c * per + j, MOE_BLK, stride=slabs), :] = y[:, j * LANE:(j + 1) * LANE]


def _gmm(blk_e, nused, xs, wg, wu, wd):
    _, d, de = wg.shape
    slabs = d // LANE
    nblk = xs.shape[0] // (MOE_BLK * slabs)
    kern = functools.partial(_gmm_kernel, slabs=slabs)

    def blk(b, be, nu):
        return jnp.minimum(b, nu[0] - 1)

    return pl.pallas_call(
        kern,
        grid_spec=pltpu.PrefetchScalarGridSpec(
            num_scalar_prefetch=2,
            grid=(nblk,),
            in_specs=[
                pl.BlockSpec((MOE_BLK * slabs, LANE), lambda b, be, nu: (blk(b, be, nu), 0)),
                pl.BlockSpec((None, d, de), lambda b, be, nu: (be[blk(b, be, nu)], 0, 0)),
                pl.BlockSpec((None, d, de), lambda b, be, nu: (be[blk(b, be, nu)], 0, 0)),
                pl.BlockSpec((None, de, d), lambda b, be, nu: (be[blk(b, be, nu)], 0, 0)),
            ],
            out_specs=pl.BlockSpec((MOE_BLK * slabs, LANE), lambda b, be, nu: (blk(b, be, nu), 0)),
        ),
        out_shape=jax.ShapeDtypeStruct(xs.shape, F32),
        compiler_params=_params("arbitrary"),
    )(blk_e, nused, xs, wg, wu, wd)


def _combine_kernel(dest_ref, wt_ref, x_ref, g2_ref, fg_ref, ys_ref, o_ref, buf_ref, wb_ref, sem,
                    *, tm, n, slabs, nt, final):
    i = pl.program_id(0)
    slot = i % 2
    rows = tm * COMBINE_PITCH

    def gather(tile, sl):
        def start(r, c):
            for k in range(TOP_K):
                src_row = dest_ref[k * n + tile * tm + r]
                src = ys_ref.at[pl.ds(pl.multiple_of(src_row * slabs, slabs), slabs), :]
                dst = buf_ref.at[sl, pl.ds(pl.multiple_of(k * rows + r * COMBINE_PITCH, SUBLANE), slabs), :]
                pltpu.make_async_copy(src, dst, sem.at[sl]).start(priority=k % 2)
            return c

        lax.fori_loop(0, tm, start, 0)

    @pl.when(i == 0)
    def _():
        gather(0, 0)

    @pl.when(i + 1 < nt)
    def _():
        gather(i + 1, 1 - slot)

    moved = TOP_K * tm * slabs
    pltpu.make_async_copy(ys_ref.at[pl.ds(0, moved), :], buf_ref.at[slot, pl.ds(0, moved), :], sem.at[slot]).wait()

    wt = wt_ref[...]
    for k in range(TOP_K):
        wb_ref[k] = jnp.broadcast_to(wt[:, k:k + 1], (tm, LANE))
    for s in range(slabs):
        c0 = s * LANE
        routed = None
        for k in range(TOP_K):
            t = wb_ref[k] * buf_ref[slot, pl.ds(k * rows + s, tm, stride=COMBINE_PITCH), :]
            routed = t if routed is None else routed + t
        o_ref[:, c0:c0 + LANE] = x_ref[:, c0:c0 + LANE] + g2_ref[:, c0:c0 + LANE] * routed
    if final:
        o = o_ref[...]
        o_ref[...] = o * lax.rsqrt(jnp.mean(o * o, axis=-1, keepdims=True) + EPS) * fg_ref[...]


def _combine(dest, wt, x, g2, ys, final_g, final, seq):
    n, d = x.shape
    slabs = d // LANE
    tm = min(128, seq)
    tps = seq // tm
    assert COMBINE_PITCH >= slabs
    kern = functools.partial(_combine_kernel, tm=tm, n=n, slabs=slabs, nt=n // tm, final=final)
    return pl.pallas_call(
        kern,
        grid_spec=pltpu.PrefetchScalarGridSpec(
            num_scalar_prefetch=1,
            grid=(n // tm,),
            in_specs=[
                pl.BlockSpec((tm, TOP_K), lambda i, *_: (i, 0)),
                pl.BlockSpec((tm, d), lambda i, *_: (i, 0)),
                pl.BlockSpec((None, 1, d), lambda i, *_: (i // tps, 0, 0)),
                pl.BlockSpec((1, d), lambda i, *_: (0, 0)),
                pl.BlockSpec(memory_space=pl.ANY),
            ],
            out_specs=pl.BlockSpec((tm, d), lambda i, *_: (i, 0)),
            scratch_shapes=[
                pltpu.VMEM((2, TOP_K * tm * COMBINE_PITCH, LANE), F32),
                pltpu.VMEM((TOP_K, tm, LANE), F32),
                pltpu.SemaphoreType.DMA((2,)),
            ],
        ),
        out_shape=jax.ShapeDtypeStruct((n, d), F32),
        compiler_params=_params("arbitrary"),
    )(dest, wt, x, g2, final_g.reshape(1, d), ys)


def _swap_halves(w):
    half = w.shape[-1] // 2
    return jnp.concatenate([w[..., half:], w[..., :half]], axis=-1)


def _prep_layer(l, p):
    d = p['w_in'].shape[1]
    sizes = [HA * 2 * DHA, HA * 2 * DHA, HA * DVA, 2 * BRANCH_W, 2 * LRU_W, Q_LORA, KV_LORA, D_ROPE, N_BRANCH * d]
    offs = np.cumsum([0] + sizes)
    w = p['w_in'][l]
    qa, ka, va, uv, xg, cq, ckv, kr, gl = [w[:, offs[i]:offs[i + 1]] for i in range(len(sizes))]
    pad = jnp.zeros((d, 512 - KV_LORA - 2 * D_ROPE), w.dtype)
    w_att = jnp.concatenate([qa * (DHA ** -0.5 * LOG2E), ka, va], axis=1).astype(BF16)
    w_rest = jnp.concatenate([uv, xg, cq, ckv, kr, _swap_halves(kr), pad, gl], axis=1).astype(BF16)

    wuq = p['w_uq'][l].reshape(Q_LORA, HD, D_NOPE + D_ROPE)
    wuq_r = wuq[:, :, D_NOPE:]
    wuq_ext = jnp.concatenate([wuq[:, :, :D_NOPE], wuq_r, _swap_halves(wuq_r)], axis=-1)
    wuq_ext = wuq_ext.reshape(Q_LORA, HD * 2 * LANE).astype(BF16)

    lam_init = 0.8 - 0.6 * math.exp(-0.3 * l)
    lam = (jnp.exp(jnp.sum(p['lam_q1'][l] * p['lam_k1'][l])) - jnp.exp(jnp.sum(p['lam_q2'][l] * p['lam_k2'][l]))
           + lam_init)
    slopes = jnp.asarray(2.0 ** (-8.0 * np.arange(1, HA + 1) / HA) * LOG2E, F32)
    par = jnp.concatenate([slopes, lam.reshape(1), jnp.full((1,), 1.0 - lam_init, F32), jnp.zeros((2,), F32)])

    rw = p['router_w'][l].T
    rw_hi = rw.astype(BF16)
    rw_lo = (rw - rw_hi.astype(F32)).astype(BF16)
    bs = jnp.repeat(p['sgu_b'][l].T, LANE, axis=1)
    return dict(
        w_att=w_att, w_rest=w_rest, wuq=wuq_ext, wukv=p['w_ukv'][l].astype(BF16), par=par,
        sgu_w=p['sgu_w'][l].astype(BF16), sgu_bs=bs,
        rg_wr=p['rg_wr'][l].astype(BF16), rg_wi=p['rg_wi'][l].astype(BF16),
        w_branch=p['w_branch'][l].astype(BF16), w_out=p['w_out'][l].astype(BF16),
        rw_hi=rw_hi, rw_lo=rw_lo,
        exp_wg=p['exp_wg'][l].astype(BF16), exp_wu=p['exp_wu'][l].astype(BF16), exp_wd=p['exp_wd'][l].astype(BF16),
        sh_wg=p['sh_wg'][l].astype(BF16), sh_wu=p['sh_wu'][l].astype(BF16), sh_wd=p['sh_wd'][l].astype(BF16),
    )


def _rope_table(n_seq, seq):
    inv_freq = ROPE_THETA ** (-jnp.arange(0, D_ROPE, 2, dtype=F32) / D_ROPE)
    ang = jnp.arange(seq, dtype=F32)[:, None] * inv_freq[None, :]
    cos, sin = jnp.cos(ang), jnp.sin(ang)
    tab = jnp.concatenate([cos, cos, -sin, sin], axis=1)
    return jnp.tile(tab, (n_seq, 1))


def _trunk(x3, mods, p, preps):
    n_seq, seq, d = x3.shape
    n = n_seq * seq
    x = x3.reshape(n, d)
    tab = _rope_table(n_seq, seq)
    depth = len(preps)
    for l in range(depth):
        w = preps[l]
        mod = mods[l]
        mod1 = jnp.stack([mod[:, 1], mod[:, 0]], axis=1)
        mod2 = jnp.stack([mod[:, 4], mod[:, 3]], axis=1)
        g1 = mod[:, 2:3]
        g2 = mod[:, 5:6]

        att = _inproj(x, p['norm1_g'][l], mod1, w['w_att'], BF16, seq)
        rest = _inproj(x, p['norm1_g'][l], mod1, w['w_rest'], BF16, seq)
        ya = _diff_attn(att, w['par'], p['subln_g'][l], n_seq, seq)
        yb = _sgu(rest, p['sgu_ln_g'][l], p['sgu_ln_b'][l], w['sgu_w'], w['sgu_bs'], seq)
        hf, hr = _lru(rest, p['conv_w'][l], p['conv_b'][l], w['rg_wr'], p['rg_br'][l], w['rg_wi'],
                      p['rg_bi'][l], p['rg_lam'][l], seq)
        q, k, v = _mla_prep(rest, tab, p['q_norm_g'][l], p['kv_norm_g'][l], w['wuq'], w['wukv'], seq)
        yd = _mla_attn(q, k, v, n_seq, seq)
        merged = _merge(ya, yb, hf, hr, rest, yd, w['w_branch'], seq)
        x = _outproj(merged, w['w_out'], x, g1, seq)

        x, h2g, eidx, wgt, rank, cnt = _router(x, p['norm2_g'][l], mod2, w['rw_hi'], w['rw_lo'],
                                               p['router_bias'][l], g2, w['sh_wg'], w['sh_wu'], w['sh_wd'], seq)
        counts = cnt[:, 0]
        pc = ((counts + MOE_BLK - 1) // MOE_BLK) * MOE_BLK
        pend = jnp.cumsum(pc).astype(I32)
        pstart = pend - pc
        eid = jnp.arange(N_EXPERTS, dtype=I32)
        dest = (jnp.sum(jnp.where(eidx[..., None] == eid, pstart, 0), axis=-1) + rank).reshape(-1).astype(I32)
        nblk = n * TOP_K // MOE_BLK + N_EXPERTS
        blk_row = jnp.arange(nblk, dtype=I32)[:, None] * MOE_BLK
        blk_e = jnp.minimum(jnp.sum((pend[None, :] <= blk_row).astype(I32), axis=1), N_EXPERTS - 1)
        nused = (pend[-1:] // MOE_BLK).astype(I32)
        xs = _dispatch(dest, pend, pc.astype(I32), h2g, nblk * MOE_BLK, seq)
        ys = _gmm(blk_e, nused, xs, w['exp_wg'], w['exp_wu'], w['exp_wd'])
        x = _combine(dest, wgt.T, x, g2, ys, p['final_g'], l == depth - 1, seq)
    return x.reshape(n_seq, seq, d)


def kernel(x_prompt, x_sample, c_prompt, c_sample, ada_w, ada_b, norm1_g, norm2_g, w_in, lam_q1, lam_k1, lam_q2, lam_k2, subln_g, sgu_ln_g, sgu_ln_b, sgu_w, sgu_b, conv_w, conv_b, rg_wr, rg_br, rg_wi, rg_bi, rg_lam, q_norm_g, kv_norm_g, w_uq, w_ukv, w_branch, w_out, router_w, router_bias, exp_wg, exp_wu, exp_wd, sh_wg, sh_wu, sh_wd, final_g):
    p = dict(ada_w=ada_w, ada_b=ada_b, norm1_g=norm1_g, norm2_g=norm2_g, w_in=w_in,
             lam_q1=lam_q1, lam_k1=lam_k1, lam_q2=lam_q2, lam_k2=lam_k2, subln_g=subln_g,
             sgu_ln_g=sgu_ln_g, sgu_ln_b=sgu_ln_b, sgu_w=sgu_w, sgu_b=sgu_b,
             conv_w=conv_w, conv_b=conv_b, rg_wr=rg_wr, rg_br=rg_br, rg_wi=rg_wi, rg_bi=rg_bi,
             rg_lam=rg_lam, q_norm_g=q_norm_g, kv_norm_g=kv_norm_g, w_uq=w_uq, w_ukv=w_ukv,
             w_branch=w_branch, w_out=w_out, router_w=router_w, router_bias=router_bias,
             exp_wg=exp_wg, exp_wu=exp_wu, exp_wd=exp_wd, sh_wg=sh_wg, sh_wu=sh_wu, sh_wd=sh_wd,
             final_g=final_g)
    depth, d, _ = ada_w.shape
    bp, bs = x_prompt.shape[0], x_sample.shape[0]
    assert bp + bs <= 8
    c8 = jnp.concatenate([c_prompt, c_sample, jnp.zeros((8 - bp - bs, d), F32)], axis=0)
    mod = _ada_mod(c8, ada_w, ada_b).reshape(depth, 8, 6, d)
    preps = [_prep_layer(l, p) for l in range(depth)]
    y_prompt = _trunk(x_prompt, mod[:, :bp], p, preps)
    y_sample = _trunk(x_sample, mod[:, bp:bp + bs], p, preps)
    return (y_prompt, y_sample)
```

```python
import functools
import math

import numpy as np
import jax
import jax.numpy as jnp
from jax import lax
from jax.experimental import pallas as pl
from jax.experimental.pallas import tpu as pltpu

F32 = jnp.float32
BF16 = jnp.bfloat16
I32 = jnp.int32

EPS = 1e-6
LOG2E = math.log2(math.e)
LANE = 128
SUBLANE = 8
VMEM_LIMIT = 48 * 1024 * 1024

HA = 4
DHA = 64
DVA = 2 * DHA
BRANCH_W = 512
SGU_CHUNK = 128
SGU_GROUPS = 4
LRU_W = 512
LRU_BLOCKS = 4
LRU_BW = LRU_W // LRU_BLOCKS
LRU_C = 8.0
HALO = 16
HD = 4
Q_LORA = 512
KV_LORA = 256
D_NOPE = 128
D_ROPE = 64
DV_D = 128
ROPE_THETA = 10000.0
N_EXPERTS = 64
TOP_K = 8
N_GROUPS = 8
TOPK_GROUPS = 4
GROUP_SZ = N_EXPERTS // N_GROUPS
ROUTE_SCALE = 2.5
MOE_BLK = 256
N_BRANCH = 4
COMBINE_PITCH = 24

R_UV = 0
R_XC = 1024
R_GATE = 1536
R_CQ = 2048
R_CKV = 2560
R_GL = 3072

NT_DIMS = (((1,), (1,)), ((), ()))


def _params(*sem):
    return pltpu.CompilerParams(dimension_semantics=sem, vmem_limit_bytes=VMEM_LIMIT)


def _ada_kernel(c_ref, w_ref, b_ref, o_ref):
    a = jax.nn.silu(c_ref[...]).astype(BF16)
    o_ref[...] = jnp.dot(a, w_ref[...].astype(BF16), preferred_element_type=F32) + b_ref[...]


def _ada_mod(c8, ada_w, ada_b):
    depth, d, n = ada_w.shape
    tn = 1024
    return pl.pallas_call(
        _ada_kernel,
        grid=(depth, n // tn),
        in_specs=[
            pl.BlockSpec((8, d), lambda l, j: (0, 0)),
            pl.BlockSpec((None, d, tn), lambda l, j: (l, 0, j)),
            pl.BlockSpec((None, 1, tn), lambda l, j: (l, 0, j)),
        ],
        out_specs=pl.BlockSpec((None, 8, tn), lambda l, j: (l, 0, j)),
        out_shape=jax.ShapeDtypeStruct((depth, 8, n), F32),
        compiler_params=_params("parallel", "parallel"),
    )(c8, ada_w, ada_b.reshape(depth, 1, n))


def _inproj_kernel(x_ref, g_ref, mod_ref, w_ref, o_ref, h_ref):
    @pl.when(pl.program_id(1) == 0)
    def _():
        x = x_ref[...]
        y = x * lax.rsqrt(jnp.mean(x * x, axis=-1, keepdims=True) + EPS) * g_ref[...]
        h_ref[...] = (y * (1.0 + mod_ref[0:1, :]) + mod_ref[1:2, :]).astype(BF16)

    o_ref[...] = jnp.dot(h_ref[...], w_ref[...], preferred_element_type=F32).astype(o_ref.dtype)


def _inproj(x, g, mod, w, layer, out_dtype, seq):
    n, d = x.shape
    nc = w.shape[2]
    tm = min(1024, seq)
    tn = 1024 if nc % 1024 == 0 else 512
    tps = seq // tm
    return pl.pallas_call(
        _inproj_kernel,
        grid=(n // tm, nc // tn),
        in_specs=[
            pl.BlockSpec((tm, d), lambda i, j: (i, 0)),
            pl.BlockSpec((1, d), lambda i, j: (0, 0)),
            pl.BlockSpec((None, 2, d), lambda i, j: (i // tps, 0, 0)),
            pl.BlockSpec((None, d, tn), lambda i, j: (layer, 0, j)),
        ],
        out_specs=pl.BlockSpec((tm, tn), lambda i, j: (i, j)),
        out_shape=jax.ShapeDtypeStruct((n, nc), out_dtype),
        scratch_shapes=[pltpu.VMEM((tm, d), BF16)],
        compiler_params=_params("parallel", "arbitrary"),
    )(x, g.reshape(1, d), mod, w)


def _online_softmax_step(s, vx, m_ref, acc_ref, idx, tk, row_term=None):
    m_prev = m_ref[idx]
    row_max = jnp.max(s, axis=-1, keepdims=True)
    if row_term is None:
        m_new = jnp.maximum(m_prev, row_max)
        shift = m_new
    else:
        m_new = jnp.maximum(m_prev, row_max + row_term)
        shift = m_new - row_term
    alpha = jnp.exp2(m_prev - m_new)
    p = jnp.exp2(s - jnp.concatenate([shift] * (tk // LANE), axis=1))
    pv = jnp.dot(p.astype(BF16), vx, preferred_element_type=F32)
    acc_ref[idx] = jnp.concatenate([alpha, alpha], axis=1) * acc_ref[idx] + pv
    m_ref[idx] = m_new


def _diff_attn_kernel(par_ref, q_ref, k_ref, v_ref, g_ref, o_ref, m_ref, acc_ref, *, tq, tk, nk):
    qi = pl.program_id(1)
    ki = pl.program_id(2)

    @pl.when(ki == 0)
    def _():
        m_ref[...] = jnp.full(m_ref.shape, -jnp.inf, F32)
        acc_ref[...] = jnp.zeros(acc_ref.shape, F32)

    lane = lax.broadcasted_iota(I32, (tq, LANE), 1)
    ones = jnp.ones((tk, LANE), BF16)
    offset = qi * tq - ki * tk
    keys_before = (ki + 1) * tk <= qi * tq
    keys_after = ki * tk >= (qi + 1) * tq

    def heads(bias_of):
        for h in range(HA):
            q = q_ref[:, h * LANE:(h + 1) * LANE]
            k = k_ref[:, h * LANE:(h + 1) * LANE]
            vx = jnp.concatenate([v_ref[:, h * LANE:(h + 1) * LANE], ones], axis=1)
            tile_bias, row_term = bias_of(h)
            zero = jnp.zeros_like(q)
            for mi, qm in enumerate((jnp.where(lane < DHA, q, zero), jnp.where(lane < DHA, zero, q))):
                s = lax.dot_general(qm, k, NT_DIMS, preferred_element_type=F32) - tile_bias
                _online_softmax_step(s, vx, m_ref, acc_ref, 2 * h + mi, tk, row_term)

    @pl.when(jnp.logical_or(keys_before, keys_after))
    def _():
        sign = jnp.where(keys_before, 1.0, -1.0)
        col = lax.broadcasted_iota(I32, (1, tk), 1).astype(F32)
        row = (lax.broadcasted_iota(I32, (tq, LANE), 0) + offset).astype(F32)

        def bias_of(h):
            slope = par_ref[h]
            return (col * slope) * (-sign), (row * slope) * (-sign)

        heads(bias_of)

    @pl.when(jnp.logical_not(jnp.logical_or(keys_before, keys_after)))
    def _():
        row = lax.broadcasted_iota(I32, (tq, tk), 0)
        col = lax.broadcasted_iota(I32, (tq, tk), 1)
        dist = jnp.abs(row - col + offset).astype(F32)
        heads(lambda h: (dist * par_ref[h], None))

    @pl.when(ki == nk - 1)
    def _():
        lam = par_ref[HA]
        for h in range(HA):
            a1 = acc_ref[2 * h]
            a2 = acc_ref[2 * h + 1]
            o = a1[:, :LANE] / a1[:, LANE:] - lam * (a2[:, :LANE] / a2[:, LANE:])
            y = o * lax.rsqrt(jnp.mean(o * o, axis=-1, keepdims=True) + EPS) * g_ref[...]
            o_ref[:, h * LANE:(h + 1) * LANE] = (y * par_ref[HA + 1]).astype(o_ref.dtype)


def _diff_attn(att, par, subln_g, n_seq, seq):
    n = att.shape[0]
    tq = min(1024, seq)
    tk = min(512, seq)
    nq, nk = seq // tq, seq // tk
    w = HA * LANE
    kern = functools.partial(_diff_attn_kernel, tq=tq, tk=tk, nk=nk)
    return pl.pallas_call(
        kern,
        grid=(n_seq, nq, nk),
        in_specs=[
            pl.BlockSpec(memory_space=pltpu.SMEM),
            pl.BlockSpec((tq, w), lambda b, qi, ki: (b * nq + qi, 0)),
            pl.BlockSpec((tk, w), lambda b, qi, ki: (b * nk + ki, 1)),
            pl.BlockSpec((tk, w), lambda b, qi, ki: (b * nk + ki, 2)),
            pl.BlockSpec((1, DVA), lambda b, qi, ki: (0, 0)),
        ],
        out_specs=pl.BlockSpec((tq, w), lambda b, qi, ki: (b * nq + qi, 0)),
        out_shape=jax.ShapeDtypeStruct((n, HA * DVA), BF16),
        scratch_shapes=[
            pltpu.VMEM((2 * HA, tq, LANE), F32),
            pltpu.VMEM((2 * HA, tq, 2 * LANE), F32),
        ],
        compiler_params=_params("parallel", "parallel", "arbitrary"),
    )(par, att, att, att, subln_g.reshape(1, DVA))


def _mla_prep_kernel(cq_ref, ckv_ref, tab_ref, gq_ref, gkv_ref, wuq_ref, wukv_ref, q_ref, k_ref, v_ref, *, scale):
    cq = cq_ref[...].astype(F32)
    cqn = (cq * lax.rsqrt(jnp.mean(cq * cq, axis=-1, keepdims=True) + EPS) * gq_ref[...]).astype(BF16)
    c = ckv_ref[...].astype(F32)
    ckv = c[:, :KV_LORA]
    ckvn = (ckv * lax.rsqrt(jnp.mean(ckv * ckv, axis=-1, keepdims=True) + EPS) * gkv_ref[...]).astype(BF16)
    tab = tab_ref[...]
    lane = lax.broadcasted_iota(I32, tab.shape, 1)

    def rope(pair):
        pr = pair * tab
        return jnp.where(lane < D_ROPE, pr + pltpu.roll(pr, D_ROPE, 1), 0.0)

    kr = rope(c[:, KV_LORA:KV_LORA + LANE]).astype(BF16)
    qf = jnp.dot(cqn, wuq_ref[...], preferred_element_type=F32)
    kvf = jnp.dot(ckvn, wukv_ref[...], preferred_element_type=F32)
    for h in range(HD):
        b0 = h * 2 * LANE
        q_ref[:, b0:b0 + LANE] = (qf[:, b0:b0 + LANE] * scale).astype(BF16)
        q_ref[:, b0 + LANE:b0 + 2 * LANE] = (rope(qf[:, b0 + LANE:b0 + 2 * LANE]) * scale).astype(BF16)
        k_ref[:, b0:b0 + LANE] = kvf[:, b0:b0 + LANE].astype(BF16)
        k_ref[:, b0 + LANE:b0 + 2 * LANE] = kr
        v_ref[:, h * LANE:(h + 1) * LANE] = kvf[:, b0 + LANE:b0 + 2 * LANE].astype(BF16)


def _mla_prep(rest, tab, gq, gkv, wuq, wukv, seq):
    n = rest.shape[0]
    tm = min(512, seq)
    scale = (D_NOPE + D_ROPE) ** -0.5 * LOG2E
    kern = functools.partial(_mla_prep_kernel, scale=scale)
    wq = HD * 2 * LANE
    return pl.pallas_call(
        kern,
        grid=(n // tm,),
        in_specs=[
            pl.BlockSpec((tm, Q_LORA), lambda i: (i, R_CQ // Q_LORA)),
            pl.BlockSpec((tm, 512), lambda i: (i, R_CKV // 512)),
            pl.BlockSpec((tm, LANE), lambda i: (i, 0)),
            pl.BlockSpec((1, Q_LORA), lambda i: (0, 0)),
            pl.BlockSpec((1, KV_LORA), lambda i: (0, 0)),
            pl.BlockSpec((Q_LORA, wq), lambda i: (0, 0)),
            pl.BlockSpec((KV_LORA, wq), lambda i: (0, 0)),
        ],
        out_specs=[
            pl.BlockSpec((tm, wq), lambda i: (i, 0)),
            pl.BlockSpec((tm, wq), lambda i: (i, 0)),
            pl.BlockSpec((tm, HD * DV_D), lambda i: (i, 0)),
        ],
        out_shape=[
            jax.ShapeDtypeStruct((n, wq), BF16),
            jax.ShapeDtypeStruct((n, wq), BF16),
            jax.ShapeDtypeStruct((n, HD * DV_D), BF16),
        ],
        compiler_params=_params("parallel"),
    )(rest, rest, tab, gq.reshape(1, -1), gkv.reshape(1, -1), wuq, wukv)


def _mla_attn_kernel(q_ref, k_ref, v_ref, o_ref, m_ref, acc_ref, *, tk, nk):
    ki = pl.program_id(2)

    @pl.when(ki == 0)
    def _():
        m_ref[...] = jnp.full(m_ref.shape, -jnp.inf, F32)
        acc_ref[...] = jnp.zeros(acc_ref.shape, F32)

    ones = jnp.ones((tk, LANE), BF16)
    for h in range(HD):
        q = q_ref[:, h * 2 * LANE:(h + 1) * 2 * LANE]
        k = k_ref[:, h * 2 * LANE:(h + 1) * 2 * LANE]
        vx = jnp.concatenate([v_ref[:, h * LANE:(h + 1) * LANE], ones], axis=1)
        s = lax.dot_general(q, k, NT_DIMS, preferred_element_type=F32)
        _online_softmax_step(s, vx, m_ref, acc_ref, h, tk)

    @pl.when(ki == nk - 1)
    def _():
        for h in range(HD):
            a = acc_ref[h]
            o_ref[:, h * LANE:(h + 1) * LANE] = (a[:, :LANE] / a[:, LANE:]).astype(o_ref.dtype)


def _mla_attn(q, k, v, n_seq, seq):
    n = q.shape[0]
    tq = min(1024, seq)
    tk = min(1024, seq)
    nq, nk = seq // tq, seq // tk
    wq = HD * 2 * LANE
    kern = functools.partial(_mla_attn_kernel, tk=tk, nk=nk)
    return pl.pallas_call(
        kern,
        grid=(n_seq, nq, nk),
        in_specs=[
            pl.BlockSpec((tq, wq), lambda b, qi, ki: (b * nq + qi, 0)),
            pl.BlockSpec((tk, wq), lambda b, qi, ki: (b * nk + ki, 0)),
            pl.BlockSpec((tk, HD * DV_D), lambda b, qi, ki: (b * nk + ki, 0)),
        ],
        out_specs=pl.BlockSpec((tq, HD * DV_D), lambda b, qi, ki: (b * nq + qi, 0)),
        out_shape=jax.ShapeDtypeStruct((n, HD * DV_D), BF16),
        scratch_shapes=[
            pltpu.VMEM((HD, tq, LANE), F32),
            pltpu.VMEM((HD, tq, 2 * LANE), F32),
        ],
        compiler_params=_params("parallel", "parallel", "arbitrary"),
    )(q, k, v)


def _sgu_kernel(uv_ref, g_ref, b_ref, ws_ref, bs_ref, o_ref, *, tm):
    z = jax.nn.gelu(uv_ref[...].astype(F32))
    u = z[:, :BRANCH_W]
    v = z[:, BRANCH_W:]
    mu = jnp.mean(v, axis=-1, keepdims=True)
    vc = v - mu
    var = jnp.mean(vc * vc, axis=-1, keepdims=True)
    vn = (vc * lax.rsqrt(var + EPS) * g_ref[...] + b_ref[...]).astype(BF16)
    for c in range(tm // SGU_CHUNK):
        r0 = c * SGU_CHUNK
        for g in range(SGU_GROUPS):
            c0 = g * LANE
            vm = jnp.dot(ws_ref[g], vn[r0:r0 + SGU_CHUNK, c0:c0 + LANE], preferred_element_type=F32)
            vm = vm + bs_ref[:, c0:c0 + LANE]
            o_ref[r0:r0 + SGU_CHUNK, c0:c0 + LANE] = (u[r0:r0 + SGU_CHUNK, c0:c0 + LANE] * vm).astype(o_ref.dtype)


def _sgu(rest, ln_g, ln_b, ws, bs, seq):
    n = rest.shape[0]
    tm = min(512, seq)
    kern = functools.partial(_sgu_kernel, tm=tm)
    return pl.pallas_call(
        kern,
        grid=(n // tm,),
        in_specs=[
            pl.BlockSpec((tm, 2 * BRANCH_W), lambda i: (i, R_UV // (2 * BRANCH_W))),
            pl.BlockSpec((1, BRANCH_W), lambda i: (0, 0)),
            pl.BlockSpec((1, BRANCH_W), lambda i: (0, 0)),
            pl.BlockSpec((SGU_GROUPS, SGU_CHUNK, SGU_CHUNK), lambda i: (0, 0, 0)),
            pl.BlockSpec((SGU_CHUNK, BRANCH_W), lambda i: (0, 0)),
        ],
        out_specs=pl.BlockSpec((tm, BRANCH_W), lambda i: (i, 0)),
        out_shape=jax.ShapeDtypeStruct((n, BRANCH_W), BF16),
        compiler_params=_params("parallel"),
    )(rest, ln_g.reshape(1, -1), ln_b.reshape(1, -1), ws, bs)


def _lru_kernel(xf_ref, xfp_ref, xfn_ref, xr_ref, xrp_ref, xrn_ref, cw_ref, cb_ref, wr_ref, br_ref, wi_ref,
                bi_ref, lam_ref, hf_ref, hr_ref, af_ref, uf_ref, ar_ref, ur_ref, hc_ref, *, tt, tps):
    j = pl.program_id(0) % tps
    rowi = lax.broadcasted_iota(I32, (tt, LRU_W), 0)

    def gates(x_ref, xp_ref, xn_ref, at_start, at_end, d, a_ref, u_ref):
        x = x_ref[...].astype(F32)
        prev = jnp.where(at_start, 0.0, xp_ref[...].astype(F32)[HALO - 1:HALO, :])
        nxt = jnp.where(at_end, 0.0, xn_ref[...].astype(F32)[0:2, :])
        xm1 = jnp.where(rowi == 0, prev, pltpu.roll(x, 1, 0))
        xp1 = jnp.where(rowi == tt - 1, nxt[0:1, :], pltpu.roll(x, tt - 1, 0))
        xp2 = jnp.where(rowi == tt - 2, nxt[0:1, :],
                        jnp.where(rowi == tt - 1, nxt[1:2, :], pltpu.roll(x, tt - 2, 0)))
        xc = xm1 * cw_ref[0:1, :] + x * cw_ref[1:2, :] + xp1 * cw_ref[2:3, :] + xp2 * cw_ref[3:4, :] + cb_ref[...]
        xcb = xc.astype(BF16)

        def blockdiag(w_ref):
            return jnp.concatenate(
                [jnp.dot(xcb[:, g * LRU_BW:(g + 1) * LRU_BW], w_ref[d, g], preferred_element_type=F32)
                 for g in range(LRU_BLOCKS)], axis=1)

        r = jax.nn.sigmoid(blockdiag(wr_ref) + br_ref[d:d + 1, :])
        ig = jax.nn.sigmoid(blockdiag(wi_ref) + bi_ref[d:d + 1, :])
        log_a = (-LRU_C * r) * jax.nn.softplus(-lam_ref[d:d + 1, :])
        a = jnp.exp(log_a)
        u = jnp.sqrt(jnp.tanh(-log_a) * (1.0 + a * a)) * (ig * xc)
        a_ref[...] = a
        u_ref[...] = u

    gates(xf_ref, xfp_ref, xfn_ref, j == 0, j == tps - 1, 0, af_ref, uf_ref)
    gates(xr_ref, xrp_ref, xrn_ref, j == tps - 1, j == 0, 1, ar_ref, ur_ref)

    @pl.when(j == 0)
    def _():
        hc_ref[...] = jnp.zeros(hc_ref.shape, F32)

    ng = tt // SUBLANE
    sub = lax.broadcasted_iota(I32, (SUBLANE, LRU_W), 0)

    def body(g, carry):
        hf_prev, hr_prev = carry
        r0 = pl.multiple_of(g * SUBLANE, SUBLANE)
        a = af_ref[pl.ds(r0, SUBLANE), :]
        u = uf_ref[pl.ds(r0, SUBLANE), :]
        for s in (1, 2, 4):
            a_s = jnp.where(sub >= s, pltpu.roll(a, s, 0), 1.0)
            u_s = jnp.where(sub >= s, pltpu.roll(u, s, 0), 0.0)
            u = a * u_s + u
            a = a * a_s
        h = a * hf_prev + u
        hf_ref[pl.ds(r0, SUBLANE), :] = h
        hf_new = jnp.broadcast_to(h[SUBLANE - 1:SUBLANE, :], (SUBLANE, LRU_W))

        r1 = pl.multiple_of((ng - 1 - g) * SUBLANE, SUBLANE)
        a = ar_ref[pl.ds(r1, SUBLANE), :]
        u = ur_ref[pl.ds(r1, SUBLANE), :]
        for s in (1, 2, 4):
            a_s = jnp.where(sub < SUBLANE - s, pltpu.roll(a, SUBLANE - s, 0), 1.0)
            u_s = jnp.where(sub < SUBLANE - s, pltpu.roll(u, SUBLANE - s, 0), 0.0)
            u = a * u_s + u
            a = a * a_s
        h = a * hr_prev + u
        hr_ref[pl.ds(r1, SUBLANE), :] = h
        hr_new = jnp.broadcast_to(h[0:1, :], (SUBLANE, LRU_W))
        return hf_new, hr_new

    hf_c, hr_c = lax.fori_loop(0, ng, body, (hc_ref[0], hc_ref[1]))
    hc_ref[0] = hf_c
    hc_ref[1] = hr_c


def _lru(rest, cw, cb, wr, br, wi, bi, lam, seq):
    n = rest.shape[0]
    tt = min(512, seq)
    tps = seq // tt
    nt = n // tt
    t8 = tt // HALO
    last8 = n // HALO - 1
    cblk = R_XC // LRU_W

    def rev(i):
        return (i // tps) * tps + (tps - 1 - i % tps)

    kern = functools.partial(_lru_kernel, tt=tt, tps=tps)
    full = lambda shape: pl.BlockSpec(shape, lambda i: (0,) * len(shape))
    return pl.pallas_call(
        kern,
        grid=(nt,),
        in_specs=[
            pl.BlockSpec((tt, LRU_W), lambda i: (i, cblk)),
            pl.BlockSpec((HALO, LRU_W), lambda i: (jnp.maximum(i * t8 - 1, 0), cblk)),
            pl.BlockSpec((HALO, LRU_W), lambda i: (jnp.minimum((i + 1) * t8, last8), cblk)),
            pl.BlockSpec((tt, LRU_W), lambda i: (rev(i), cblk)),
            pl.BlockSpec((HALO, LRU_W), lambda i: (jnp.maximum(rev(i) * t8 - 1, 0), cblk)),
            pl.BlockSpec((HALO, LRU_W), lambda i: (jnp.minimum((rev(i) + 1) * t8, last8), cblk)),
            full((4, LRU_W)),
            full((1, LRU_W)),
            full((2, LRU_BLOCKS, LRU_BW, LRU_BW)),
            full((2, LRU_W)),
            full((2, LRU_BLOCKS, LRU_BW, LRU_BW)),
            full((2, LRU_W)),
            full((2, LRU_W)),
        ],
        out_specs=[
            pl.BlockSpec((tt, LRU_W), lambda i: (i, 0)),
            pl.BlockSpec((tt, LRU_W), lambda i: (rev(i), 0)),
        ],
        out_shape=[jax.ShapeDtypeStruct((n, LRU_W), F32), jax.ShapeDtypeStruct((n, LRU_W), F32)],
        scratch_shapes=[pltpu.VMEM((tt, LRU_W), F32)] * 4 + [pltpu.VMEM((2, SUBLANE, LRU_W), F32)],
        compiler_params=_params("arbitrary"),
    )(rest, rest, rest, rest, rest, rest, cw, cb.reshape(1, -1), wr, br, wi, bi, lam)


def _merge_kernel(ya_ref, yb_ref, hf_ref, hr_ref, gate_ref, yd_ref, wbr_ref, g0_ref, g1_ref, g2_ref, g3_ref, o_ref):
    yc = (jax.nn.gelu(gate_ref[...].astype(F32)) * (hf_ref[...] + hr_ref[...])).astype(BF16)
    ys = (ya_ref[...], yb_ref[...], yc, yd_ref[...])
    gls = (g0_ref, g1_ref, g2_ref, g3_ref)
    acc = None
    for k in range(N_BRANCH):
        t = jax.nn.sigmoid(gls[k][...].astype(F32)) * jnp.dot(ys[k], wbr_ref[k], preferred_element_type=F32)
        acc = t if acc is None else acc + t
    o_ref[...] = acc.astype(o_ref.dtype)


def _merge(ya, yb, hf, hr, rest, yd, wbr, seq):
    n = ya.shape[0]
    d = wbr.shape[-1]
    tm = min(512, seq)
    tn = 512
    row = lambda w: pl.BlockSpec((tm, w), lambda i, j: (i, 0))

    def gl_spec(k):
        base = (R_GL + k * d) // tn
        return pl.BlockSpec((tm, tn), lambda i, j: (i, base + j))

    return pl.pallas_call(
        _merge_kernel,
        grid=(n // tm, d // tn),
        in_specs=[
            row(BRANCH_W), row(BRANCH_W), row(LRU_W), row(LRU_W),
            pl.BlockSpec((tm, LRU_W), lambda i, j: (i, R_GATE // LRU_W)),
            row(BRANCH_W),
            pl.BlockSpec((N_BRANCH, BRANCH_W, tn), lambda i, j: (0, 0, j)),
            gl_spec(0), gl_spec(1), gl_spec(2), gl_spec(3),
        ],
        out_specs=pl.BlockSpec((tm, tn), lambda i, j: (i, j)),
        out_shape=jax.ShapeDtypeStruct((n, d), BF16),
        compiler_params=_params("parallel", "parallel"),
    )(ya, yb, hf, hr, rest, yd, wbr, rest, rest, rest, rest)


def _outproj_kernel(m_ref, w_ref, x_ref, g_ref, o_ref):
    o_ref[...] = x_ref[...] + g_ref[...] * jnp.dot(m_ref[...], w_ref[...], preferred_element_type=F32)


def _outproj(merged, w, x, g1, seq):
    n, d = x.shape
    tm = min(512, seq)
    tps = seq // tm
    return pl.pallas_call(
        _outproj_kernel,
        grid=(n // tm,),
        in_specs=[
            pl.BlockSpec((tm, d), lambda i: (i, 0)),
            pl.BlockSpec((d, d), lambda i: (0, 0)),
            pl.BlockSpec((tm, d), lambda i: (i, 0)),
            pl.BlockSpec((None, 1, d), lambda i: (i // tps, 0, 0)),
        ],
        out_specs=pl.BlockSpec((tm, d), lambda i: (i, 0)),
        out_shape=jax.ShapeDtypeStruct((n, d), F32),
        compiler_params=_params("parallel"),
    )(merged, w, x, g1)


def _router_kernel(x_ref, g_ref, mod_ref, wh_ref, wl_ref, rb_ref, tri_ref, g2_ref, swg_ref, swu_ref, swd_ref,
                   x1_ref, h2g_ref, e_ref, w_ref, r_ref, cnt_ref, run_ref, *, tm, d):
    @pl.when(pl.program_id(0) == 0)
    def _():
        run_ref[...] = jnp.zeros(run_ref.shape, F32)

    x = x_ref[...]
    y = x * lax.rsqrt(jnp.mean(x * x, axis=-1, keepdims=True) + EPS) * g_ref[...]
    h = y * (1.0 + mod_ref[0:1, :]) + mod_ref[1:2, :]
    hb = h.astype(BF16)
    hs = (jax.nn.silu(jnp.dot(hb, swg_ref[...], preferred_element_type=F32))
          * jnp.dot(hb, swu_ref[...], preferred_element_type=F32)).astype(BF16)
    x1_ref[...] = x + g2_ref[...] * jnp.dot(hs, swd_ref[...], preferred_element_type=F32)
    slabs = d // LANE
    for s in range(slabs):
        h2g_ref[pl.ds(s, tm, stride=slabs), :] = h[:, s * LANE:(s + 1) * LANE]

    hl = (h - hb.astype(F32)).astype(BF16)
    logits = (lax.dot_general(wh_ref[...], hb, NT_DIMS, preferred_element_type=F32)
              + lax.dot_general(wl_ref[...], hb, NT_DIMS, preferred_element_type=F32)
              + lax.dot_general(wh_ref[...], hl, NT_DIMS, preferred_element_type=F32))
    sc = jax.nn.sigmoid(logits)
    sel = sc + rb_ref[...]

    neg = -jnp.inf
    i8 = lax.broadcasted_iota(I32, (GROUP_SZ, tm), 0).astype(F32)
    rows = []
    for g in range(N_GROUPS):
        blk = sel[g * GROUP_SZ:(g + 1) * GROUP_SZ, :]
        m1 = jnp.max(blk, axis=0, keepdims=True)
        i1 = jnp.min(jnp.where(blk == m1, i8, float(GROUP_SZ)), axis=0, keepdims=True)
        m2 = jnp.max(jnp.where(i8 == i1, neg, blk), axis=0, keepdims=True)
        rows.append(m1 + m2)
    gs = jnp.concatenate(rows, axis=0)
    gi8 = lax.broadcasted_iota(I32, (N_GROUPS, tm), 0).astype(F32)
    gsel = jnp.zeros((N_GROUPS, tm), F32)
    for _ in range(TOPK_GROUPS):
        gm = jnp.max(gs, axis=0, keepdims=True)
        gi = jnp.min(jnp.where(gs == gm, gi8, float(N_GROUPS)), axis=0, keepdims=True)
        hit = gi8 == gi
        gsel = jnp.where(hit, 1.0, gsel)
        gs = jnp.where(hit, neg, gs)
    emask = jnp.concatenate(
        [jnp.broadcast_to(gsel[g:g + 1, :], (GROUP_SZ, tm)) for g in range(N_GROUPS)], axis=0)
    selm = jnp.where(emask > 0.0, sel, neg)

    i64 = lax.broadcasted_iota(I32, (N_EXPERTS, tm), 0).astype(F32)
    chosen = jnp.zeros((N_EXPERTS, tm), F32)
    idxs, wts = [], []
    for _ in range(TOP_K):
        mx = jnp.max(selm, axis=0, keepdims=True)
        ix = jnp.min(jnp.where(selm == mx, i64, float(N_EXPERTS)), axis=0, keepdims=True)
        oh = i64 == ix
        wts.append(jnp.sum(jnp.where(oh, sc, 0.0), axis=0, keepdims=True))
        idxs.append(ix)
        selm = jnp.where(oh, neg, selm)
        chosen = jnp.where(oh, 1.0, chosen)
    wk = jnp.concatenate(wts, axis=0)
    w_ref[...] = wk / jnp.sum(wk, axis=0, keepdims=True) * ROUTE_SCALE
    e_ref[...] = jnp.concatenate(idxs, axis=0).astype(I32)

    before = jnp.dot(chosen.astype(BF16), tri_ref[...], preferred_element_type=F32)
    rank = run_ref[...] + before
    r_ref[...] = jnp.concatenate(
        [jnp.sum(jnp.where(i64 == idxs[k], rank, 0.0), axis=0, keepdims=True) for k in range(TOP_K)],
        axis=0).astype(I32)
    run_ref[...] = run_ref[...] + jnp.sum(chosen, axis=1, keepdims=True)
    cnt_ref[...] = jnp.broadcast_to(run_ref[...], cnt_ref.shape).astype(I32)


def _router(x, g, mod, wh, wl, rb, g2, swg, swu, swd, seq):
    n, d = x.shape
    ds_ = swg.shape[1]
    tm = min(512, seq)
    tps = seq // tm
    slabs = d // LANE
    tri = jnp.triu(jnp.ones((tm, tm), F32), 1).astype(BF16)
    kern = functools.partial(_router_kernel, tm=tm, d=d)
    return pl.pallas_call(
        kern,
        grid=(n // tm,),
        in_specs=[
            pl.BlockSpec((tm, d), lambda i: (i, 0)),
            pl.BlockSpec((1, d), lambda i: (0, 0)),
            pl.BlockSpec((None, 2, d), lambda i: (i // tps, 0, 0)),
            pl.BlockSpec((N_EXPERTS, d), lambda i: (0, 0)),
            pl.BlockSpec((N_EXPERTS, d), lambda i: (0, 0)),
            pl.BlockSpec((N_EXPERTS, 1), lambda i: (0, 0)),
            pl.BlockSpec((tm, tm), lambda i: (0, 0)),
            pl.BlockSpec((None, 1, d), lambda i: (i // tps, 0, 0)),
            pl.BlockSpec((d, ds_), lambda i: (0, 0)),
            pl.BlockSpec((d, ds_), lambda i: (0, 0)),
            pl.BlockSpec((ds_, d), lambda i: (0, 0)),
        ],
        out_specs=[
            pl.BlockSpec((tm, d), lambda i: (i, 0)),
            pl.BlockSpec((tm * slabs, LANE), lambda i: (i, 0)),
            pl.BlockSpec((TOP_K, tm), lambda i: (0, i)),
            pl.BlockSpec((TOP_K, tm), lambda i: (0, i)),
            pl.BlockSpec((TOP_K, tm), lambda i: (0, i)),
            pl.BlockSpec((N_EXPERTS, LANE), lambda i: (0, 0)),
        ],
        out_shape=[
            jax.ShapeDtypeStruct((n, d), F32),
            jax.ShapeDtypeStruct((n * slabs, LANE), F32),
            jax.ShapeDtypeStruct((TOP_K, n), I32),
            jax.ShapeDtypeStruct((TOP_K, n), F32),
            jax.ShapeDtypeStruct((TOP_K, n), I32),
            jax.ShapeDtypeStruct((N_EXPERTS, LANE), I32),
        ],
        scratch_shapes=[pltpu.VMEM((N_EXPERTS, 1), F32)],
        compiler_params=_params("arbitrary"),
    )(x, g.reshape(1, d), mod, wh, wl, rb.reshape(N_EXPERTS, 1), tri, g2, swg, swu, swd)


def _dispatch_kernel(dest_ref, pend_ref, pc_ref, h_ref, xs_ref, zero_ref, sem, zsem, *, tm, n, slabs):
    i = pl.program_id(0)
    rows_blk = MOE_BLK * slabs

    def tail_copy(e):
        start = pl.multiple_of((pend_ref[e] - MOE_BLK) * slabs, rows_blk)
        return pltpu.make_async_copy(zero_ref, xs_ref.at[pl.ds(start, rows_blk), :], zsem)

    @pl.when(i == 0)
    def _():
        zero_ref[...] = jnp.zeros(zero_ref.shape, F32)

        def zstart(e, c):
            @pl.when(pc_ref[e] > 0)
            def _():
                tail_copy(e).start()
            return c

        def zwait(e, c):
            @pl.when(pc_ref[e] > 0)
            def _():
                tail_copy(e).wait()
            return c

        lax.fori_loop(0, N_EXPERTS, zstart, 0)
        lax.fori_loop(0, N_EXPERTS, zwait, 0)

    def row_copy(r, k):
        src = h_ref.at[pl.ds(pl.multiple_of(r * slabs, slabs), slabs), :]
        dst_row = dest_ref[k * n + i * tm + r]
        dst = xs_ref.at[pl.ds(pl.multiple_of(dst_row * slabs, slabs), slabs), :]
        return pltpu.make_async_copy(src, dst, sem)

    def start(r, c):
        for k in range(TOP_K):
            row_copy(r, k).start(priority=k % 2)
        return c

    lax.fori_loop(0, tm, start, 0)
    for _ in range(TOP_K):
        pltpu.make_async_copy(h_ref, xs_ref.at[pl.ds(0, tm * slabs), :], sem).wait()


def _dispatch(dest, pend, pc, h2g, n_rows, seq):
    n = dest.shape[0] // TOP_K
    slabs = h2g.shape[0] // n
    tm = min(256, seq)
    kern = functools.partial(_dispatch_kernel, tm=tm, n=n, slabs=slabs)
    return pl.pallas_call(
        kern,
        grid_spec=pltpu.PrefetchScalarGridSpec(
            num_scalar_prefetch=3,
            grid=(n // tm,),
            in_specs=[pl.BlockSpec((tm * slabs, LANE), lambda i, *_: (i, 0))],
            out_specs=pl.BlockSpec(memory_space=pl.ANY),
            scratch_shapes=[
                pltpu.VMEM((MOE_BLK * slabs, LANE), F32),
                pltpu.SemaphoreType.DMA(()),
                pltpu.SemaphoreType.DMA(()),
            ],
        ),
        out_shape=jax.ShapeDtypeStruct((n_rows * slabs, LANE), F32),
        compiler_params=_params("arbitrary"),
    )(dest, pend, pc, h2g)


def _gmm_kernel(be_ref, nu_ref, x_ref, wg_ref, wu_ref, wd_ref, o_ref, *, slabs):
    @pl.when(pl.program_id(0) < nu_ref[0])
    def _():
        x = jnp.concatenate(
            [x_ref[pl.ds(s, MOE_BLK, stride=slabs), :].astype(BF16) for s in range(slabs)], axis=1)
        hg = jnp.dot(x, wg_ref[...], preferred_element_type=F32)
        hu = jnp.dot(x, wu_ref[...], preferred_element_type=F32)
        hb = (jax.nn.silu(hg) * hu).astype(BF16)
        per = 4
        for c in range(slabs // per):
            y = jnp.dot(hb, wd_ref[:, c * per * LANE:(c + 1) * per * LANE], preferred_element_type=F32)
            for j in range(per):
                o_ref[pl.ds(c * per + j, MOE_BLK, stride=slabs), :] = y[:, j * LANE:(j + 1) * LANE]


def _gmm(blk_e, nused, xs, wg, wu, wd, layer):
    _, _, d, de = wg.shape
    slabs = d // LANE
    nblk = xs.shape[0] // (MOE_BLK * slabs)
    kern = functools.partial(_gmm_kernel, slabs=slabs)

    def blk(b, be, nu):
        return jnp.minimum(b, nu[0] - 1)

    return pl.pallas_call(
        kern,
        grid_spec=pltpu.PrefetchScalarGridSpec(
            num_scalar_prefetch=2,
            grid=(nblk,),
            in_specs=[
                pl.BlockSpec((MOE_BLK * slabs, LANE), lambda b, be, nu: (blk(b, be, nu), 0)),
                pl.BlockSpec((None, None, d, de), lambda b, be, nu: (layer, be[blk(b, be, nu)], 0, 0)),
                pl.BlockSpec((None, None, d, de), lambda b, be, nu: (layer, be[blk(b, be, nu)], 0, 0)),
                pl.BlockSpec((None, None, de, d), lambda b, be, nu: (layer, be[blk(b, be, nu)], 0, 0)),
            ],
            out_specs=pl.BlockSpec((MOE_BLK * slabs, LANE), lambda b, be, nu: (blk(b, be, nu), 0)),
        ),
        out_shape=jax.ShapeDtypeStruct(xs.shape, F32),
        compiler_params=_params("arbitrary"),
    )(blk_e, nused, xs, wg, wu, wd)


def _combine_kernel(dest_ref, wt_ref, x_ref, g2_ref, fg_ref, ys_ref, o_ref, buf_ref, wb_ref, sem,
                    *, tm, n, slabs, nt, final):
    i = pl.program_id(0)
    slot = i % 2
    rows = tm * COMBINE_PITCH

    def gather(tile, sl):
        def start(r, c):
            for k in range(TOP_K):
                src_row = dest_ref[k * n + tile * tm + r]
                src = ys_ref.at[pl.ds(pl.multiple_of(src_row * slabs, slabs), slabs), :]
                dst = buf_ref.at[sl, pl.ds(pl.multiple_of(k * rows + r * COMBINE_PITCH, SUBLANE), slabs), :]
                pltpu.make_async_copy(src, dst, sem.at[sl]).start(priority=k % 2)
            return c

        lax.fori_loop(0, tm, start, 0)

    @pl.when(i == 0)
    def _():
        gather(0, 0)

    @pl.when(i + 1 < nt)
    def _():
        gather(i + 1, 1 - slot)

    moved = TOP_K * tm * slabs
    pltpu.make_async_copy(ys_ref.at[pl.ds(0, moved), :], buf_ref.at[slot, pl.ds(0, moved), :], sem.at[slot]).wait()

    wt = wt_ref[...]
    for k in range(TOP_K):
        wb_ref[k] = jnp.broadcast_to(wt[:, k:k + 1], (tm, LANE))
    for s in range(slabs):
        c0 = s * LANE
        routed = None
        for k in range(TOP_K):
            t = wb_ref[k] * buf_ref[slot, pl.ds(k * rows + s, tm, stride=COMBINE_PITCH), :]
            routed = t if routed is None else routed + t
        o_ref[:, c0:c0 + LANE] = x_ref[:, c0:c0 + LANE] + g2_ref[:, c0:c0 + LANE] * routed
    if final:
        o = o_ref[...]
        o_ref[...] = o * lax.rsqrt(jnp.mean(o * o, axis=-1, keepdims=True) + EPS) * fg_ref[...]


def _combine(dest, wt, x, g2, ys, final_g, final, seq):
    n, d = x.shape
    slabs = d // LANE
    tm = min(128, seq)
    tps = seq // tm
    assert COMBINE_PITCH >= slabs
    kern = functools.partial(_combine_kernel, tm=tm, n=n, slabs=slabs, nt=n // tm, final=final)
    return pl.pallas_call(
        kern,
        grid_spec=pltpu.PrefetchScalarGridSpec(
            num_scalar_prefetch=1,
            grid=(n // tm,),
            in_specs=[
                pl.BlockSpec((tm, TOP_K), lambda i, *_: (i, 0)),
                pl.BlockSpec((tm, d), lambda i, *_: (i, 0)),
                pl.BlockSpec((None, 1, d), lambda i, *_: (i // tps, 0, 0)),
                pl.BlockSpec((1, d), lambda i, *_: (0, 0)),
                pl.BlockSpec(memory_space=pl.ANY),
            ],
            out_specs=pl.BlockSpec((tm, d), lambda i, *_: (i, 0)),
            scratch_shapes=[
                pltpu.VMEM((2, TOP_K * tm * COMBINE_PITCH, LANE), F32),
                pltpu.VMEM((TOP_K, tm, LANE), F32),
                pltpu.SemaphoreType.DMA((2,)),
            ],
        ),
        out_shape=jax.ShapeDtypeStruct((n, d), F32),
        compiler_params=_params("arbitrary"),
    )(dest, wt, x, g2, final_g.reshape(1, d), ys)


def _swap_halves(w):
    half = w.shape[-1] // 2
    return jnp.concatenate([w[..., half:], w[..., :half]], axis=-1)


def _prep_w_in(w):
    depth, d, _ = w.shape
    sizes = [HA * 2 * DHA, HA * 2 * DHA, HA * DVA, 2 * BRANCH_W, 2 * LRU_W, Q_LORA, KV_LORA, D_ROPE, N_BRANCH * d]
    offs = np.cumsum([0] + sizes)
    qa, ka, va, uv, xg, cq, ckv, kr, gl = [w[:, :, offs[i]:offs[i + 1]] for i in range(len(sizes))]
    pad = jnp.zeros((depth, d, 512 - KV_LORA - 2 * D_ROPE), w.dtype)
    w_att = jnp.concatenate([qa * (DHA ** -0.5 * LOG2E), ka, va], axis=2).astype(BF16)
    w_rest = jnp.concatenate([uv, xg, cq, ckv, kr, _swap_halves(kr), pad, gl], axis=2).astype(BF16)
    return w_att, w_rest


def _prep_layer(l, p):
    wuq =p['w_uq'][l].reshape(Q_LORA, HD, D_NOPE + D_ROPE)
    wuq_r = wuq[:, :, D_NOPE:]
    wuq_ext = jnp.concatenate([wuq[:, :, :D_NOPE], wuq_r, _swap_halves(wuq_r)], axis=-1)
    wuq_ext = wuq_ext.reshape(Q_LORA, HD * 2 * LANE).astype(BF16)

    lam_init = 0.8 - 0.6 * math.exp(-0.3 * l)
    lam = (jnp.exp(jnp.sum(p['lam_q1'][l] * p['lam_k1'][l])) - jnp.exp(jnp.sum(p['lam_q2'][l] * p['lam_k2'][l]))
           + lam_init)
    slopes = jnp.asarray(2.0 ** (-8.0 * np.arange(1, HA + 1) / HA) * LOG2E, F32)
    par = jnp.concatenate([slopes, lam.reshape(1), jnp.full((1,), 1.0 - lam_init, F32), jnp.zeros((2,), F32)])

    rw = p['router_w'][l].T
    rw_hi = rw.astype(BF16)
    rw_lo = (rw - rw_hi.astype(F32)).astype(BF16)
    bs = jnp.repeat(p['sgu_b'][l].T, LANE, axis=1)
    return dict(
        wuq=wuq_ext, wukv=p['w_ukv'][l].astype(BF16), par=par,
        sgu_w=p['sgu_w'][l].astype(BF16), sgu_bs=bs,
        rg_wr=p['rg_wr'][l].astype(BF16), rg_wi=p['rg_wi'][l].astype(BF16),
        w_branch=p['w_branch'][l].astype(BF16), w_out=p['w_out'][l].astype(BF16),
        rw_hi=rw_hi, rw_lo=rw_lo,
        sh_wg=p['sh_wg'][l].astype(BF16), sh_wu=p['sh_wu'][l].astype(BF16), sh_wd=p['sh_wd'][l].astype(BF16),
    )


def _rope_table(n_seq, seq):
    inv_freq = ROPE_THETA ** (-jnp.arange(0, D_ROPE, 2, dtype=F32) / D_ROPE)
    ang = jnp.arange(seq, dtype=F32)[:, None] * inv_freq[None, :]
    cos, sin = jnp.cos(ang), jnp.sin(ang)
    tab = jnp.concatenate([cos, cos, -sin, sin], axis=1)
    return jnp.tile(tab, (n_seq, 1))


def _trunk(x3, mods, p, preps):
    n_seq, seq, d = x3.shape
    n = n_seq * seq
    x = x3.reshape(n, d)
    tab = _rope_table(n_seq, seq)
    depth = len(preps)
    for l in range(depth):
        w = preps[l]
        mod = mods[l]
        mod1 = jnp.stack([mod[:, 1], mod[:, 0]], axis=1)
        mod2 = jnp.stack([mod[:, 4], mod[:, 3]], axis=1)
        g1 = mod[:, 2:3]
        g2 = mod[:, 5:6]

        att = _inproj(x, p['norm1_g'][l], mod1, p['w_att'], l, BF16, seq)
        rest = _inproj(x, p['norm1_g'][l], mod1, p['w_rest'], l, BF16, seq)
        ya = _diff_attn(att, w['par'], p['subln_g'][l], n_seq, seq)
        yb = _sgu(rest, p['sgu_ln_g'][l], p['sgu_ln_b'][l], w['sgu_w'], w['sgu_bs'], seq)
        hf, hr = _lru(rest, p['conv_w'][l], p['conv_b'][l], w['rg_wr'], p['rg_br'][l], w['rg_wi'],
                      p['rg_bi'][l], p['rg_lam'][l], seq)
        q, k, v = _mla_prep(rest, tab, p['q_norm_g'][l], p['kv_norm_g'][l], w['wuq'], w['wukv'], seq)
        yd = _mla_attn(q, k, v, n_seq, seq)
        merged = _merge(ya, yb, hf, hr, rest, yd, w['w_branch'], seq)
        x = _outproj(merged, w['w_out'], x, g1, seq)

        x, h2g, eidx, wgt, rank, cnt = _router(x, p['norm2_g'][l], mod2, w['rw_hi'], w['rw_lo'],
                                               p['router_bias'][l], g2, w['sh_wg'], w['sh_wu'], w['sh_wd'], seq)
        counts = cnt[:, 0]
        pc = ((counts + MOE_BLK - 1) // MOE_BLK) * MOE_BLK
        pend = jnp.cumsum(pc).astype(I32)
        pstart = pend - pc
        eid = jnp.arange(N_EXPERTS, dtype=I32)
        dest = (jnp.sum(jnp.where(eidx[..., None] == eid, pstart, 0), axis=-1) + rank).reshape(-1).astype(I32)
        nblk = n * TOP_K // MOE_BLK + N_EXPERTS
        blk_row = jnp.arange(nblk, dtype=I32)[:, None] * MOE_BLK
        blk_e = jnp.minimum(jnp.sum((pend[None, :] <= blk_row).astype(I32), axis=1), N_EXPERTS - 1)
        nused = (pend[-1:] // MOE_BLK).astype(I32)
        xs = _dispatch(dest, pend, pc.astype(I32), h2g, nblk * MOE_BLK, seq)
        ys = _gmm(blk_e, nused, xs, p['exp_wg_bf16'], p['exp_wu_bf16'], p['exp_wd_bf16'], l)
        x = _combine(dest, wgt.T, x, g2, ys, p['final_g'], l == depth - 1, seq)
    return x.reshape(n_seq, seq, d)


def kernel(x_prompt, x_sample, c_prompt, c_sample, ada_w, ada_b, norm1_g, norm2_g, w_in, lam_q1, lam_k1, lam_q2, lam_k2, subln_g, sgu_ln_g, sgu_ln_b, sgu_w, sgu_b, conv_w, conv_b, rg_wr, rg_br, rg_wi, rg_bi, rg_lam, q_norm_g, kv_norm_g, w_uq, w_ukv, w_branch, w_out, router_w, router_bias, exp_wg, exp_wu, exp_wd, sh_wg, sh_wu, sh_wd, final_g):
    p = dict(ada_w=ada_w, ada_b=ada_b, norm1_g=norm1_g, norm2_g=norm2_g, w_in=w_in,
             lam_q1=lam_q1, lam_k1=lam_k1, lam_q2=lam_q2, lam_k2=lam_k2, subln_g=subln_g,
             sgu_ln_g=sgu_ln_g, sgu_ln_b=sgu_ln_b, sgu_w=sgu_w, sgu_b=sgu_b,
             conv_w=conv_w, conv_b=conv_b, rg_wr=rg_wr, rg_br=rg_br, rg_wi=rg_wi, rg_bi=rg_bi,
             rg_lam=rg_lam, q_norm_g=q_norm_g, kv_norm_g=kv_norm_g, w_uq=w_uq, w_ukv=w_ukv,
             w_branch=w_branch, w_out=w_out, router_w=router_w, router_bias=router_bias,
             exp_wg=exp_wg, exp_wu=exp_wu, exp_wd=exp_wd, sh_wg=sh_wg, sh_wu=sh_wu, sh_wd=sh_wd,
             final_g=final_g)
    depth, d, _ = ada_w.shape
    bp, bs = x_prompt.shape[0], x_sample.shape[0]
    assert bp + bs <= 8
    c8 = jnp.concatenate([c_prompt, c_sample, jnp.zeros((8 - bp - bs, d), F32)], axis=0)
    mod = _ada_mod(c8, ada_w, ada_b).reshape(depth, 8, 6, d)
    p['w_att'], p['w_rest'] = _prep_w_in(w_in)
    p.update(exp_wg_bf16=exp_wg.astype(BF16), exp_wu_bf16=exp_wu.astype(BF16), exp_wd_bf16=exp_wd.astype(BF16))
    preps = [_prep_layer(l, p) for l in range(depth)]
    y_prompt = _trunk(x_prompt, mod[:, :bp], p, preps)
    y_sample = _trunk(x_sample, mod[:, bp:bp + bs], p, preps)
    return (y_prompt, y_sample)
```

```python
import functools
import math

import numpy as np
import jax
import jax.numpy as jnp
from jax import lax
from jax.experimental import pallas as pl
from jax.experimental.pallas import tpu as pltpu

F32 = jnp.float32
BF16 = jnp.bfloat16
I32 = jnp.int32

EPS = 1e-6
LOG2E = math.log2(math.e)
LANE = 128
SUBLANE = 8
VMEM_LIMIT = 48 * 1024 * 1024

HA = 4
DHA = 64
DVA = 2 * DHA
BRANCH_W = 512
SGU_CHUNK = 128
SGU_GROUPS = 4
LRU_W = 512
LRU_BLOCKS = 4
LRU_BW = LRU_W // LRU_BLOCKS
LRU_C = 8.0
HALO = 16
HD = 4
Q_LORA = 512
KV_LORA = 256
D_NOPE = 128
D_ROPE = 64
DV_D = 128
ROPE_THETA = 10000.0
N_EXPERTS = 64
TOP_K = 8
N_GROUPS = 8
TOPK_GROUPS = 4
GROUP_SZ = N_EXPERTS // N_GROUPS
ROUTE_SCALE = 2.5
MOE_BLK = 256
N_BRANCH = 4
COMBINE_PITCH = 24

R_UV = 0
R_XC = 1024
R_GATE = 1536
R_CQ = 2048
R_CKV = 2560
R_GL = 3072

NT_DIMS = (((1,), (1,)), ((), ()))


def _params(*sem):
    return pltpu.CompilerParams(dimension_semantics=sem, vmem_limit_bytes=VMEM_LIMIT)


def _ada_kernel(c_ref, w_ref, b_ref, o_ref):
    a = jax.nn.silu(c_ref[...]).astype(BF16)
    o_ref[...] = jnp.dot(a, w_ref[...].astype(BF16), preferred_element_type=F32) + b_ref[...]


def _ada_mod(c8, ada_w, ada_b):
    depth, d, n = ada_w.shape
    tn = 1024
    return pl.pallas_call(
        _ada_kernel,
        grid=(depth, n // tn),
        in_specs=[
            pl.BlockSpec((8, d), lambda l, j: (0, 0)),
            pl.BlockSpec((None, d, tn), lambda l, j: (l, 0, j)),
            pl.BlockSpec((None, 1, tn), lambda l, j: (l, 0, j)),
        ],
        out_specs=pl.BlockSpec((None, 8, tn), lambda l, j: (l, 0, j)),
        out_shape=jax.ShapeDtypeStruct((depth, 8, n), F32),
        compiler_params=_params("parallel", "parallel"),
    )(c8, ada_w, ada_b.reshape(depth, 1, n))


def _inproj_kernel(x_ref, g_ref, mod_ref, w_ref, o_ref, h_ref):
    @pl.when(pl.program_id(1) == 0)
    def _():
        x = x_ref[...]
        y = x * lax.rsqrt(jnp.mean(x * x, axis=-1, keepdims=True) + EPS) * g_ref[...]
        h_ref[...] = (y * (1.0 + mod_ref[0:1, :]) + mod_ref[1:2, :]).astype(BF16)

    o_ref[...] = jnp.dot(h_ref[...], w_ref[...], preferred_element_type=F32).astype(o_ref.dtype)


def _inproj(x, g, mod, w, layer, out_dtype, seq):
    n, d = x.shape
    nc = w.shape[2]
    tm = min(1024, seq)
    tn = 1024 if nc % 1024 == 0 else 512
    tps = seq // tm
    return pl.pallas_call(
        _inproj_kernel,
        grid=(n // tm, nc // tn),
        in_specs=[
            pl.BlockSpec((tm, d), lambda i, j: (i, 0)),
            pl.BlockSpec((1, d), lambda i, j: (0, 0)),
            pl.BlockSpec((None, 2, d), lambda i, j: (i // tps, 0, 0)),
            pl.BlockSpec((None, d, tn), lambda i, j: (layer, 0, j)),
        ],
        out_specs=pl.BlockSpec((tm, tn), lambda i, j: (i, j)),
        out_shape=jax.ShapeDtypeStruct((n, nc), out_dtype),
        scratch_shapes=[pltpu.VMEM((tm, d), BF16)],
        compiler_params=_params("parallel", "arbitrary"),
    )(x, g.reshape(1, d), mod, w)


def _online_softmax_step(s, vx, m_ref, acc_ref, idx, tk):
    m_prev = m_ref[idx]
    m_new = jnp.maximum(m_prev, jnp.max(s, axis=-1, keepdims=True))
    alpha = jnp.exp2(m_prev - m_new)
    p = jnp.exp2(s - jnp.concatenate([m_new] * (tk // LANE), axis=1))
    pv = jnp.dot(p.astype(BF16), vx, preferred_element_type=F32)
    acc_ref[idx] = jnp.concatenate([alpha, alpha], axis=1) * acc_ref[idx] + pv
    m_ref[idx] = m_new


def _diff_attn_kernel(par_ref, q_ref, kt_ref, v_ref, g_ref, o_ref, m_ref, acc_ref, *, tq, tk, nk):
    qi = pl.program_id(1)
    ki = pl.program_id(2)

    @pl.when(ki == 0)
    def _():
        m_ref[...] = jnp.full(m_ref.shape, -jnp.inf, F32)
        acc_ref[...] = jnp.zeros(acc_ref.shape, F32)

    lane = lax.broadcasted_iota(I32, (tq, LANE), 1)
    row = lax.broadcasted_iota(I32, (tq, tk), 0)
    col = lax.broadcasted_iota(I32, (tq, tk), 1)
    dist = jnp.abs(row - col + (qi * tq - ki * tk)).astype(F32)
    ones = jnp.ones((tk, LANE), BF16)
    for h in range(HA):
        q = q_ref[:, h * LANE:(h + 1) * LANE]
        kt = kt_ref[h * LANE:(h + 1) * LANE, :]
        vx = jnp.concatenate([v_ref[:, h * LANE:(h + 1) * LANE], ones], axis=1)
        bias = dist * par_ref[h]
        zero = jnp.zeros_like(q)
        for mi, qm in enumerate((jnp.where(lane < DHA, q, zero), jnp.where(lane < DHA, zero, q))):
            s = jnp.dot(qm, kt, preferred_element_type=F32) - bias
            _online_softmax_step(s, vx, m_ref, acc_ref, 2 * h + mi, tk)

    @pl.when(ki == nk - 1)
    def _():
        lam = par_ref[HA]
        for h in range(HA):
            a1 = acc_ref[2 * h]
            a2 = acc_ref[2 * h + 1]
            o = a1[:, :LANE] / a1[:, LANE:] - lam * (a2[:, :LANE] / a2[:, LANE:])
            y = o * lax.rsqrt(jnp.mean(o * o, axis=-1, keepdims=True) + EPS) * g_ref[...]
            o_ref[:, h * LANE:(h + 1) * LANE] = (y * par_ref[HA + 1]).astype(o_ref.dtype)


def _diff_attn(att, par, subln_g, n_seq, seq):
    n = att.shape[0]
    tq = min(1024, seq)
    tk = min(512, seq)
    nq, nk = seq // tq, seq // tk
    w = HA * LANE
    kern = functools.partial(_diff_attn_kernel, tq=tq, tk=tk, nk=nk)
    kt = att[:, w:2 * w].T
    return pl.pallas_call(
        kern,
        grid=(n_seq, nq, nk),
        in_specs=[
            pl.BlockSpec(memory_space=pltpu.SMEM),
            pl.BlockSpec((tq, w), lambda b, qi, ki: (b * nq + qi, 0)),
            pl.BlockSpec((w, tk), lambda b, qi, ki: (0, b * nk + ki)),
            pl.BlockSpec((tk, w), lambda b, qi, ki: (b * nk + ki, 2)),
            pl.BlockSpec((1, DVA), lambda b, qi, ki: (0, 0)),
        ],
        out_specs=pl.BlockSpec((tq, w), lambda b, qi, ki: (b * nq + qi, 0)),
        out_shape=jax.ShapeDtypeStruct((n, HA * DVA), BF16),
        scratch_shapes=[
            pltpu.VMEM((2 * HA, tq, LANE), F32),
            pltpu.VMEM((2 * HA, tq, 2 * LANE), F32),
        ],
        compiler_params=_params("parallel", "parallel", "arbitrary"),
    )(par, att, kt, att, subln_g.reshape(1, DVA))


def _mla_prep_kernel(cq_ref, ckv_ref, tab_ref, gq_ref, gkv_ref, wuq_ref, wukv_ref, q_ref, k_ref, v_ref, *, scale):
    cq = cq_ref[...].astype(F32)
    cqn = (cq * lax.rsqrt(jnp.mean(cq * cq, axis=-1, keepdims=True) + EPS) * gq_ref[...]).astype(BF16)
    c = ckv_ref[...].astype(F32)
    ckv = c[:, :KV_LORA]
    ckvn = (ckv * lax.rsqrt(jnp.mean(ckv * ckv, axis=-1, keepdims=True) + EPS) * gkv_ref[...]).astype(BF16)
    tab = tab_ref[...]
    lane = lax.broadcasted_iota(I32, tab.shape, 1)

    def rope(pair):
        pr = pair * tab
        return jnp.where(lane < D_ROPE, pr + pltpu.roll(pr, D_ROPE, 1), 0.0)

    kr = rope(c[:, KV_LORA:KV_LORA + LANE]).astype(BF16)
    qf = jnp.dot(cqn, wuq_ref[...], preferred_element_type=F32)
    kvf = jnp.dot(ckvn, wukv_ref[...], preferred_element_type=F32)
    for h in range(HD):
        b0 = h * 2 * LANE
        q_ref[:, b0:b0 + LANE] = (qf[:, b0:b0 + LANE] * scale).astype(BF16)
        q_ref[:, b0 + LANE:b0 + 2 * LANE] = (rope(qf[:, b0 + LANE:b0 + 2 * LANE]) * scale).astype(BF16)
        k_ref[:, b0:b0 + LANE] = kvf[:, b0:b0 + LANE].astype(BF16)
        k_ref[:, b0 + LANE:b0 + 2 * LANE] = kr
        v_ref[:, h * LANE:(h + 1) * LANE] = kvf[:, b0 + LANE:b0 + 2 * LANE].astype(BF16)


def _mla_prep(rest, tab, gq, gkv, wuq, wukv, seq):
    n = rest.shape[0]
    tm = min(512, seq)
    scale = (D_NOPE + D_ROPE) ** -0.5 * LOG2E
    kern = functools.partial(_mla_prep_kernel, scale=scale)
    wq = HD * 2 * LANE
    return pl.pallas_call(
        kern,
        grid=(n // tm,),
        in_specs=[
            pl.BlockSpec((tm, Q_LORA), lambda i: (i, R_CQ // Q_LORA)),
            pl.BlockSpec((tm, 512), lambda i: (i, R_CKV // 512)),
            pl.BlockSpec((tm, LANE), lambda i: (i, 0)),
            pl.BlockSpec((1, Q_LORA), lambda i: (0, 0)),
            pl.BlockSpec((1, KV_LORA), lambda i: (0, 0)),
            pl.BlockSpec((Q_LORA, wq), lambda i: (0, 0)),
            pl.BlockSpec((KV_LORA, wq), lambda i: (0, 0)),
        ],
        out_specs=[
            pl.BlockSpec((tm, wq), lambda i: (i, 0)),
            pl.BlockSpec((tm, wq), lambda i: (i, 0)),
            pl.BlockSpec((tm, HD * DV_D), lambda i: (i, 0)),
        ],
        out_shape=[
            jax.ShapeDtypeStruct((n, wq), BF16),
            jax.ShapeDtypeStruct((n, wq), BF16),
            jax.ShapeDtypeStruct((n, HD * DV_D), BF16),
        ],
        compiler_params=_params("parallel"),
    )(rest, rest, tab, gq.reshape(1, -1), gkv.reshape(1, -1), wuq, wukv)


def _mla_attn_kernel(q_ref, kt_ref, v_ref, o_ref, m_ref, acc_ref, *, tk, nk):
    ki = pl.program_id(2)

    @pl.when(ki == 0)
    def _():
        m_ref[...] = jnp.full(m_ref.shape, -jnp.inf, F32)
        acc_ref[...] = jnp.zeros(acc_ref.shape, F32)

    ones = jnp.ones((tk, LANE), BF16)
    for h in range(HD):
        q = q_ref[:, h * 2 * LANE:(h + 1) * 2 * LANE]
        kt = kt_ref[h * 2 * LANE:(h + 1) * 2 * LANE, :]
        vx = jnp.concatenate([v_ref[:, h * LANE:(h + 1) * LANE], ones], axis=1)
        s = jnp.dot(q, kt, preferred_element_type=F32)
        _online_softmax_step(s, vx, m_ref, acc_ref, h, tk)

    @pl.when(ki == nk - 1)
    def _():
        for h in range(HD):
            a = acc_ref[h]
            o_ref[:, h * LANE:(h + 1) * LANE] = (a[:, :LANE] / a[:, LANE:]).astype(o_ref.dtype)


def _mla_attn(q, k, v, n_seq, seq):
    n = q.shape[0]
    tq = min(1024, seq)
    tk = min(1024, seq)
    nq, nk = seq // tq, seq // tk
    wq = HD * 2 * LANE
    kern = functools.partial(_mla_attn_kernel, tk=tk, nk=nk)
    return pl.pallas_call(
        kern,
        grid=(n_seq, nq, nk),
        in_specs=[
            pl.BlockSpec((tq, wq), lambda b, qi, ki: (b * nq + qi, 0)),
            pl.BlockSpec((wq, tk), lambda b, qi, ki: (0, b * nk + ki)),
            pl.BlockSpec((tk, HD * DV_D), lambda b, qi, ki: (b * nk + ki, 0)),
        ],
        out_specs=pl.BlockSpec((tq, HD * DV_D), lambda b, qi, ki: (b * nq + qi, 0)),
        out_shape=jax.ShapeDtypeStruct((n, HD * DV_D), BF16),
        scratch_shapes=[
            pltpu.VMEM((HD, tq, LANE), F32),
            pltpu.VMEM((HD, tq, 2 * LANE), F32),
        ],
        compiler_params=_params("parallel", "parallel", "arbitrary"),
    )(q, k.T, v)


def _sgu_kernel(uv_ref, g_ref, b_ref, ws_ref, bs_ref, o_ref, *, tm):
    z = jax.nn.gelu(uv_ref[...].astype(F32))
    u = z[:, :BRANCH_W]
    v = z[:, BRANCH_W:]
    mu = jnp.mean(v, axis=-1, keepdims=True)
    vc = v - mu
    var = jnp.mean(vc * vc, axis=-1, keepdims=True)
    vn = (vc * lax.rsqrt(var + EPS) * g_ref[...] + b_ref[...]).astype(BF16)
    for c in range(tm // SGU_CHUNK):
        r0 = c * SGU_CHUNK
        for g in range(SGU_GROUPS):
            c0 = g * LANE
            vm = jnp.dot(ws_ref[g], vn[r0:r0 + SGU_CHUNK, c0:c0 + LANE], preferred_element_type=F32)
            vm = vm + bs_ref[:, c0:c0 + LANE]
            o_ref[r0:r0 + SGU_CHUNK, c0:c0 + LANE] = (u[r0:r0 + SGU_CHUNK, c0:c0 + LANE] * vm).astype(o_ref.dtype)


def _sgu(rest, ln_g, ln_b, ws, bs, seq):
    n = rest.shape[0]
    tm = min(512, seq)
    kern = functools.partial(_sgu_kernel, tm=tm)
    return pl.pallas_call(
        kern,
        grid=(n // tm,),
        in_specs=[
            pl.BlockSpec((tm, 2 * BRANCH_W), lambda i: (i, R_UV // (2 * BRANCH_W))),
            pl.BlockSpec((1, BRANCH_W), lambda i: (0, 0)),
            pl.BlockSpec((1, BRANCH_W), lambda i: (0, 0)),
            pl.BlockSpec((SGU_GROUPS, SGU_CHUNK, SGU_CHUNK), lambda i: (0, 0, 0)),
            pl.BlockSpec((SGU_CHUNK, BRANCH_W), lambda i: (0, 0)),
        ],
        out_specs=pl.BlockSpec((tm, BRANCH_W), lambda i: (i, 0)),
        out_shape=jax.ShapeDtypeStruct((n, BRANCH_W), BF16),
        compiler_params=_params("parallel"),
    )(rest, ln_g.reshape(1, -1), ln_b.reshape(1, -1), ws, bs)


def _lru_kernel(xf_ref, xfp_ref, xfn_ref, xr_ref, xrp_ref, xrn_ref, cw_ref, cb_ref, wr_ref, br_ref, wi_ref,
                bi_ref, lam_ref, hf_ref, hr_ref, af_ref, uf_ref, ar_ref, ur_ref, hc_ref, *, tt, tps):
    j = pl.program_id(0) % tps
    rowi = lax.broadcasted_iota(I32, (tt, LRU_W), 0)

    def gates(x_ref, xp_ref, xn_ref, at_start, at_end, d, a_ref, u_ref):
        x = x_ref[...].astype(F32)
        prev = jnp.where(at_start, 0.0, xp_ref[...].astype(F32)[HALO - 1:HALO, :])
        nxt = jnp.where(at_end, 0.0, xn_ref[...].astype(F32)[0:2, :])
        xm1 = jnp.where(rowi == 0, prev, pltpu.roll(x, 1, 0))
        xp1 = jnp.where(rowi == tt - 1, nxt[0:1, :], pltpu.roll(x, tt - 1, 0))
        xp2 = jnp.where(rowi == tt - 2, nxt[0:1, :],
                        jnp.where(rowi == tt - 1, nxt[1:2, :], pltpu.roll(x, tt - 2, 0)))
        xc = xm1 * cw_ref[0:1, :] + x * cw_ref[1:2, :] + xp1 * cw_ref[2:3, :] + xp2 * cw_ref[3:4, :] + cb_ref[...]
        xcb = xc.astype(BF16)

        def blockdiag(w_ref):
            return jnp.concatenate(
                [jnp.dot(xcb[:, g * LRU_BW:(g + 1) * LRU_BW], w_ref[d, g], preferred_element_type=F32)
                 for g in range(LRU_BLOCKS)], axis=1)

        r = jax.nn.sigmoid(blockdiag(wr_ref) + br_ref[d:d + 1, :])
        ig = jax.nn.sigmoid(blockdiag(wi_ref) + bi_ref[d:d + 1, :])
        log_a = (-LRU_C * r) * jax.nn.softplus(-lam_ref[d:d + 1, :])
        a = jnp.exp(log_a)
        u = jnp.sqrt(jnp.tanh(-log_a) * (1.0 + a * a)) * (ig * xc)
        a_ref[...] = a
        u_ref[...] = u

    gates(xf_ref, xfp_ref, xfn_ref, j == 0, j == tps - 1, 0, af_ref, uf_ref)
    gates(xr_ref, xrp_ref, xrn_ref, j == tps - 1, j == 0, 1, ar_ref, ur_ref)

    @pl.when(j == 0)
    def _():
        hc_ref[...] = jnp.zeros(hc_ref.shape, F32)

    ng = tt // SUBLANE
    sub = lax.broadcasted_iota(I32, (SUBLANE, LRU_W), 0)

    def body(g, carry):
        hf_prev, hr_prev = carry
        r0 = pl.multiple_of(g * SUBLANE, SUBLANE)
        a = af_ref[pl.ds(r0, SUBLANE), :]
        u = uf_ref[pl.ds(r0, SUBLANE), :]
        for s in (1, 2, 4):
            a_s = jnp.where(sub >= s, pltpu.roll(a, s, 0), 1.0)
            u_s = jnp.where(sub >= s, pltpu.roll(u, s, 0), 0.0)
            u = a * u_s + u
            a = a * a_s
        h = a * hf_prev + u
        hf_ref[pl.ds(r0, SUBLANE), :] = h
        hf_new = jnp.broadcast_to(h[SUBLANE - 1:SUBLANE, :], (SUBLANE, LRU_W))

        r1 = pl.multiple_of((ng - 1 - g) * SUBLANE, SUBLANE)
        a = ar_ref[pl.ds(r1, SUBLANE), :]
        u = ur_ref[pl.ds(r1, SUBLANE), :]
        for s in (1, 2, 4):
            a_s = jnp.where(sub < SUBLANE - s, pltpu.roll(a, SUBLANE - s, 0), 1.0)
            u_s = jnp.where(sub < SUBLANE - s, pltpu.roll(u, SUBLANE - s, 0), 0.0)
            u = a * u_s + u
            a = a * a_s
        h = a * hr_prev + u
        hr_ref[pl.ds(r1, SUBLANE), :] = h
        hr_new = jnp.broadcast_to(h[0:1, :], (SUBLANE, LRU_W))
        return hf_new, hr_new

    hf_c, hr_c = lax.fori_loop(0, ng, body, (hc_ref[0], hc_ref[1]))
    hc_ref[0] = hf_c
    hc_ref[1] = hr_c


def _lru(rest, cw, cb, wr, br, wi, bi, lam, seq):
    n = rest.shape[0]
    tt = min(512, seq)
    tps = seq // tt
    nt = n // tt
    t8 = tt // HALO
    last8 = n // HALO - 1
    cblk = R_XC // LRU_W

    def rev(i):
        return (i // tps) * tps + (tps - 1 - i % tps)

    kern = functools.partial(_lru_kernel, tt=tt, tps=tps)
    full = lambda shape: pl.BlockSpec(shape, lambda i: (0,) * len(shape))
    return pl.pallas_call(
        kern,
        grid=(nt,),
        in_specs=[
            pl.BlockSpec((tt, LRU_W), lambda i: (i, cblk)),
            pl.BlockSpec((HALO, LRU_W), lambda i: (jnp.maximum(i * t8 - 1, 0), cblk)),
            pl.BlockSpec((HALO, LRU_W), lambda i: (jnp.minimum((i + 1) * t8, last8), cblk)),
            pl.BlockSpec((tt, LRU_W), lambda i: (rev(i), cblk)),
            pl.BlockSpec((HALO, LRU_W), lambda i: (jnp.maximum(rev(i) * t8 - 1, 0), cblk)),
            pl.BlockSpec((HALO, LRU_W), lambda i: (jnp.minimum((rev(i) + 1) * t8, last8), cblk)),
            full((4, LRU_W)),
            full((1, LRU_W)),
            full((2, LRU_BLOCKS, LRU_BW, LRU_BW)),
            full((2, LRU_W)),
            full((2, LRU_BLOCKS, LRU_BW, LRU_BW)),
            full((2, LRU_W)),
            full((2, LRU_W)),
        ],
        out_specs=[
            pl.BlockSpec((tt, LRU_W), lambda i: (i, 0)),
            pl.BlockSpec((tt, LRU_W), lambda i: (rev(i), 0)),
        ],
        out_shape=[jax.ShapeDtypeStruct((n, LRU_W), F32), jax.ShapeDtypeStruct((n, LRU_W), F32)],
        scratch_shapes=[pltpu.VMEM((tt, LRU_W), F32)] * 4 + [pltpu.VMEM((2, SUBLANE, LRU_W), F32)],
        compiler_params=_params("arbitrary"),
    )(rest, rest, rest, rest, rest, rest, cw, cb.reshape(1, -1), wr, br, wi, bi, lam)


def _merge_kernel(ya_ref, yb_ref, hf_ref, hr_ref, gate_ref, yd_ref, wbr_ref, g0_ref, g1_ref, g2_ref, g3_ref, o_ref):
    yc = (jax.nn.gelu(gate_ref[...].astype(F32)) * (hf_ref[...] + hr_ref[...])).astype(BF16)
    ys = (ya_ref[...], yb_ref[...], yc, yd_ref[...])
    gls = (g0_ref, g1_ref, g2_ref, g3_ref)
    acc = None
    for k in range(N_BRANCH):
        t = jax.nn.sigmoid(gls[k][...].astype(F32)) * jnp.dot(ys[k], wbr_ref[k], preferred_element_type=F32)
        acc = t if acc is None else acc + t
    o_ref[...] = acc.astype(o_ref.dtype)


def _merge(ya, yb, hf, hr, rest, yd, wbr, seq):
    n = ya.shape[0]
    d = wbr.shape[-1]
    tm = min(512, seq)
    tn = 512
    row = lambda w: pl.BlockSpec((tm, w), lambda i, j: (i, 0))

    def gl_spec(k):
        base = (R_GL + k * d) // tn
        return pl.BlockSpec((tm, tn), lambda i, j: (i, base + j))

    return pl.pallas_call(
        _merge_kernel,
        grid=(n // tm, d // tn),
        in_specs=[
            row(BRANCH_W), row(BRANCH_W), row(LRU_W), row(LRU_W),
            pl.BlockSpec((tm, LRU_W), lambda i, j: (i, R_GATE // LRU_W)),
            row(BRANCH_W),
            pl.BlockSpec((N_BRANCH, BRANCH_W, tn), lambda i, j: (0, 0, j)),
            gl_spec(0), gl_spec(1), gl_spec(2), gl_spec(3),
        ],
        out_specs=pl.BlockSpec((tm, tn), lambda i, j: (i, j)),
        out_shape=jax.ShapeDtypeStruct((n, d), BF16),
        compiler_params=_params("parallel", "parallel"),
    )(ya, yb, hf, hr, rest, yd, wbr, rest, rest, rest, rest)


def _outproj_kernel(m_ref, w_ref, x_ref, g_ref, o_ref):
    o_ref[...] = x_ref[...] + g_ref[...] * jnp.dot(m_ref[...], w_ref[...], preferred_element_type=F32)


def _outproj(merged, w, x, g1, seq):
    n, d = x.shape
    tm = min(512, seq)
    tps = seq // tm
    return pl.pallas_call(
        _outproj_kernel,
        grid=(n // tm,),
        in_specs=[
            pl.BlockSpec((tm, d), lambda i: (i, 0)),
            pl.BlockSpec((d, d), lambda i: (0, 0)),
            pl.BlockSpec((tm, d), lambda i: (i, 0)),
            pl.BlockSpec((None, 1, d), lambda i: (i // tps, 0, 0)),
        ],
        out_specs=pl.BlockSpec((tm, d), lambda i: (i, 0)),
        out_shape=jax.ShapeDtypeStruct((n, d), F32),
        compiler_params=_params("parallel"),
    )(merged, w, x, g1)


def _router_kernel(x_ref, g_ref, mod_ref, wh_ref, wl_ref, rb_ref, tri_ref, g2_ref, swg_ref, swu_ref, swd_ref,
                   x1_ref, h2g_ref, e_ref, w_ref, r_ref, cnt_ref, run_ref, *, tm, d):
    @pl.when(pl.program_id(0) == 0)
    def _():
        run_ref[...] = jnp.zeros(run_ref.shape, F32)

    x = x_ref[...]
    y = x * lax.rsqrt(jnp.mean(x * x, axis=-1, keepdims=True) + EPS) * g_ref[...]
    h = y * (1.0 + mod_ref[0:1, :]) + mod_ref[1:2, :]
    hb = h.astype(BF16)
    hs = (jax.nn.silu(jnp.dot(hb, swg_ref[...], preferred_element_type=F32))
          * jnp.dot(hb, swu_ref[...], preferred_element_type=F32)).astype(BF16)
    x1_ref[...] = x + g2_ref[...] * jnp.dot(hs, swd_ref[...], preferred_element_type=F32)
    slabs = d // LANE
    for s in range(slabs):
        h2g_ref[pl.ds(s, tm, stride=slabs), :] = h[:, s * LANE:(s + 1) * LANE]

    hl = (h - hb.astype(F32)).astype(BF16)
    logits = (lax.dot_general(wh_ref[...], hb, NT_DIMS, preferred_element_type=F32)
              + lax.dot_general(wl_ref[...], hb, NT_DIMS, preferred_element_type=F32)
              + lax.dot_general(wh_ref[...], hl, NT_DIMS, preferred_element_type=F32))
    sc = jax.nn.sigmoid(logits)
    sel = sc + rb_ref[...]

    neg = -jnp.inf
    i8 = lax.broadcasted_iota(I32, (GROUP_SZ, tm), 0).astype(F32)
    rows = []
    for g in range(N_GROUPS):
        blk = sel[g * GROUP_SZ:(g + 1) * GROUP_SZ, :]
        m1 = jnp.max(blk, axis=0, keepdims=True)
        i1 = jnp.min(jnp.where(blk == m1, i8, float(GROUP_SZ)), axis=0, keepdims=True)
        m2 = jnp.max(jnp.where(i8 == i1, neg, blk), axis=0, keepdims=True)
        rows.append(m1 + m2)
    gs = jnp.concatenate(rows, axis=0)
    gi8 = lax.broadcasted_iota(I32, (N_GROUPS, tm), 0).astype(F32)
    gsel = jnp.zeros((N_GROUPS, tm), F32)
    for _ in range(TOPK_GROUPS):
        gm = jnp.max(gs, axis=0, keepdims=True)
        gi = jnp.min(jnp.where(gs == gm, gi8, float(N_GROUPS)), axis=0, keepdims=True)
        hit = gi8 == gi
        gsel = jnp.where(hit, 1.0, gsel)
        gs = jnp.where(hit, neg, gs)
    emask = jnp.concatenate(
        [jnp.broadcast_to(gsel[g:g + 1, :], (GROUP_SZ, tm)) for g in range(N_GROUPS)], axis=0)
    selm = jnp.where(emask > 0.0, sel, neg)

    i64 = lax.broadcasted_iota(I32, (N_EXPERTS, tm), 0).astype(F32)
    chosen = jnp.zeros((N_EXPERTS, tm), F32)
    idxs, wts = [], []
    for _ in range(TOP_K):
        mx = jnp.max(selm, axis=0, keepdims=True)
        ix = jnp.min(jnp.where(selm == mx, i64, float(N_EXPERTS)), axis=0, keepdims=True)
        oh = i64 == ix
        wts.append(jnp.sum(jnp.where(oh, sc, 0.0), axis=0, keepdims=True))
        idxs.append(ix)
        selm = jnp.where(oh, neg, selm)
        chosen = jnp.where(oh, 1.0, chosen)
    wk = jnp.concatenate(wts, axis=0)
    w_ref[...] = wk / jnp.sum(wk, axis=0, keepdims=True) * ROUTE_SCALE
    e_ref[...] = jnp.concatenate(idxs, axis=0).astype(I32)

    before = jnp.dot(chosen.astype(BF16), tri_ref[...], preferred_element_type=F32)
    rank = run_ref[...] + before
    r_ref[...] = jnp.concatenate(
        [jnp.sum(jnp.where(i64 == idxs[k], rank, 0.0), axis=0, keepdims=True) for k in range(TOP_K)],
        axis=0).astype(I32)
    run_ref[...] = run_ref[...] + jnp.sum(chosen, axis=1, keepdims=True)
    cnt_ref[...] = jnp.broadcast_to(run_ref[...], cnt_ref.shape).astype(I32)


def _router(x, g, mod, wh, wl, rb, g2, swg, swu, swd, seq):
    n, d = x.shape
    ds_ = swg.shape[1]
    tm = min(512, seq)
    tps = seq // tm
    slabs = d // LANE
    tri = jnp.triu(jnp.ones((tm, tm), F32), 1).astype(BF16)
    kern = functools.partial(_router_kernel, tm=tm, d=d)
    return pl.pallas_call(
        kern,
        grid=(n // tm,),
        in_specs=[
            pl.BlockSpec((tm, d), lambda i: (i, 0)),
            pl.BlockSpec((1, d), lambda i: (0, 0)),
            pl.BlockSpec((None, 2, d), lambda i: (i // tps, 0, 0)),
            pl.BlockSpec((N_EXPERTS, d), lambda i: (0, 0)),
            pl.BlockSpec((N_EXPERTS, d), lambda i: (0, 0)),
            pl.BlockSpec((N_EXPERTS, 1), lambda i: (0, 0)),
            pl.BlockSpec((tm, tm), lambda i: (0, 0)),
            pl.BlockSpec((None, 1, d), lambda i: (i // tps, 0, 0)),
            pl.BlockSpec((d, ds_), lambda i: (0, 0)),
            pl.BlockSpec((d, ds_), lambda i: (0, 0)),
            pl.BlockSpec((ds_, d), lambda i: (0, 0)),
        ],
        out_specs=[
            pl.BlockSpec((tm, d), lambda i: (i, 0)),
            pl.BlockSpec((tm * slabs, LANE), lambda i: (i, 0)),
            pl.BlockSpec((TOP_K, tm), lambda i: (0, i)),
            pl.BlockSpec((TOP_K, tm), lambda i: (0, i)),
            pl.BlockSpec((TOP_K, tm), lambda i: (0, i)),
            pl.BlockSpec((N_EXPERTS, LANE), lambda i: (0, 0)),
        ],
        out_shape=[
            jax.ShapeDtypeStruct((n, d), F32),
            jax.ShapeDtypeStruct((n * slabs, LANE), F32),
            jax.ShapeDtypeStruct((TOP_K, n), I32),
            jax.ShapeDtypeStruct((TOP_K, n), F32),
            jax.ShapeDtypeStruct((TOP_K, n), I32),
            jax.ShapeDtypeStruct((N_EXPERTS, LANE), I32),
        ],
        scratch_shapes=[pltpu.VMEM((N_EXPERTS, 1), F32)],
        compiler_params=_params("arbitrary"),
    )(x, g.reshape(1, d), mod, wh, wl, rb.reshape(N_EXPERTS, 1), tri, g2, swg, swu, swd)


def _dispatch_kernel(dest_ref, pend_ref, pc_ref, h_ref, xs_ref, zero_ref, sem, zsem, *, tm, n, slabs):
    i = pl.program_id(0)
    rows_blk = MOE_BLK * slabs

    def tail_copy(e):
        start = pl.multiple_of((pend_ref[e] - MOE_BLK) * slabs, rows_blk)
        return pltpu.make_async_copy(zero_ref, xs_ref.at[pl.ds(start, rows_blk), :], zsem)

    @pl.when(i == 0)
    def _():
        zero_ref[...] = jnp.zeros(zero_ref.shape, F32)

        def zstart(e, c):
            @pl.when(pc_ref[e] > 0)
            def _():
                tail_copy(e).start()
            return c

        def zwait(e, c):
            @pl.when(pc_ref[e] > 0)
            def _():
                tail_copy(e).wait()
            return c

        lax.fori_loop(0, N_EXPERTS, zstart, 0)
        lax.fori_loop(0, N_EXPERTS, zwait, 0)

    def row_copy(r, k):
        src = h_ref.at[pl.ds(pl.multiple_of(r * slabs, slabs), slabs), :]
        dst_row = dest_ref[k * n + i * tm + r]
        dst = xs_ref.at[pl.ds(pl.multiple_of(dst_row * slabs, slabs), slabs), :]
        return pltpu.make_async_copy(src, dst, sem)

    def start(r, c):
        for k in range(TOP_K):
            row_copy(r, k).start(priority=k % 2)
        return c

    lax.fori_loop(0, tm, start, 0)
    for _ in range(TOP_K):
        pltpu.make_async_copy(h_ref, xs_ref.at[pl.ds(0, tm * slabs), :], sem).wait()


def _dispatch(dest, pend, pc, h2g, n_rows, seq):
    n = dest.shape[0] // TOP_K
    slabs = h2g.shape[0] // n
    tm = min(256, seq)
    kern = functools.partial(_dispatch_kernel, tm=tm, n=n, slabs=slabs)
    return pl.pallas_call(
        kern,
        grid_spec=pltpu.PrefetchScalarGridSpec(
            num_scalar_prefetch=3,
            grid=(n // tm,),
            in_specs=[pl.BlockSpec((tm * slabs, LANE), lambda i, *_: (i, 0))],
            out_specs=pl.BlockSpec(memory_space=pl.ANY),
            scratch_shapes=[
                pltpu.VMEM((MOE_BLK * slabs, LANE), F32),
                pltpu.SemaphoreType.DMA(()),
                pltpu.SemaphoreType.DMA(()),
            ],
        ),
        out_shape=jax.ShapeDtypeStruct((n_rows * slabs, LANE), F32),
        compiler_params=_params("arbitrary"),
    )(dest, pend, pc, h2g)


def _gmm_kernel(be_ref, nu_ref, x_ref, wg_ref, wu_ref, wd_ref, o_ref, *, slabs):
    @pl.when(pl.program_id(0) < nu_ref[0])
    def _():
        x = jnp.concatenate(
            [x_ref[pl.ds(s, MOE_BLK, stride=slabs), :].astype(BF16) for s in range(slabs)], axis=1)
        hg = jnp.dot(x, wg_ref[...], preferred_element_type=F32)
        hu = jnp.dot(x, wu_ref[...], preferred_element_type=F32)
        hb = (jax.nn.silu(hg) * hu).astype(BF16)
        per = 4
        for c in range(slabs // per):
            y = jnp.dot(hb, wd_ref[:, c * per * LANE:(c + 1) * per * LANE], preferred_element_type=F32)
            for j in range(per):
                o_ref[pl.ds(c * per + j, MOE_BLK, stride=slabs), :] = y[:, j * LANE:(j + 1) * LANE]


def _gmm(blk_e, nused, xs, wg, wu, wd, layer):
    _, _, d, de = wg.shape
    slabs = d // LANE
    nblk = xs.shape[0] // (MOE_BLK * slabs)
    kern = functools.partial(_gmm_kernel, slabs=slabs)

    def blk(b, be, nu):
        return jnp.minimum(b, nu[0] - 1)

    return pl.pallas_call(
        kern,
        grid_spec=pltpu.PrefetchScalarGridSpec(
            num_scalar_prefetch=2,
            grid=(nblk,),
            in_specs=[
                pl.BlockSpec((MOE_BLK * slabs, LANE), lambda b, be, nu: (blk(b, be, nu), 0)),
                pl.BlockSpec((None, None, d, de), lambda b, be, nu: (layer, be[blk(b, be, nu)], 0, 0)),
                pl.BlockSpec((None, None, d, de), lambda b, be, nu: (layer, be[blk(b, be, nu)], 0, 0)),
                pl.BlockSpec((None, None, de, d), lambda b, be, nu: (layer, be[blk(b, be, nu)], 0, 0)),
            ],
            out_specs=pl.BlockSpec((MOE_BLK * slabs, LANE), lambda b, be, nu: (blk(b, be, nu), 0)),
        ),
        out_shape=jax.ShapeDtypeStruct(xs.shape, F32),
        compiler_params=_params("arbitrary"),
    )(blk_e, nused, xs, wg, wu, wd)


def _combine_kernel(dest_ref, wt_ref, x_ref, g2_ref, fg_ref, ys_ref, o_ref, buf_ref, wb_ref, sem,
                    *, tm, n, slabs, nt, final):
    i = pl.program_id(0)
    slot = i % 2
    rows = tm * COMBINE_PITCH

    def gather(tile, sl):
        def start(r, c):
            for k in range(TOP_K):
                src_row = dest_ref[k * n + tile * tm + r]
                src = ys_ref.at[pl.ds(pl.multiple_of(src_row * slabs, slabs), slabs), :]
                dst = buf_ref.at[sl, pl.ds(pl.multiple_of(k * rows + r * COMBINE_PITCH, SUBLANE), slabs), :]
                pltpu.make_async_copy(src, dst, sem.at[sl]).start(priority=k % 2)
            return c

        lax.fori_loop(0, tm, start, 0)

    @pl.when(i == 0)
    def _():
        gather(0, 0)

    @pl.when(i + 1 < nt)
    def _():
        gather(i + 1, 1 - slot)

    moved = TOP_K * tm * slabs
    pltpu.make_async_copy(ys_ref.at[pl.ds(0, moved), :], buf_ref.at[slot, pl.ds(0, moved), :], sem.at[slot]).wait()

    wt = wt_ref[...]
    for k in range(TOP_K):
        wb_ref[k] = jnp.broadcast_to(wt[:, k:k + 1], (tm, LANE))
    for s in range(slabs):
        c0 = s * LANE
        routed = None
        for k in range(TOP_K):
            t = wb_ref[k] * buf_ref[slot, pl.ds(k * rows + s, tm, stride=COMBINE_PITCH), :]
            routed = t if routed is None else routed + t
        o_ref[:, c0:c0 + LANE] = x_ref[:, c0:c0 + LANE] + g2_ref[:, c0:c0 + LANE] * routed
    if final:
        o = o_ref[...]
        o_ref[...] = o * lax.rsqrt(jnp.mean(o * o, axis=-1, keepdims=True) + EPS) * fg_ref[...]


def _combine(dest, wt, x, g2, ys, final_g, final, seq):
    n, d = x.shape
    slabs = d // LANE
    tm = min(128, seq)
    tps = seq // tm
    assert COMBINE_PITCH >= slabs
    kern = functools.partial(_combine_kernel, tm=tm, n=n, slabs=slabs, nt=n // tm, final=final)
    return pl.pallas_call(
        kern,
        grid_spec=pltpu.PrefetchScalarGridSpec(
            num_scalar_prefetch=1,
            grid=(n // tm,),
            in_specs=[
                pl.BlockSpec((tm, TOP_K), lambda i, *_: (i, 0)),
                pl.BlockSpec((tm, d), lambda i, *_: (i, 0)),
                pl.BlockSpec((None, 1, d), lambda i, *_: (i // tps, 0, 0)),
                pl.BlockSpec((1, d), lambda i, *_: (0, 0)),
                pl.BlockSpec(memory_space=pl.ANY),
            ],
            out_specs=pl.BlockSpec((tm, d), lambda i, *_: (i, 0)),
            scratch_shapes=[
                pltpu.VMEM((2, TOP_K * tm * COMBINE_PITCH, LANE), F32),
                pltpu.VMEM((TOP_K, tm, LANE), F32),
                pltpu.SemaphoreType.DMA((2,)),
            ],
        ),
        out_shape=jax.ShapeDtypeStruct((n, d), F32),
        compiler_params=_params("arbitrary"),
    )(dest, wt, x, g2, final_g.reshape(1, d), ys)


def _swap_halves(w):
    half = w.shape[-1] // 2
    return jnp.concatenate([w[..., half:], w[..., :half]], axis=-1)


def _prep_w_in(w):
    depth, d, _ = w.shape
    sizes = [HA * 2 * DHA, HA * 2 * DHA, HA * DVA, 2 * BRANCH_W, 2 * LRU_W, Q_LORA, KV_LORA, D_ROPE, N_BRANCH * d]
    offs = np.cumsum([0] + sizes)
    qa, ka, va, uv, xg, cq, ckv, kr, gl = [w[:, :, offs[i]:offs[i + 1]] for i in range(len(sizes))]
    pad = jnp.zeros((depth, d, 512 - KV_LORA - 2 * D_ROPE), w.dtype)
    w_att = jnp.concatenate([qa * (DHA ** -0.5 * LOG2E), ka, va], axis=2).astype(BF16)
    w_rest = jnp.concatenate([uv, xg, cq, ckv, kr, _swap_halves(kr), pad, gl], axis=2).astype(BF16)
    return w_att, w_rest


def _prep_layer(l, p):
    wuq =p['w_uq'][l].reshape(Q_LORA, HD, D_NOPE + D_ROPE)
    wuq_r = wuq[:, :, D_NOPE:]
    wuq_ext = jnp.concatenate([wuq[:, :, :D_NOPE], wuq_r, _swap_halves(wuq_r)], axis=-1)
    wuq_ext = wuq_ext.reshape(Q_LORA, HD * 2 * LANE).astype(BF16)

    lam_init = 0.8 - 0.6 * math.exp(-0.3 * l)
    lam = (jnp.exp(jnp.sum(p['lam_q1'][l] * p['lam_k1'][l])) - jnp.exp(jnp.sum(p['lam_q2'][l] * p['lam_k2'][l]))
           + lam_init)
    slopes = jnp.asarray(2.0 ** (-8.0 * np.arange(1, HA + 1) / HA) * LOG2E, F32)
    par = jnp.concatenate([slopes, lam.reshape(1), jnp.full((1,), 1.0 - lam_init, F32), jnp.zeros((2,), F32)])

    rw = p['router_w'][l].T
    rw_hi = rw.astype(BF16)
    rw_lo = (rw - rw_hi.astype(F32)).astype(BF16)
    bs = jnp.repeat(p['sgu_b'][l].T, LANE, axis=1)
    return dict(
        wuq=wuq_ext, wukv=p['w_ukv'][l].astype(BF16), par=par,
        sgu_w=p['sgu_w'][l].astype(BF16), sgu_bs=bs,
        rg_wr=p['rg_wr'][l].astype(BF16), rg_wi=p['rg_wi'][l].astype(BF16),
        w_branch=p['w_branch'][l].astype(BF16), w_out=p['w_out'][l].astype(BF16),
        rw_hi=rw_hi, rw_lo=rw_lo,
        sh_wg=p['sh_wg'][l].astype(BF16), sh_wu=p['sh_wu'][l].astype(BF16), sh_wd=p['sh_wd'][l].astype(BF16),
    )


def _rope_table(n_seq, seq):
    inv_freq = ROPE_THETA ** (-jnp.arange(0, D_ROPE, 2, dtype=F32) / D_ROPE)
    ang = jnp.arange(seq, dtype=F32)[:, None] * inv_freq[None, :]
    cos, sin = jnp.cos(ang), jnp.sin(ang)
    tab = jnp.concatenate([cos, cos, -sin, sin], axis=1)
    return jnp.tile(tab, (n_seq, 1))


def _trunk(x3, mods, p, preps):
    n_seq, seq, d = x3.shape
    n = n_seq * seq
    x = x3.reshape(n, d)
    tab = _rope_table(n_seq, seq)
    depth = len(preps)
    for l in range(depth):
        w = preps[l]
        mod = mods[l]
        mod1 = jnp.stack([mod[:, 1], mod[:, 0]], axis=1)
        mod2 = jnp.stack([mod[:, 4], mod[:, 3]], axis=1)
        g1 = mod[:, 2:3]
        g2 = mod[:, 5:6]

        att = _inproj(x, p['norm1_g'][l], mod1, p['w_att'], l, BF16, seq)
        rest = _inproj(x, p['norm1_g'][l], mod1, p['w_rest'], l, BF16, seq)
        ya = _diff_attn(att, w['par'], p['subln_g'][l], n_seq, seq)
        yb = _sgu(rest, p['sgu_ln_g'][l], p['sgu_ln_b'][l], w['sgu_w'], w['sgu_bs'], seq)
        hf, hr = _lru(rest, p['conv_w'][l], p['conv_b'][l], w['rg_wr'], p['rg_br'][l], w['rg_wi'],
                      p['rg_bi'][l], p['rg_lam'][l], seq)
        q, k, v = _mla_prep(rest, tab, p['q_norm_g'][l], p['kv_norm_g'][l], w['wuq'], w['wukv'], seq)
        yd = _mla_attn(q, k, v, n_seq, seq)
        merged = _merge(ya, yb, hf, hr, rest, yd, w['w_branch'], seq)
        x = _outproj(merged, w['w_out'], x, g1, seq)

        x, h2g, eidx, wgt, rank, cnt = _router(x, p['norm2_g'][l], mod2, w['rw_hi'], w['rw_lo'],
                                               p['router_bias'][l], g2, w['sh_wg'], w['sh_wu'], w['sh_wd'], seq)
        counts = cnt[:, 0]
        pc = ((counts + MOE_BLK - 1) // MOE_BLK) * MOE_BLK
        pend = jnp.cumsum(pc).astype(I32)
        pstart = pend - pc
        eid = jnp.arange(N_EXPERTS, dtype=I32)
        dest = (jnp.sum(jnp.where(eidx[..., None] == eid, pstart, 0), axis=-1) + rank).reshape(-1).astype(I32)
        nblk = n * TOP_K // MOE_BLK + N_EXPERTS
        blk_row = jnp.arange(nblk, dtype=I32)[:, None] * MOE_BLK
        blk_e = jnp.minimum(jnp.sum((pend[None, :] <= blk_row).astype(I32), axis=1), N_EXPERTS - 1)
        nused = (pend[-1:] // MOE_BLK).astype(I32)
        xs = _dispatch(dest, pend, pc.astype(I32), h2g, nblk * MOE_BLK, seq)
        ys = _gmm(blk_e, nused, xs, p['exp_wg_bf16'], p['exp_wu_bf16'], p['exp_wd_bf16'], l)
        x = _combine(dest, wgt.T, x, g2, ys, p['final_g'], l == depth - 1, seq)
    return x.reshape(n_seq, seq, d)


def kernel(x_prompt, x_sample, c_prompt, c_sample, ada_w, ada_b, norm1_g, norm2_g, w_in, lam_q1, lam_k1, lam_q2, lam_k2, subln_g, sgu_ln_g, sgu_ln_b, sgu_w, sgu_b, conv_w, conv_b, rg_wr, rg_br, rg_wi, rg_bi, rg_lam, q_norm_g, kv_norm_g, w_uq, w_ukv, w_branch, w_out, router_w, router_bias, exp_wg, exp_wu, exp_wd, sh_wg, sh_wu, sh_wd, final_g):
    p = dict(ada_w=ada_w, ada_b=ada_b, norm1_g=norm1_g, norm2_g=norm2_g, w_in=w_in,
             lam_q1=lam_q1, lam_k1=lam_k1, lam_q2=lam_q2, lam_k2=lam_k2, subln_g=subln_g,
             sgu_ln_g=sgu_ln_g, sgu_ln_b=sgu_ln_b, sgu_w=sgu_w, sgu_b=sgu_b,
             conv_w=conv_w, conv_b=conv_b, rg_wr=rg_wr, rg_br=rg_br, rg_wi=rg_wi, rg_bi=rg_bi,
             rg_lam=rg_lam, q_norm_g=q_norm_g, kv_norm_g=kv_norm_g, w_uq=w_uq, w_ukv=w_ukv,
             w_branch=w_branch, w_out=w_out, router_w=router_w, router_bias=router_bias,
             exp_wg=exp_wg, exp_wu=exp_wu, exp_wd=exp_wd, sh_wg=sh_wg, sh_wu=sh_wu, sh_wd=sh_wd,
             final_g=final_g)
    depth, d, _ = ada_w.shape
    bp, bs = x_prompt.shape[0], x_sample.shape[0]
    assert bp + bs <= 8
    c8 = jnp.concatenate([c_prompt, c_sample, jnp.zeros((8 - bp - bs, d), F32)], axis=0)
    mod = _ada_mod(c8, ada_w, ada_b).reshape(depth, 8, 6, d)
    p['w_att'], p['w_rest'] = _prep_w_in(w_in)
    p.update(exp_wg_bf16=exp_wg.astype(BF16), exp_wu_bf16=exp_wu.astype(BF16), exp_wd_bf16=exp_wd.astype(BF16))
    preps = [_prep_layer(l, p) for l in range(depth)]
    y_prompt = _trunk(x_prompt, mod[:, :bp], p, preps)
    y_sample = _trunk(x_sample, mod[:, bp:bp + bs], p, preps)
    return (y_prompt, y_sample)
```

```python
import functools
import math

import numpy as np
import jax
import jax.numpy as jnp
from jax import lax
from jax.experimental import pallas as pl
from jax.experimental.pallas import tpu as pltpu

F32 = jnp.float32
BF16 = jnp.bfloat16
I32 = jnp.int32

EPS = 1e-6
LOG2E = math.log2(math.e)
LANE = 128
SUBLANE = 8
VMEM_LIMIT = 48 * 1024 * 1024

HA = 4
DHA = 64
DVA = 2 * DHA
BRANCH_W = 512
SGU_CHUNK = 128
SGU_GROUPS = 4
LRU_W = 512
LRU_BLOCKS = 4
LRU_BW = LRU_W // LRU_BLOCKS
LRU_C = 8.0
HALO = 16
HD = 4
Q_LORA = 512
KV_LORA = 256
D_NOPE = 128
D_ROPE = 64
DV_D = 128
ROPE_THETA = 10000.0
N_EXPERTS = 64
TOP_K = 8
N_GROUPS = 8
TOPK_GROUPS = 4
GROUP_SZ = N_EXPERTS // N_GROUPS
ROUTE_SCALE = 2.5
MOE_BLK = 512
N_BRANCH = 4
COMBINE_PITCH = 24

R_UV = 0
R_XC = 1024
R_GATE = 1536
R_CQ = 2048
R_CKV = 2560
R_GL = 3072

NT_DIMS = (((1,), (1,)), ((), ()))


def _params(*sem):
    return pltpu.CompilerParams(dimension_semantics=sem, vmem_limit_bytes=VMEM_LIMIT)


def _ada_kernel(c_ref, w_ref, b_ref, o_ref):
    a = jax.nn.silu(c_ref[...]).astype(BF16)
    o_ref[...] = jnp.dot(a, w_ref[...].astype(BF16), preferred_element_type=F32) + b_ref[...]


def _ada_mod(c8, ada_w, ada_b):
    depth, d, n = ada_w.shape
    tn = 1024
    return pl.pallas_call(
        _ada_kernel,
        grid=(depth, n // tn),
        in_specs=[
            pl.BlockSpec((8, d), lambda l, j: (0, 0)),
            pl.BlockSpec((None, d, tn), lambda l, j: (l, 0, j)),
            pl.BlockSpec((None, 1, tn), lambda l, j: (l, 0, j)),
        ],
        out_specs=pl.BlockSpec((None, 8, tn), lambda l, j: (l, 0, j)),
        out_shape=jax.ShapeDtypeStruct((depth, 8, n), F32),
        compiler_params=_params("parallel", "parallel"),
    )(c8, ada_w, ada_b.reshape(depth, 1, n))


def _inproj_kernel(x_ref, g_ref, mod_ref, w_ref, o_ref, h_ref):
    @pl.when(pl.program_id(1) == 0)
    def _():
        x = x_ref[...]
        y = x * lax.rsqrt(jnp.mean(x * x, axis=-1, keepdims=True) + EPS) * g_ref[...]
        h_ref[...] = (y * (1.0 + mod_ref[0:1, :]) + mod_ref[1:2, :]).astype(BF16)

    o_ref[...] = jnp.dot(h_ref[...], w_ref[...], preferred_element_type=F32).astype(o_ref.dtype)


def _inproj(x, g, mod, w, layer, out_dtype, seq):
    n, d = x.shape
    nc = w.shape[2]
    tm = min(1024, seq)
    tn = 1024 if nc % 1024 == 0 else 512
    tps = seq // tm
    return pl.pallas_call(
        _inproj_kernel,
        grid=(n // tm, nc // tn),
        in_specs=[
            pl.BlockSpec((tm, d), lambda i, j: (i, 0)),
            pl.BlockSpec((1, d), lambda i, j: (0, 0)),
            pl.BlockSpec((None, 2, d), lambda i, j: (i // tps, 0, 0)),
            pl.BlockSpec((None, d, tn), lambda i, j: (layer, 0, j)),
        ],
        out_specs=pl.BlockSpec((tm, tn), lambda i, j: (i, j)),
        out_shape=jax.ShapeDtypeStruct((n, nc), out_dtype),
        scratch_shapes=[pltpu.VMEM((tm, d), BF16)],
        compiler_params=_params("parallel", "arbitrary"),
    )(x, g.reshape(1, d), mod, w)


def _online_softmax_step(s, vx, m_ref, acc_ref, idx, tk):
    m_prev = m_ref[idx]
    m_new = jnp.maximum(m_prev, jnp.max(s, axis=-1, keepdims=True))
    alpha = jnp.exp2(m_prev - m_new)
    p = jnp.exp2(s - jnp.concatenate([m_new] * (tk // LANE), axis=1))
    pv = jnp.dot(p.astype(BF16), vx, preferred_element_type=F32)
    acc_ref[idx] = jnp.concatenate([alpha, alpha], axis=1) * acc_ref[idx] + pv
    m_ref[idx] = m_new


def _diff_attn_kernel(par_ref, q_ref, k_ref, v_ref, g_ref, o_ref, m_ref, acc_ref, *, tq, tk, nk):
    qi = pl.program_id(1)
    ki = pl.program_id(2)

    @pl.when(ki == 0)
    def _():
        m_ref[...] = jnp.full(m_ref.shape, -jnp.inf, F32)
        acc_ref[...] = jnp.zeros(acc_ref.shape, F32)

    lane = lax.broadcasted_iota(I32, (tq, LANE), 1)
    row = lax.broadcasted_iota(I32, (tq, tk), 0)
    col = lax.broadcasted_iota(I32, (tq, tk), 1)
    dist = jnp.abs(row - col + (qi * tq - ki * tk)).astype(F32)
    ones = jnp.ones((tk, LANE), BF16)
    for h in range(HA):
        q = q_ref[:, h * LANE:(h + 1) * LANE]
        k = k_ref[:, h * LANE:(h + 1) * LANE]
        vx = jnp.concatenate([v_ref[:, h * LANE:(h + 1) * LANE], ones], axis=1)
        bias = dist * par_ref[h]
        zero = jnp.zeros_like(q)
        for mi, qm in enumerate((jnp.where(lane < DHA, q, zero), jnp.where(lane < DHA, zero, q))):
            s = lax.dot_general(qm, k, NT_DIMS, preferred_element_type=F32) - bias
            _online_softmax_step(s, vx, m_ref, acc_ref, 2 * h + mi, tk)

    @pl.when(ki == nk - 1)
    def _():
        lam = par_ref[HA]
        for h in range(HA):
            a1 = acc_ref[2 * h]
            a2 = acc_ref[2 * h + 1]
            o = a1[:, :LANE] / a1[:, LANE:] - lam * (a2[:, :LANE] / a2[:, LANE:])
            y = o * lax.rsqrt(jnp.mean(o * o, axis=-1, keepdims=True) + EPS) * g_ref[...]
            o_ref[:, h * LANE:(h + 1) * LANE] = (y * par_ref[HA + 1]).astype(o_ref.dtype)


def _diff_attn(att, par, subln_g, n_seq, seq):
    n = att.shape[0]
    tq = min(1024, seq)
    tk = min(512, seq)
    nq, nk = seq // tq, seq // tk
    w = HA * LANE
    kern = functools.partial(_diff_attn_kernel, tq=tq, tk=tk, nk=nk)
    return pl.pallas_call(
        kern,
        grid=(n_seq, nq, nk),
        in_specs=[
            pl.BlockSpec(memory_space=pltpu.SMEM),
            pl.BlockSpec((tq, w), lambda b, qi, ki: (b * nq + qi, 0)),
            pl.BlockSpec((tk, w), lambda b, qi, ki: (b * nk + ki, 1)),
            pl.BlockSpec((tk, w), lambda b, qi, ki: (b * nk + ki, 2)),
            pl.BlockSpec((1, DVA), lambda b, qi, ki: (0, 0)),
        ],
        out_specs=pl.BlockSpec((tq, w), lambda b, qi, ki: (b * nq + qi, 0)),
        out_shape=jax.ShapeDtypeStruct((n, HA * DVA), BF16),
        scratch_shapes=[
            pltpu.VMEM((2 * HA, tq, LANE), F32),
            pltpu.VMEM((2 * HA, tq, 2 * LANE), F32),
        ],
        compiler_params=_params("parallel", "parallel", "arbitrary"),
    )(par, att, att, att, subln_g.reshape(1, DVA))


def _mla_prep_kernel(cq_ref, ckv_ref, tab_ref, gq_ref, gkv_ref, wuq_ref, wukv_ref, q_ref, k_ref, v_ref, *, scale):
    cq = cq_ref[...].astype(F32)
    cqn = (cq * lax.rsqrt(jnp.mean(cq * cq, axis=-1, keepdims=True) + EPS) * gq_ref[...]).astype(BF16)
    c = ckv_ref[...].astype(F32)
    ckv = c[:, :KV_LORA]
    ckvn = (ckv * lax.rsqrt(jnp.mean(ckv * ckv, axis=-1, keepdims=True) + EPS) * gkv_ref[...]).astype(BF16)
    tab = tab_ref[...]
    lane = lax.broadcasted_iota(I32, tab.shape, 1)

    def rope(pair):
        pr = pair * tab
        return jnp.where(lane < D_ROPE, pr + pltpu.roll(pr, D_ROPE, 1), 0.0)

    kr = rope(c[:, KV_LORA:KV_LORA + LANE]).astype(BF16)
    qf = jnp.dot(cqn, wuq_ref[...], preferred_element_type=F32)
    kvf = jnp.dot(ckvn, wukv_ref[...], preferred_element_type=F32)
    for h in range(HD):
        b0 = h * 2 * LANE
        q_ref[:, b0:b0 + LANE] = (qf[:, b0:b0 + LANE] * scale).astype(BF16)
        q_ref[:, b0 + LANE:b0 + 2 * LANE] = (rope(qf[:, b0 + LANE:b0 + 2 * LANE]) * scale).astype(BF16)
        k_ref[:, b0:b0 + LANE] = kvf[:, b0:b0 + LANE].astype(BF16)
        k_ref[:, b0 + LANE:b0 + 2 * LANE] = kr
        v_ref[:, h * LANE:(h + 1) * LANE] = kvf[:, b0 + LANE:b0 + 2 * LANE].astype(BF16)


def _mla_prep(rest, tab, gq, gkv, wuq, wukv, seq):
    n = rest.shape[0]
    tm = min(512, seq)
    scale = (D_NOPE + D_ROPE) ** -0.5 * LOG2E
    kern = functools.partial(_mla_prep_kernel, scale=scale)
    wq = HD * 2 * LANE
    return pl.pallas_call(
        kern,
        grid=(n // tm,),
        in_specs=[
            pl.BlockSpec((tm, Q_LORA), lambda i: (i, R_CQ // Q_LORA)),
            pl.BlockSpec((tm, 512), lambda i: (i, R_CKV // 512)),
            pl.BlockSpec((tm, LANE), lambda i: (i, 0)),
            pl.BlockSpec((1, Q_LORA), lambda i: (0, 0)),
            pl.BlockSpec((1, KV_LORA), lambda i: (0, 0)),
            pl.BlockSpec((Q_LORA, wq), lambda i: (0, 0)),
            pl.BlockSpec((KV_LORA, wq), lambda i: (0, 0)),
        ],
        out_specs=[
            pl.BlockSpec((tm, wq), lambda i: (i, 0)),
            pl.BlockSpec((tm, wq), lambda i: (i, 0)),
            pl.BlockSpec((tm, HD * DV_D), lambda i: (i, 0)),
        ],
        out_shape=[
            jax.ShapeDtypeStruct((n, wq), BF16),
            jax.ShapeDtypeStruct((n, wq), BF16),
            jax.ShapeDtypeStruct((n, HD * DV_D), BF16),
        ],
        compiler_params=_params("parallel"),
    )(rest, rest, tab, gq.reshape(1, -1), gkv.reshape(1, -1), wuq, wukv)


def _mla_attn_kernel(q_ref, k_ref, v_ref, o_ref, m_ref, acc_ref, *, tk, nk):
    ki = pl.program_id(2)

    @pl.when(ki == 0)
    def _():
        m_ref[...] = jnp.full(m_ref.shape, -jnp.inf, F32)
        acc_ref[...] = jnp.zeros(acc_ref.shape, F32)

    ones = jnp.ones((tk, LANE), BF16)
    for h in range(HD):
        q = q_ref[:, h * 2 * LANE:(h + 1) * 2 * LANE]
        k = k_ref[:, h * 2 * LANE:(h + 1) * 2 * LANE]
        vx = jnp.concatenate([v_ref[:, h * LANE:(h + 1) * LANE], ones], axis=1)
        s = lax.dot_general(q, k, NT_DIMS, preferred_element_type=F32)
        _online_softmax_step(s, vx, m_ref, acc_ref, h, tk)

    @pl.when(ki == nk - 1)
    def _():
        for h in range(HD):
            a = acc_ref[h]
            o_ref[:, h * LANE:(h + 1) * LANE] = (a[:, :LANE] / a[:, LANE:]).astype(o_ref.dtype)


def _mla_attn(q, k, v, n_seq, seq):
    n = q.shape[0]
    tq = min(1024, seq)
    tk = min(1024, seq)
    nq, nk = seq // tq, seq // tk
    wq = HD * 2 * LANE
    kern = functools.partial(_mla_attn_kernel, tk=tk, nk=nk)
    return pl.pallas_call(
        kern,
        grid=(n_seq, nq, nk),
        in_specs=[
            pl.BlockSpec((tq, wq), lambda b, qi, ki: (b * nq + qi, 0)),
            pl.BlockSpec((tk, wq), lambda b, qi, ki: (b * nk + ki, 0)),
            pl.BlockSpec((tk, HD * DV_D), lambda b, qi, ki: (b * nk + ki, 0)),
        ],
        out_specs=pl.BlockSpec((tq, HD * DV_D), lambda b, qi, ki: (b * nq + qi, 0)),
        out_shape=jax.ShapeDtypeStruct((n, HD * DV_D), BF16),
        scratch_shapes=[
            pltpu.VMEM((HD, tq, LANE), F32),
            pltpu.VMEM((HD, tq, 2 * LANE), F32),
        ],
        compiler_params=_params("parallel", "parallel", "arbitrary"),
    )(q, k, v)


def _sgu_kernel(uv_ref, g_ref, b_ref, ws_ref, bs_ref, o_ref, *, tm):
    z = jax.nn.gelu(uv_ref[...].astype(F32))
    u = z[:, :BRANCH_W]
    v = z[:, BRANCH_W:]
    mu = jnp.mean(v, axis=-1, keepdims=True)
    vc = v - mu
    var = jnp.mean(vc * vc, axis=-1, keepdims=True)
    vn = (vc * lax.rsqrt(var + EPS) * g_ref[...] + b_ref[...]).astype(BF16)
    for c in range(tm // SGU_CHUNK):
        r0 = c * SGU_CHUNK
        for g in range(SGU_GROUPS):
            c0 = g * LANE
            vm = jnp.dot(ws_ref[g], vn[r0:r0 + SGU_CHUNK, c0:c0 + LANE], preferred_element_type=F32)
            vm = vm + bs_ref[:, c0:c0 + LANE]
            o_ref[r0:r0 + SGU_CHUNK, c0:c0 + LANE] = (u[r0:r0 + SGU_CHUNK, c0:c0 + LANE] * vm).astype(o_ref.dtype)


def _sgu(rest, ln_g, ln_b, ws, bs, seq):
    n = rest.shape[0]
    tm = min(512, seq)
    kern = functools.partial(_sgu_kernel, tm=tm)
    return pl.pallas_call(
        kern,
        grid=(n // tm,),
        in_specs=[
            pl.BlockSpec((tm, 2 * BRANCH_W), lambda i: (i, R_UV // (2 * BRANCH_W))),
            pl.BlockSpec((1, BRANCH_W), lambda i: (0, 0)),
            pl.BlockSpec((1, BRANCH_W), lambda i: (0, 0)),
            pl.BlockSpec((SGU_GROUPS, SGU_CHUNK, SGU_CHUNK), lambda i: (0, 0, 0)),
            pl.BlockSpec((SGU_CHUNK, BRANCH_W), lambda i: (0, 0)),
        ],
        out_specs=pl.BlockSpec((tm, BRANCH_W), lambda i: (i, 0)),
        out_shape=jax.ShapeDtypeStruct((n, BRANCH_W), BF16),
        compiler_params=_params("parallel"),
    )(rest, ln_g.reshape(1, -1), ln_b.reshape(1, -1), ws, bs)


def _lru_kernel(xf_ref, xfp_ref, xfn_ref, xr_ref, xrp_ref, xrn_ref, cw_ref, cb_ref, wr_ref, br_ref, wi_ref,
                bi_ref, lam_ref, hf_ref, hr_ref, af_ref, uf_ref, ar_ref, ur_ref, hc_ref, *, tt, tps):
    j = pl.program_id(0) % tps
    rowi = lax.broadcasted_iota(I32, (tt, LRU_W), 0)

    def gates(x_ref, xp_ref, xn_ref, at_start, at_end, d, a_ref, u_ref):
        x = x_ref[...].astype(F32)
        prev = jnp.where(at_start, 0.0, xp_ref[...].astype(F32)[HALO - 1:HALO, :])
        nxt = jnp.where(at_end, 0.0, xn_ref[...].astype(F32)[0:2, :])
        xm1 = jnp.where(rowi == 0, prev, pltpu.roll(x, 1, 0))
        xp1 = jnp.where(rowi == tt - 1, nxt[0:1, :], pltpu.roll(x, tt - 1, 0))
        xp2 = jnp.where(rowi == tt - 2, nxt[0:1, :],
                        jnp.where(rowi == tt - 1, nxt[1:2, :], pltpu.roll(x, tt - 2, 0)))
        xc = xm1 * cw_ref[0:1, :] + x * cw_ref[1:2, :] + xp1 * cw_ref[2:3, :] + xp2 * cw_ref[3:4, :] + cb_ref[...]
        xcb = xc.astype(BF16)

        def blockdiag(w_ref):
            return jnp.concatenate(
                [jnp.dot(xcb[:, g * LRU_BW:(g + 1) * LRU_BW], w_ref[d, g], preferred_element_type=F32)
                 for g in range(LRU_BLOCKS)], axis=1)

        r = jax.nn.sigmoid(blockdiag(wr_ref) + br_ref[d:d + 1, :])
        ig = jax.nn.sigmoid(blockdiag(wi_ref) + bi_ref[d:d + 1, :])
        log_a = (-LRU_C * r) * jax.nn.softplus(-lam_ref[d:d + 1, :])
        a = jnp.exp(log_a)
        u = jnp.sqrt(jnp.tanh(-log_a) * (1.0 + a * a)) * (ig * xc)
        a_ref[...] = a
        u_ref[...] = u

    gates(xf_ref, xfp_ref, xfn_ref, j == 0, j == tps - 1, 0, af_ref, uf_ref)
    gates(xr_ref, xrp_ref, xrn_ref, j == tps - 1, j == 0, 1, ar_ref, ur_ref)

    @pl.when(j == 0)
    def _():
        hc_ref[...] = jnp.zeros(hc_ref.shape, F32)

    ng = tt // SUBLANE
    sub = lax.broadcasted_iota(I32, (SUBLANE, LRU_W), 0)

    def body(g, carry):
        hf_prev, hr_prev = carry
        r0 = pl.multiple_of(g * SUBLANE, SUBLANE)
        a = af_ref[pl.ds(r0, SUBLANE), :]
        u = uf_ref[pl.ds(r0, SUBLANE), :]
        for s in (1, 2, 4):
            a_s = jnp.where(sub >= s, pltpu.roll(a, s, 0), 1.0)
            u_s = jnp.where(sub >= s, pltpu.roll(u, s, 0), 0.0)
            u = a * u_s + u
            a = a * a_s
        h = a * hf_prev + u
        hf_ref[pl.ds(r0, SUBLANE), :] = h
        hf_new = jnp.broadcast_to(h[SUBLANE - 1:SUBLANE, :], (SUBLANE, LRU_W))

        r1 = pl.multiple_of((ng - 1 - g) * SUBLANE, SUBLANE)
        a = ar_ref[pl.ds(r1, SUBLANE), :]
        u = ur_ref[pl.ds(r1, SUBLANE), :]
        for s in (1, 2, 4):
            a_s = jnp.where(sub < SUBLANE - s, pltpu.roll(a, SUBLANE - s, 0), 1.0)
            u_s = jnp.where(sub < SUBLANE - s, pltpu.roll(u, SUBLANE - s, 0), 0.0)
            u = a * u_s + u
            a = a * a_s
        h = a * hr_prev + u
        hr_ref[pl.ds(r1, SUBLANE), :] = h
        hr_new = jnp.broadcast_to(h[0:1, :], (SUBLANE, LRU_W))
        return hf_new, hr_new

    hf_c, hr_c = lax.fori_loop(0, ng, body, (hc_ref[0], hc_ref[1]))
    hc_ref[0] = hf_c
    hc_ref[1] = hr_c


def _lru(rest, cw, cb, wr, br, wi, bi, lam, seq):
    n = rest.shape[0]
    tt = min(512, seq)
    tps = seq // tt
    nt = n // tt
    t8 = tt // HALO
    last8 = n // HALO - 1
    cblk = R_XC // LRU_W

    def rev(i):
        return (i // tps) * tps + (tps - 1 - i % tps)

    kern = functools.partial(_lru_kernel, tt=tt, tps=tps)
    full = lambda shape: pl.BlockSpec(shape, lambda i: (0,) * len(shape))
    return pl.pallas_call(
        kern,
        grid=(nt,),
        in_specs=[
            pl.BlockSpec((tt, LRU_W), lambda i: (i, cblk)),
            pl.BlockSpec((HALO, LRU_W), lambda i: (jnp.maximum(i * t8 - 1, 0), cblk)),
            pl.BlockSpec((HALO, LRU_W), lambda i: (jnp.minimum((i + 1) * t8, last8), cblk)),
            pl.BlockSpec((tt, LRU_W), lambda i: (rev(i), cblk)),
            pl.BlockSpec((HALO, LRU_W), lambda i: (jnp.maximum(rev(i) * t8 - 1, 0), cblk)),
            pl.BlockSpec((HALO, LRU_W), lambda i: (jnp.minimum((rev(i) + 1) * t8, last8), cblk)),
            full((4, LRU_W)),
            full((1, LRU_W)),
            full((2, LRU_BLOCKS, LRU_BW, LRU_BW)),
            full((2, LRU_W)),
            full((2, LRU_BLOCKS, LRU_BW, LRU_BW)),
            full((2, LRU_W)),
            full((2, LRU_W)),
        ],
        out_specs=[
            pl.BlockSpec((tt, LRU_W), lambda i: (i, 0)),
            pl.BlockSpec((tt, LRU_W), lambda i: (rev(i), 0)),
        ],
        out_shape=[jax.ShapeDtypeStruct((n, LRU_W), F32), jax.ShapeDtypeStruct((n, LRU_W), F32)],
        scratch_shapes=[pltpu.VMEM((tt, LRU_W), F32)] * 4 + [pltpu.VMEM((2, SUBLANE, LRU_W), F32)],
        compiler_params=_params("arbitrary"),
    )(rest, rest, rest, rest, rest, rest, cw, cb.reshape(1, -1), wr, br, wi, bi, lam)


def _merge_kernel(ya_ref, yb_ref, hf_ref, hr_ref, gate_ref, yd_ref, wbr_ref, g0_ref, g1_ref, g2_ref, g3_ref, o_ref):
    yc = (jax.nn.gelu(gate_ref[...].astype(F32)) * (hf_ref[...] + hr_ref[...])).astype(BF16)
    ys = (ya_ref[...], yb_ref[...], yc, yd_ref[...])
    gls = (g0_ref, g1_ref, g2_ref, g3_ref)
    acc = None
    for k in range(N_BRANCH):
        t = jax.nn.sigmoid(gls[k][...].astype(F32)) * jnp.dot(ys[k], wbr_ref[k], preferred_element_type=F32)
        acc = t if acc is None else acc + t
    o_ref[...] = acc.astype(o_ref.dtype)


def _merge(ya, yb, hf, hr, rest, yd, wbr, seq):
    n = ya.shape[0]
    d = wbr.shape[-1]
    tm = min(512, seq)
    tn = 512
    row = lambda w: pl.BlockSpec((tm, w), lambda i, j: (i, 0))

    def gl_spec(k):
        base = (R_GL + k * d) // tn
        return pl.BlockSpec((tm, tn), lambda i, j: (i, base + j))

    return pl.pallas_call(
        _merge_kernel,
        grid=(n // tm, d // tn),
        in_specs=[
            row(BRANCH_W), row(BRANCH_W), row(LRU_W), row(LRU_W),
            pl.BlockSpec((tm, LRU_W), lambda i, j: (i, R_GATE // LRU_W)),
            row(BRANCH_W),
            pl.BlockSpec((N_BRANCH, BRANCH_W, tn), lambda i, j: (0, 0, j)),
            gl_spec(0), gl_spec(1), gl_spec(2), gl_spec(3),
        ],
        out_specs=pl.BlockSpec((tm, tn), lambda i, j: (i, j)),
        out_shape=jax.ShapeDtypeStruct((n, d), BF16),
        compiler_params=_params("parallel", "parallel"),
    )(ya, yb, hf, hr, rest, yd, wbr, rest, rest, rest, rest)


def _outproj_kernel(m_ref, w_ref, x_ref, g_ref, o_ref):
    o_ref[...] = x_ref[...] + g_ref[...] * jnp.dot(m_ref[...], w_ref[...], preferred_element_type=F32)


def _outproj(merged, w, x, g1, seq):
    n, d = x.shape
    tm = min(512, seq)
    tps = seq // tm
    return pl.pallas_call(
        _outproj_kernel,
        grid=(n // tm,),
        in_specs=[
            pl.BlockSpec((tm, d), lambda i: (i, 0)),
            pl.BlockSpec((d, d), lambda i: (0, 0)),
            pl.BlockSpec((tm, d), lambda i: (i, 0)),
            pl.BlockSpec((None, 1, d), lambda i: (i // tps, 0, 0)),
        ],
        out_specs=pl.BlockSpec((tm, d), lambda i: (i, 0)),
        out_shape=jax.ShapeDtypeStruct((n, d), F32),
        compiler_params=_params("parallel"),
    )(merged, w, x, g1)


def _router_kernel(x_ref, g_ref, mod_ref, wh_ref, wl_ref, rb_ref, tri_ref, g2_ref, swg_ref, swu_ref, swd_ref,
                   x1_ref, h2g_ref, e_ref, w_ref, r_ref, cnt_ref, run_ref, *, tm, d):
    @pl.when(pl.program_id(0) == 0)
    def _():
        run_ref[...] = jnp.zeros(run_ref.shape, F32)

    x = x_ref[...]
    y = x * lax.rsqrt(jnp.mean(x * x, axis=-1, keepdims=True) + EPS) * g_ref[...]
    h = y * (1.0 + mod_ref[0:1, :]) + mod_ref[1:2, :]
    hb = h.astype(BF16)
    hs = (jax.nn.silu(jnp.dot(hb, swg_ref[...], preferred_element_type=F32))
          * jnp.dot(hb, swu_ref[...], preferred_element_type=F32)).astype(BF16)
    x1_ref[...] = x + g2_ref[...] * jnp.dot(hs, swd_ref[...], preferred_element_type=F32)
    slabs = d // LANE
    for s in range(slabs):
        h2g_ref[pl.ds(s, tm, stride=slabs), :] = h[:, s * LANE:(s + 1) * LANE]

    hl = (h - hb.astype(F32)).astype(BF16)
    logits = (lax.dot_general(wh_ref[...], hb, NT_DIMS, preferred_element_type=F32)
              + lax.dot_general(wl_ref[...], hb, NT_DIMS, preferred_element_type=F32)
              + lax.dot_general(wh_ref[...], hl, NT_DIMS, preferred_element_type=F32))
    sc = jax.nn.sigmoid(logits)
    sel = sc + rb_ref[...]

    neg = -jnp.inf
    i8 = lax.broadcasted_iota(I32, (GROUP_SZ, tm), 0).astype(F32)
    rows = []
    for g in range(N_GROUPS):
        blk = sel[g * GROUP_SZ:(g + 1) * GROUP_SZ, :]
        m1 = jnp.max(blk, axis=0, keepdims=True)
        i1 = jnp.min(jnp.where(blk == m1, i8, float(GROUP_SZ)), axis=0, keepdims=True)
        m2 = jnp.max(jnp.where(i8 == i1, neg, blk), axis=0, keepdims=True)
        rows.append(m1 + m2)
    gs = jnp.concatenate(rows, axis=0)
    gi8 = lax.broadcasted_iota(I32, (N_GROUPS, tm), 0).astype(F32)
    gsel = jnp.zeros((N_GROUPS, tm), F32)
    for _ in range(TOPK_GROUPS):
        gm = jnp.max(gs, axis=0, keepdims=True)
        gi = jnp.min(jnp.where(gs == gm, gi8, float(N_GROUPS)), axis=0, keepdims=True)
        hit = gi8 == gi
        gsel = jnp.where(hit, 1.0, gsel)
        gs = jnp.where(hit, neg, gs)
    emask = jnp.concatenate(
        [jnp.broadcast_to(gsel[g:g + 1, :], (GROUP_SZ, tm)) for g in range(N_GROUPS)], axis=0)
    selm = jnp.where(emask > 0.0, sel, neg)

    i64 = lax.broadcasted_iota(I32, (N_EXPERTS, tm), 0).astype(F32)
    chosen = jnp.zeros((N_EXPERTS, tm), F32)
    idxs, wts = [], []
    for _ in range(TOP_K):
        mx = jnp.max(selm, axis=0, keepdims=True)
        ix = jnp.min(jnp.where(selm == mx, i64, float(N_EXPERTS)), axis=0, keepdims=True)
        oh = i64 == ix
        wts.append(jnp.sum(jnp.where(oh, sc, 0.0), axis=0, keepdims=True))
        idxs.append(ix)
        selm = jnp.where(oh, neg, selm)
        chosen = jnp.where(oh, 1.0, chosen)
    wk = jnp.concatenate(wts, axis=0)
    w_ref[...] = wk / jnp.sum(wk, axis=0, keepdims=True) * ROUTE_SCALE
    e_ref[...] = jnp.concatenate(idxs, axis=0).astype(I32)

    before = jnp.dot(chosen.astype(BF16), tri_ref[...], preferred_element_type=F32)
    rank = run_ref[...] + before
    r_ref[...] = jnp.concatenate(
        [jnp.sum(jnp.where(i64 == idxs[k], rank, 0.0), axis=0, keepdims=True) for k in range(TOP_K)],
        axis=0).astype(I32)
    run_ref[...] = run_ref[...] + jnp.sum(chosen, axis=1, keepdims=True)
    cnt_ref[...] = jnp.broadcast_to(run_ref[...], cnt_ref.shape).astype(I32)


def _router(x, g, mod, wh, wl, rb, g2, swg, swu, swd, seq):
    n, d = x.shape
    ds_ = swg.shape[1]
    tm = min(512, seq)
    tps = seq // tm
    slabs = d // LANE
    tri = jnp.triu(jnp.ones((tm, tm), F32), 1).astype(BF16)
    kern = functools.partial(_router_kernel, tm=tm, d=d)
    return pl.pallas_call(
        kern,
        grid=(n // tm,),
        in_specs=[
            pl.BlockSpec((tm, d), lambda i: (i, 0)),
            pl.BlockSpec((1, d), lambda i: (0, 0)),
            pl.BlockSpec((None, 2, d), lambda i: (i // tps, 0, 0)),
            pl.BlockSpec((N_EXPERTS, d), lambda i: (0, 0)),
            pl.BlockSpec((N_EXPERTS, d), lambda i: (0, 0)),
            pl.BlockSpec((N_EXPERTS, 1), lambda i: (0, 0)),
            pl.BlockSpec((tm, tm), lambda i: (0, 0)),
            pl.BlockSpec((None, 1, d), lambda i: (i // tps, 0, 0)),
            pl.BlockSpec((d, ds_), lambda i: (0, 0)),
            pl.BlockSpec((d, ds_), lambda i: (0, 0)),
            pl.BlockSpec((ds_, d), lambda i: (0, 0)),
        ],
        out_specs=[
            pl.BlockSpec((tm, d), lambda i: (i, 0)),
            pl.BlockSpec((tm * slabs, LANE), lambda i: (i, 0)),
            pl.BlockSpec((TOP_K, tm), lambda i: (0, i)),
            pl.BlockSpec((TOP_K, tm), lambda i: (0, i)),
            pl.BlockSpec((TOP_K, tm), lambda i: (0, i)),
            pl.BlockSpec((N_EXPERTS, LANE), lambda i: (0, 0)),
        ],
        out_shape=[
            jax.ShapeDtypeStruct((n, d), F32),
            jax.ShapeDtypeStruct((n * slabs, LANE), F32),
            jax.ShapeDtypeStruct((TOP_K, n), I32),
            jax.ShapeDtypeStruct((TOP_K, n), F32),
            jax.ShapeDtypeStruct((TOP_K, n), I32),
            jax.ShapeDtypeStruct((N_EXPERTS, LANE), I32),
        ],
        scratch_shapes=[pltpu.VMEM((N_EXPERTS, 1), F32)],
        compiler_params=_params("arbitrary"),
    )(x, g.reshape(1, d), mod, wh, wl, rb.reshape(N_EXPERTS, 1), tri, g2, swg, swu, swd)


def _dispatch_kernel(dest_ref, pend_ref, pc_ref, h_ref, xs_ref, zero_ref, sem, zsem, *, tm, n, slabs):
    i = pl.program_id(0)
    rows_blk = MOE_BLK * slabs

    def tail_copy(e):
        start = pl.multiple_of((pend_ref[e] - MOE_BLK) * slabs, rows_blk)
        return pltpu.make_async_copy(zero_ref, xs_ref.at[pl.ds(start, rows_blk), :], zsem)

    @pl.when(i == 0)
    def _():
        zero_ref[...] = jnp.zeros(zero_ref.shape, F32)

        def zstart(e, c):
            @pl.when(pc_ref[e] > 0)
            def _():
                tail_copy(e).start()
            return c

        def zwait(e, c):
            @pl.when(pc_ref[e] > 0)
            def _():
                tail_copy(e).wait()
            return c

        lax.fori_loop(0, N_EXPERTS, zstart, 0)
        lax.fori_loop(0, N_EXPERTS, zwait, 0)

    def row_copy(r, k):
        src = h_ref.at[pl.ds(pl.multiple_of(r * slabs, slabs), slabs), :]
        dst_row = dest_ref[k * n + i * tm + r]
        dst = xs_ref.at[pl.ds(pl.multiple_of(dst_row * slabs, slabs), slabs), :]
        return pltpu.make_async_copy(src, dst, sem)

    def start(r, c):
        for k in range(TOP_K):
            row_copy(r, k).start(priority=k % 2)
        return c

    lax.fori_loop(0, tm, start, 0)
    for _ in range(TOP_K):
        pltpu.make_async_copy(h_ref, xs_ref.at[pl.ds(0, tm * slabs), :], sem).wait()


def _dispatch(dest, pend, pc, h2g, n_rows, seq):
    n = dest.shape[0] // TOP_K
    slabs = h2g.shape[0] // n
    tm = min(256, seq)
    kern = functools.partial(_dispatch_kernel, tm=tm, n=n, slabs=slabs)
    return pl.pallas_call(
        kern,
        grid_spec=pltpu.PrefetchScalarGridSpec(
            num_scalar_prefetch=3,
            grid=(n // tm,),
            in_specs=[pl.BlockSpec((tm * slabs, LANE), lambda i, *_: (i, 0))],
            out_specs=pl.BlockSpec(memory_space=pl.ANY),
            scratch_shapes=[
                pltpu.VMEM((MOE_BLK * slabs, LANE), F32),
                pltpu.SemaphoreType.DMA(()),
                pltpu.SemaphoreType.DMA(()),
            ],
        ),
        out_shape=jax.ShapeDtypeStruct((n_rows * slabs, LANE), F32),
        compiler_params=_params("arbitrary"),
    )(dest, pend, pc, h2g)


def _gmm_kernel(be_ref, nu_ref, x_ref, wg_ref, wu_ref, wd_ref, o_ref, *, slabs):
    @pl.when(pl.program_id(0) < nu_ref[0])
    def _():
        x = jnp.concatenate(
            [x_ref[pl.ds(s, MOE_BLK, stride=slabs), :].astype(BF16) for s in range(slabs)], axis=1)
        hg = jnp.dot(x, wg_ref[...], preferred_element_type=F32)
        hu = jnp.dot(x, wu_ref[...], preferred_element_type=F32)
        hb = (jax.nn.silu(hg) * hu).astype(BF16)
        per = 4
        for c in range(slabs // per):
            y = jnp.dot(hb, wd_ref[:, c * per * LANE:(c + 1) * per * LANE], preferred_element_type=F32)
            for j in range(per):
                o_ref[pl.ds(c * per + j, MOE_BLK, stride=slabs), :] = y[:, j * LANE:(j + 1) * LANE]


def _gmm(blk_e, nused, xs, wg, wu, wd, layer):
    _, _, d, de = wg.shape
    slabs = d // LANE
    nblk = xs.shape[0] // (MOE_BLK * slabs)
    kern = functools.partial(_gmm_kernel, slabs=slabs)

    def blk(b, be, nu):
        return jnp.minimum(b, nu[0] - 1)

    return pl.pallas_call(
        kern,
        grid_spec=pltpu.PrefetchScalarGridSpec(
            num_scalar_prefetch=2,
            grid=(nblk,),
            in_specs=[
                pl.BlockSpec((MOE_BLK * slabs, LANE), lambda b, be, nu: (blk(b, be, nu), 0)),
                pl.BlockSpec((None, None, d, de), lambda b, be, nu: (layer, be[blk(b, be, nu)], 0, 0)),
                pl.BlockSpec((None, None, d, de), lambda b, be, nu: (layer, be[blk(b, be, nu)], 0, 0)),
                pl.BlockSpec((None, None, de, d), lambda b, be, nu: (layer, be[blk(b, be, nu)], 0, 0)),
            ],
            out_specs=pl.BlockSpec((MOE_BLK * slabs, LANE), lambda b, be, nu: (blk(b, be, nu), 0)),
        ),
        out_shape=jax.ShapeDtypeStruct(xs.shape, F32),
        compiler_params=_params("arbitrary"),
    )(blk_e, nused, xs, wg, wu, wd)


def _combine_kernel(dest_ref, wt_ref, x_ref, g2_ref, fg_ref, ys_ref, o_ref, buf_ref, wb_ref, sem,
                    *, tm, n, slabs, nt, final):
    i = pl.program_id(0)
    slot = i % 2
    rows = tm * COMBINE_PITCH

    def gather(tile, sl):
        def start(r, c):
            for k in range(TOP_K):
                src_row = dest_ref[k * n + tile * tm + r]
                src = ys_ref.at[pl.ds(pl.multiple_of(src_row * slabs, slabs), slabs), :]
                dst = buf_ref.at[sl, pl.ds(pl.multiple_of(k * rows + r * COMBINE_PITCH, SUBLANE), slabs), :]
                pltpu.make_async_copy(src, dst, sem.at[sl]).start(priority=k % 2)
            return c

        lax.fori_loop(0, tm, start, 0)

    @pl.when(i == 0)
    def _():
        gather(0, 0)

    @pl.when(i + 1 < nt)
    def _():
        gather(i + 1, 1 - slot)

    moved = TOP_K * tm * slabs
    pltpu.make_async_copy(ys_ref.at[pl.ds(0, moved), :], buf_ref.at[slot, pl.ds(0, moved), :], sem.at[slot]).wait()

    wt = wt_ref[...]
    for k in range(TOP_K):
        wb_ref[k] = jnp.broadcast_to(wt[:, k:k + 1], (tm, LANE))
    for s in range(slabs):
        c0 = s * LANE
        routed = None
        for k in range(TOP_K):
            t = wb_ref[k] * buf_ref[slot, pl.ds(k * rows + s, tm, stride=COMBINE_PITCH), :]
            routed = t if routed is None else routed + t
        o_ref[:, c0:c0 + LANE] = x_ref[:, c0:c0 + LANE] + g2_ref[:, c0:c0 + LANE] * routed
    if final:
        o = o_ref[...]
        o_ref[...] = o * lax.rsqrt(jnp.mean(o * o, axis=-1, keepdims=True) + EPS) * fg_ref[...]


def _combine(dest, wt, x, g2, ys, final_g, final, seq):
    n, d = x.shape
    slabs = d // LANE
    tm = min(128, seq)
    tps = seq // tm
    assert COMBINE_PITCH >= slabs
    kern = functools.partial(_combine_kernel, tm=tm, n=n, slabs=slabs, nt=n // tm, final=final)
    return pl.pallas_call(
        kern,
        grid_spec=pltpu.PrefetchScalarGridSpec(
            num_scalar_prefetch=1,
            grid=(n // tm,),
            in_specs=[
                pl.BlockSpec((tm, TOP_K), lambda i, *_: (i, 0)),
                pl.BlockSpec((tm, d), lambda i, *_: (i, 0)),
                pl.BlockSpec((None, 1, d), lambda i, *_: (i // tps, 0, 0)),
                pl.BlockSpec((1, d), lambda i, *_: (0, 0)),
                pl.BlockSpec(memory_space=pl.ANY),
            ],
            out_specs=pl.BlockSpec((tm, d), lambda i, *_: (i, 0)),
            scratch_shapes=[
                pltpu.VMEM((2, TOP_K * tm * COMBINE_PITCH, LANE), F32),
                pltpu.VMEM((TOP_K, tm, LANE), F32),
                pltpu.SemaphoreType.DMA((2,)),
            ],
        ),
        out_shape=jax.ShapeDtypeStruct((n, d), F32),
        compiler_params=_params("arbitrary"),
    )(dest, wt, x, g2, final_g.reshape(1, d), ys)


def _swap_halves(w):
    half = w.shape[-1] // 2
    return jnp.concatenate([w[..., half:], w[..., :half]], axis=-1)


def _prep_w_in(w):
    depth, d, _ = w.shape
    sizes = [HA * 2 * DHA, HA * 2 * DHA, HA * DVA, 2 * BRANCH_W, 2 * LRU_W, Q_LORA, KV_LORA, D_ROPE, N_BRANCH * d]
    offs = np.cumsum([0] + sizes)
    qa, ka, va, uv, xg, cq, ckv, kr, gl = [w[:, :, offs[i]:offs[i + 1]] for i in range(len(sizes))]
    pad = jnp.zeros((depth, d, 512 - KV_LORA - 2 * D_ROPE), w.dtype)
    w_att = jnp.concatenate([qa * (DHA ** -0.5 * LOG2E), ka, va], axis=2).astype(BF16)
    w_rest = jnp.concatenate([uv, xg, cq, ckv, kr, _swap_halves(kr), pad, gl], axis=2).astype(BF16)
    return w_att, w_rest


def _prep_layer(l, p):
    wuq =p['w_uq'][l].reshape(Q_LORA, HD, D_NOPE + D_ROPE)
    wuq_r = wuq[:, :, D_NOPE:]
    wuq_ext = jnp.concatenate([wuq[:, :, :D_NOPE], wuq_r, _swap_halves(wuq_r)], axis=-1)
    wuq_ext = wuq_ext.reshape(Q_LORA, HD * 2 * LANE).astype(BF16)

    lam_init = 0.8 - 0.6 * math.exp(-0.3 * l)
    lam = (jnp.exp(jnp.sum(p['lam_q1'][l] * p['lam_k1'][l])) - jnp.exp(jnp.sum(p['lam_q2'][l] * p['lam_k2'][l]))
           + lam_init)
    slopes = jnp.asarray(2.0 ** (-8.0 * np.arange(1, HA + 1) / HA) * LOG2E, F32)
    par = jnp.concatenate([slopes, lam.reshape(1), jnp.full((1,), 1.0 - lam_init, F32), jnp.zeros((2,), F32)])

    rw = p['router_w'][l].T
    rw_hi = rw.astype(BF16)
    rw_lo = (rw - rw_hi.astype(F32)).astype(BF16)
    bs = jnp.repeat(p['sgu_b'][l].T, LANE, axis=1)
    return dict(
        wuq=wuq_ext, wukv=p['w_ukv'][l].astype(BF16), par=par,
        sgu_w=p['sgu_w'][l].astype(BF16), sgu_bs=bs,
        rg_wr=p['rg_wr'][l].astype(BF16), rg_wi=p['rg_wi'][l].astype(BF16),
        w_branch=p['w_branch'][l].astype(BF16), w_out=p['w_out'][l].astype(BF16),
        rw_hi=rw_hi, rw_lo=rw_lo,
        sh_wg=p['sh_wg'][l].astype(BF16), sh_wu=p['sh_wu'][l].astype(BF16), sh_wd=p['sh_wd'][l].astype(BF16),
    )


def _rope_table(n_seq, seq):
    inv_freq = ROPE_THETA ** (-jnp.arange(0, D_ROPE, 2, dtype=F32) / D_ROPE)
    ang = jnp.arange(seq, dtype=F32)[:, None] * inv_freq[None, :]
    cos, sin = jnp.cos(ang), jnp.sin(ang)
    tab = jnp.concatenate([cos, cos, -sin, sin], axis=1)
    return jnp.tile(tab, (n_seq, 1))


def _trunk(x3, mods, p, preps):
    n_seq, seq, d = x3.shape
    n = n_seq * seq
    x = x3.reshape(n, d)
    tab = _rope_table(n_seq, seq)
    depth = len(preps)
    for l in range(depth):
        w = preps[l]
        mod = mods[l]
        mod1 = jnp.stack([mod[:, 1], mod[:, 0]], axis=1)
        mod2 = jnp.stack([mod[:, 4], mod[:, 3]], axis=1)
        g1 = mod[:, 2:3]
        g2 = mod[:, 5:6]

        att = _inproj(x, p['norm1_g'][l], mod1, p['w_att'], l, BF16, seq)
        rest = _inproj(x, p['norm1_g'][l], mod1, p['w_rest'], l, BF16, seq)
        ya = _diff_attn(att, w['par'], p['subln_g'][l], n_seq, seq)
        yb = _sgu(rest, p['sgu_ln_g'][l], p['sgu_ln_b'][l], w['sgu_w'], w['sgu_bs'], seq)
        hf, hr = _lru(rest, p['conv_w'][l], p['conv_b'][l], w['rg_wr'], p['rg_br'][l], w['rg_wi'],
                      p['rg_bi'][l], p['rg_lam'][l], seq)
        q, k, v = _mla_prep(rest, tab, p['q_norm_g'][l], p['kv_norm_g'][l], w['wuq'], w['wukv'], seq)
        yd = _mla_attn(q, k, v, n_seq, seq)
        merged = _merge(ya, yb, hf, hr, rest, yd, w['w_branch'], seq)
        x = _outproj(merged, w['w_out'], x, g1, seq)

        x, h2g, eidx, wgt, rank, cnt = _router(x, p['norm2_g'][l], mod2, w['rw_hi'], w['rw_lo'],
                                               p['router_bias'][l], g2, w['sh_wg'], w['sh_wu'], w['sh_wd'], seq)
        counts = cnt[:, 0]
        pc = ((counts + MOE_BLK - 1) // MOE_BLK) * MOE_BLK
        pend = jnp.cumsum(pc).astype(I32)
        pstart = pend - pc
        eid = jnp.arange(N_EXPERTS, dtype=I32)
        dest = (jnp.sum(jnp.where(eidx[..., None] == eid, pstart, 0), axis=-1) + rank).reshape(-1).astype(I32)
        nblk = n * TOP_K // MOE_BLK + N_EXPERTS
        blk_row = jnp.arange(nblk, dtype=I32)[:, None] * MOE_BLK
        blk_e = jnp.minimum(jnp.sum((pend[None, :] <= blk_row).astype(I32), axis=1), N_EXPERTS - 1)
        nused = (pend[-1:] // MOE_BLK).astype(I32)
        xs = _dispatch(dest, pend, pc.astype(I32), h2g, nblk * MOE_BLK, seq)
        ys = _gmm(blk_e, nused, xs, p['exp_wg_bf16'], p['exp_wu_bf16'], p['exp_wd_bf16'], l)
        x = _combine(dest, wgt.T, x, g2, ys, p['final_g'], l == depth - 1, seq)
    return x.reshape(n_seq, seq, d)


def kernel(x_prompt, x_sample, c_prompt, c_sample, ada_w, ada_b, norm1_g, norm2_g, w_in, lam_q1, lam_k1, lam_q2, lam_k2, subln_g, sgu_ln_g, sgu_ln_b, sgu_w, sgu_b, conv_w, conv_b, rg_wr, rg_br, rg_wi, rg_bi, rg_lam, q_norm_g, kv_norm_g, w_uq, w_ukv, w_branch, w_out, router_w, router_bias, exp_wg, exp_wu, exp_wd, sh_wg, sh_wu, sh_wd, final_g):
    p = dict(ada_w=ada_w, ada_b=ada_b, norm1_g=norm1_g, norm2_g=norm2_g, w_in=w_in,
             lam_q1=lam_q1, lam_k1=lam_k1, lam_q2=lam_q2, lam_k2=lam_k2, subln_g=subln_g,
             sgu_ln_g=sgu_ln_g, sgu_ln_b=sgu_ln_b, sgu_w=sgu_w, sgu_b=sgu_b,
             conv_w=conv_w, conv_b=conv_b, rg_wr=rg_wr, rg_br=rg_br, rg_wi=rg_wi, rg_bi=rg_bi,
             rg_lam=rg_lam, q_norm_g=q_norm_g, kv_norm_g=kv_norm_g, w_uq=w_uq, w_ukv=w_ukv,
             w_branch=w_branch, w_out=w_out, router_w=router_w, router_bias=router_bias,
             exp_wg=exp_wg, exp_wu=exp_wu, exp_wd=exp_wd, sh_wg=sh_wg, sh_wu=sh_wu, sh_wd=sh_wd,
             final_g=final_g)
    depth, d, _ = ada_w.shape
    bp, bs = x_prompt.shape[0], x_sample.shape[0]
    assert bp + bs <= 8
    c8 = jnp.concatenate([c_prompt, c_sample, jnp.zeros((8 - bp - bs, d), F32)], axis=0)
    mod = _ada_mod(c8, ada_w, ada_b).reshape(depth, 8, 6, d)
    p['w_att'], p['w_rest'] = _prep_w_in(w_in)
    p.update(exp_wg_bf16=exp_wg.astype(BF16), exp_wu_bf16=exp_wu.astype(BF16), exp_wd_bf16=exp_wd.astype(BF16))
    preps = [_prep_layer(l, p) for l in range(depth)]
    y_prompt = _trunk(x_prompt, mod[:, :bp], p, preps)
    y_sample = _trunk(x_sample, mod[:, bp:bp + bs], p, preps)
    return (y_prompt, y_sample)
```

```python
import functools
import math

import numpy as np
import jax
import jax.numpy as jnp
from jax import lax
from jax.experimental import pallas as pl
from jax.experimental.pallas import tpu as pltpu

F32 = jnp.float32
BF16 = jnp.bfloat16
I32 = jnp.int32

EPS = 1e-6
LOG2E = math.log2(math.e)
LANE = 128
SUBLANE = 8
VMEM_LIMIT = 48 * 1024 * 1024

HA = 4
DHA = 64
DVA = 2 * DHA
DIFF_HEADS_PER_STEP = 2
BRANCH_W = 512
SGU_CHUNK = 128
SGU_GROUPS = 4
LRU_W = 512
LRU_BLOCKS = 4
LRU_BW = LRU_W // LRU_BLOCKS
LRU_C = 8.0
HALO = 16
HD = 4
Q_LORA = 512
KV_LORA = 256
D_NOPE = 128
D_ROPE = 64
DV_D = 128
ROPE_THETA = 10000.0
N_EXPERTS = 64
TOP_K = 8
N_GROUPS = 8
TOPK_GROUPS = 4
GROUP_SZ = N_EXPERTS // N_GROUPS
ROUTE_SCALE = 2.5
MOE_BLK = 512
N_BRANCH = 4
COMBINE_PITCH = 24

R_UV = 0
R_XC = 1024
R_GATE = 1536
R_CQ = 2048
R_CKV = 2560
R_GL = 3072

NT_DIMS = (((1,), (1,)), ((), ()))


def _params(*sem):
    return pltpu.CompilerParams(dimension_semantics=sem, vmem_limit_bytes=VMEM_LIMIT)


def _ada_kernel(c_ref, w_ref, b_ref, o_ref):
    a = jax.nn.silu(c_ref[...]).astype(BF16)
    o_ref[...] = jnp.dot(a, w_ref[...].astype(BF16), preferred_element_type=F32) + b_ref[...]


def _ada_mod(c8, ada_w, ada_b):
    depth, d, n = ada_w.shape
    tn = 1024
    return pl.pallas_call(
        _ada_kernel,
        grid=(depth, n // tn),
        in_specs=[
            pl.BlockSpec((8, d), lambda l, j: (0, 0)),
            pl.BlockSpec((None, d, tn), lambda l, j: (l, 0, j)),
            pl.BlockSpec((None, 1, tn), lambda l, j: (l, 0, j)),
        ],
        out_specs=pl.BlockSpec((None, 8, tn), lambda l, j: (l, 0, j)),
        out_shape=jax.ShapeDtypeStruct((depth, 8, n), F32),
        compiler_params=_params("parallel", "parallel"),
    )(c8, ada_w, ada_b.reshape(depth, 1, n))


def _inproj_kernel(x_ref, g_ref, mod_ref, w_ref, o_ref, h_ref):
    @pl.when(pl.program_id(1) == 0)
    def _():
        x = x_ref[...]
        y = x * lax.rsqrt(jnp.mean(x * x, axis=-1, keepdims=True) + EPS) * g_ref[...]
        h_ref[...] = (y * (1.0 + mod_ref[0:1, :]) + mod_ref[1:2, :]).astype(BF16)

    o_ref[...] = jnp.dot(h_ref[...], w_ref[...], preferred_element_type=F32).astype(o_ref.dtype)


def _inproj(x, g, mod, w, layer, out_dtype, seq):
    n, d = x.shape
    nc = w.shape[2]
    tm = min(1024, seq)
    tn = 1024 if nc % 1024 == 0 else 512
    tps = seq // tm
    return pl.pallas_call(
        _inproj_kernel,
        grid=(n // tm, nc // tn),
        in_specs=[
            pl.BlockSpec((tm, d), lambda i, j: (i, 0)),
            pl.BlockSpec((1, d), lambda i, j: (0, 0)),
            pl.BlockSpec((None, 2, d), lambda i, j: (i // tps, 0, 0)),
            pl.BlockSpec((None, d, tn), lambda i, j: (layer, 0, j)),
        ],
        out_specs=pl.BlockSpec((tm, tn), lambda i, j: (i, j)),
        out_shape=jax.ShapeDtypeStruct((n, nc), out_dtype),
        scratch_shapes=[pltpu.VMEM((tm, d), BF16)],
        compiler_params=_params("parallel", "arbitrary"),
    )(x, g.reshape(1, d), mod, w)


def _online_softmax_step(s, vx, m_ref, acc_ref, idx, tk):
    m_prev = m_ref[idx]
    m_new = jnp.maximum(m_prev, jnp.max(s, axis=-1, keepdims=True))
    alpha = jnp.exp2(m_prev - m_new)
    p = jnp.exp2(s - jnp.concatenate([m_new] * (tk // LANE), axis=1))
    pv = jnp.dot(p.astype(BF16), vx, preferred_element_type=F32)
    acc_ref[idx] = jnp.concatenate([alpha, alpha], axis=1) * acc_ref[idx] + pv
    m_ref[idx] = m_new


def _diff_attn_kernel(par_ref, q_ref, k_ref, v_ref, g_ref, o_ref, m_ref, acc_ref, *, tq, tk, nk):
    hg = pl.program_id(1)
    qi = pl.program_id(2)
    ki = pl.program_id(3)

    @pl.when(ki == 0)
    def _():
        m_ref[...] = jnp.full(m_ref.shape, -jnp.inf, F32)
        acc_ref[...] = jnp.zeros(acc_ref.shape, F32)

    lane = lax.broadcasted_iota(I32, (tq, LANE), 1)
    row = lax.broadcasted_iota(I32, (tq, tk), 0)
    col = lax.broadcasted_iota(I32, (tq, tk), 1)
    dist = jnp.abs(row - col + (qi * tq - ki * tk)).astype(F32)
    ones = jnp.ones((tk, LANE), BF16)
    for h in range(DIFF_HEADS_PER_STEP):
        q = q_ref[:, h * LANE:(h + 1) * LANE]
        k = k_ref[:, h * LANE:(h + 1) * LANE]
        vx = jnp.concatenate([v_ref[:, h * LANE:(h + 1) * LANE], ones], axis=1)
        bias = dist * par_ref[hg * DIFF_HEADS_PER_STEP + h]
        zero = jnp.zeros_like(q)
        for mi, qm in enumerate((jnp.where(lane < DHA, q, zero), jnp.where(lane < DHA, zero, q))):
            s = lax.dot_general(qm, k, NT_DIMS, preferred_element_type=F32) - bias
            _online_softmax_step(s, vx, m_ref, acc_ref, 2 * h + mi, tk)

    @pl.when(ki == nk - 1)
    def _():
        lam = par_ref[HA]
        for h in range(DIFF_HEADS_PER_STEP):
            a1 = acc_ref[2 * h]
            a2 = acc_ref[2 * h + 1]
            o = a1[:, :LANE] / a1[:, LANE:] - lam * (a2[:, :LANE] / a2[:, LANE:])
            y = o * lax.rsqrt(jnp.mean(o * o, axis=-1, keepdims=True) + EPS) * g_ref[...]
            o_ref[:, h * LANE:(h + 1) * LANE] = (y * par_ref[HA + 1]).astype(o_ref.dtype)


def _diff_attn(att, par, subln_g, n_seq, seq):
    n = att.shape[0]
    tq = min(1024, seq)
    tk = min(512, seq)
    nq, nk = seq // tq, seq // tk
    w = DIFF_HEADS_PER_STEP * LANE
    ng = HA // DIFF_HEADS_PER_STEP
    kern = functools.partial(_diff_attn_kernel, tq=tq, tk=tk, nk=nk)
    return pl.pallas_call(
        kern,
        grid=(n_seq, ng, nq, nk),
        in_specs=[
            pl.BlockSpec(memory_space=pltpu.SMEM),
            pl.BlockSpec((tq, w), lambda b, g, qi, ki: (b * nq + qi, g)),
            pl.BlockSpec((tk, w), lambda b, g, qi, ki: (b * nk + ki, ng + g)),
            pl.BlockSpec((tk, w), lambda b, g, qi, ki: (b * nk + ki, 2 * ng + g)),
            pl.BlockSpec((1, DVA), lambda b, g, qi, ki: (0, 0)),
        ],
        out_specs=pl.BlockSpec((tq, w), lambda b, g, qi, ki: (b * nq + qi, g)),
        out_shape=jax.ShapeDtypeStruct((n, HA * DVA), BF16),
        scratch_shapes=[
            pltpu.VMEM((2 * DIFF_HEADS_PER_STEP, tq, LANE), F32),
            pltpu.VMEM((2 * DIFF_HEADS_PER_STEP, tq, 2 * LANE), F32),
        ],
        compiler_params=_params("parallel", "parallel", "parallel", "arbitrary"),
    )(par, att, att, att, subln_g.reshape(1, DVA))


def _mla_prep_kernel(cq_ref, ckv_ref, tab_ref, gq_ref, gkv_ref, wuq_ref, wukv_ref, q_ref, k_ref, v_ref, *, scale):
    cq = cq_ref[...].astype(F32)
    cqn = (cq * lax.rsqrt(jnp.mean(cq * cq, axis=-1, keepdims=True) + EPS) * gq_ref[...]).astype(BF16)
    c = ckv_ref[...].astype(F32)
    ckv = c[:, :KV_LORA]
    ckvn = (ckv * lax.rsqrt(jnp.mean(ckv * ckv, axis=-1, keepdims=True) + EPS) * gkv_ref[...]).astype(BF16)
    tab = tab_ref[...]
    lane = lax.broadcasted_iota(I32, tab.shape, 1)

    def rope(pair):
        pr = pair * tab
        return jnp.where(lane < D_ROPE, pr + pltpu.roll(pr, D_ROPE, 1), 0.0)

    kr = rope(c[:, KV_LORA:KV_LORA + LANE]).astype(BF16)
    qf = jnp.dot(cqn, wuq_ref[...], preferred_element_type=F32)
    kvf = jnp.dot(ckvn, wukv_ref[...], preferred_element_type=F32)
    for h in range(HD):
        b0 = h * 2 * LANE
        q_ref[:, b0:b0 + LANE] = (qf[:, b0:b0 + LANE] * scale).astype(BF16)
        q_ref[:, b0 + LANE:b0 + 2 * LANE] = (rope(qf[:, b0 + LANE:b0 + 2 * LANE]) * scale).astype(BF16)
        k_ref[:, b0:b0 + LANE] = kvf[:, b0:b0 + LANE].astype(BF16)
        k_ref[:, b0 + LANE:b0 + 2 * LANE] = kr
        v_ref[:, h * LANE:(h + 1) * LANE] = kvf[:, b0 + LANE:b0 + 2 * LANE].astype(BF16)


def _mla_prep(rest, tab, gq, gkv, wuq, wukv, seq):
    n = rest.shape[0]
    tm = min(512, seq)
    scale = (D_NOPE + D_ROPE) ** -0.5 * LOG2E
    kern = functools.partial(_mla_prep_kernel, scale=scale)
    wq = HD * 2 * LANE
    return pl.pallas_call(
        kern,
        grid=(n // tm,),
        in_specs=[
            pl.BlockSpec((tm, Q_LORA), lambda i: (i, R_CQ // Q_LORA)),
            pl.BlockSpec((tm, 512), lambda i: (i, R_CKV // 512)),
            pl.BlockSpec((tm, LANE), lambda i: (i, 0)),
            pl.BlockSpec((1, Q_LORA), lambda i: (0, 0)),
            pl.BlockSpec((1, KV_LORA), lambda i: (0, 0)),
            pl.BlockSpec((Q_LORA, wq), lambda i: (0, 0)),
            pl.BlockSpec((KV_LORA, wq), lambda i: (0, 0)),
        ],
        out_specs=[
            pl.BlockSpec((tm, wq), lambda i: (i, 0)),
            pl.BlockSpec((tm, wq), lambda i: (i, 0)),
            pl.BlockSpec((tm, HD * DV_D), lambda i: (i, 0)),
        ],
        out_shape=[
            jax.ShapeDtypeStruct((n, wq), BF16),
            jax.ShapeDtypeStruct((n, wq), BF16),
            jax.ShapeDtypeStruct((n, HD * DV_D), BF16),
        ],
        compiler_params=_params("parallel"),
    )(rest, rest, tab, gq.reshape(1, -1), gkv.reshape(1, -1), wuq, wukv)


def _mla_attn_kernel(q_ref, k_ref, v_ref, o_ref, m_ref, acc_ref, *, tk, nk):
    ki = pl.program_id(2)

    @pl.when(ki == 0)
    def _():
        m_ref[...] = jnp.full(m_ref.shape, -jnp.inf, F32)
        acc_ref[...] = jnp.zeros(acc_ref.shape, F32)

    ones = jnp.ones((tk, LANE), BF16)
    for h in range(HD):
        q = q_ref[:, h * 2 * LANE:(h + 1) * 2 * LANE]
        k = k_ref[:, h * 2 * LANE:(h + 1) * 2 * LANE]
        vx = jnp.concatenate([v_ref[:, h * LANE:(h + 1) * LANE], ones], axis=1)
        s = lax.dot_general(q, k, NT_DIMS, preferred_element_type=F32)
        _online_softmax_step(s, vx, m_ref, acc_ref, h, tk)

    @pl.when(ki == nk - 1)
    def _():
        for h in range(HD):
            a = acc_ref[h]
            o_ref[:, h * LANE:(h + 1) * LANE] = (a[:, :LANE] / a[:, LANE:]).astype(o_ref.dtype)


def _mla_attn(q, k, v, n_seq, seq):
    n = q.shape[0]
    tq = min(1024, seq)
    tk = min(1024, seq)
    nq, nk = seq // tq, seq // tk
    wq = HD * 2 * LANE
    kern = functools.partial(_mla_attn_kernel, tk=tk, nk=nk)
    return pl.pallas_call(
        kern,
        grid=(n_seq, nq, nk),
        in_specs=[
            pl.BlockSpec((tq, wq), lambda b, qi, ki: (b * nq + qi, 0)),
            pl.BlockSpec((tk, wq), lambda b, qi, ki: (b * nk + ki, 0)),
            pl.BlockSpec((tk, HD * DV_D), lambda b, qi, ki: (b * nk + ki, 0)),
        ],
        out_specs=pl.BlockSpec((tq, HD * DV_D), lambda b, qi, ki: (b * nq + qi, 0)),
        out_shape=jax.ShapeDtypeStruct((n, HD * DV_D), BF16),
        scratch_shapes=[
            pltpu.VMEM((HD, tq, LANE), F32),
            pltpu.VMEM((HD, tq, 2 * LANE), F32),
        ],
        compiler_params=_params("parallel", "parallel", "arbitrary"),
    )(q, k, v)


def _sgu_kernel(uv_ref, g_ref, b_ref, ws_ref, bs_ref, o_ref, *, tm):
    z = jax.nn.gelu(uv_ref[...].astype(F32))
    u = z[:, :BRANCH_W]
    v = z[:, BRANCH_W:]
    mu = jnp.mean(v, axis=-1, keepdims=True)
    vc = v - mu
    var = jnp.mean(vc * vc, axis=-1, keepdims=True)
    vn = (vc * lax.rsqrt(var + EPS) * g_ref[...] + b_ref[...]).astype(BF16)
    for c in range(tm // SGU_CHUNK):
        r0 = c * SGU_CHUNK
        for g in range(SGU_GROUPS):
            c0 = g * LANE
            vm = jnp.dot(ws_ref[g], vn[r0:r0 + SGU_CHUNK, c0:c0 + LANE], preferred_element_type=F32)
            vm = vm + bs_ref[:, c0:c0 + LANE]
            o_ref[r0:r0 + SGU_CHUNK, c0:c0 + LANE] = (u[r0:r0 + SGU_CHUNK, c0:c0 + LANE] * vm).astype(o_ref.dtype)


def _sgu(rest, ln_g, ln_b, ws, bs, seq):
    n = rest.shape[0]
    tm = min(512, seq)
    kern = functools.partial(_sgu_kernel, tm=tm)
    return pl.pallas_call(
        kern,
        grid=(n // tm,),
        in_specs=[
            pl.BlockSpec((tm, 2 * BRANCH_W), lambda i: (i, R_UV // (2 * BRANCH_W))),
            pl.BlockSpec((1, BRANCH_W), lambda i: (0, 0)),
            pl.BlockSpec((1, BRANCH_W), lambda i: (0, 0)),
            pl.BlockSpec((SGU_GROUPS, SGU_CHUNK, SGU_CHUNK), lambda i: (0, 0, 0)),
            pl.BlockSpec((SGU_CHUNK, BRANCH_W), lambda i: (0, 0)),
        ],
        out_specs=pl.BlockSpec((tm, BRANCH_W), lambda i: (i, 0)),
        out_shape=jax.ShapeDtypeStruct((n, BRANCH_W), BF16),
        compiler_params=_params("parallel"),
    )(rest, ln_g.reshape(1, -1), ln_b.reshape(1, -1), ws, bs)


def _lru_kernel(xf_ref, xfp_ref, xfn_ref, xr_ref, xrp_ref, xrn_ref, cw_ref, cb_ref, wr_ref, br_ref, wi_ref,
                bi_ref, lam_ref, hf_ref, hr_ref, af_ref, uf_ref, ar_ref, ur_ref, hc_ref, *, tt, tps):
    j = pl.program_id(0) % tps
    rowi = lax.broadcasted_iota(I32, (tt, LRU_W), 0)

    def gates(x_ref, xp_ref, xn_ref, at_start, at_end, d, a_ref, u_ref):
        x = x_ref[...].astype(F32)
        prev = jnp.where(at_start, 0.0, xp_ref[...].astype(F32)[HALO - 1:HALO, :])
        nxt = jnp.where(at_end, 0.0, xn_ref[...].astype(F32)[0:2, :])
        xm1 = jnp.where(rowi == 0, prev, pltpu.roll(x, 1, 0))
        xp1 = jnp.where(rowi == tt - 1, nxt[0:1, :], pltpu.roll(x, tt - 1, 0))
        xp2 = jnp.where(rowi == tt - 2, nxt[0:1, :],
                        jnp.where(rowi == tt - 1, nxt[1:2, :], pltpu.roll(x, tt - 2, 0)))
        xc = xm1 * cw_ref[0:1, :] + x * cw_ref[1:2, :] + xp1 * cw_ref[2:3, :] + xp2 * cw_ref[3:4, :] + cb_ref[...]
        xcb = xc.astype(BF16)

        def blockdiag(w_ref):
            return jnp.concatenate(
                [jnp.dot(xcb[:, g * LRU_BW:(g + 1) * LRU_BW], w_ref[d, g], preferred_element_type=F32)
                 for g in range(LRU_BLOCKS)], axis=1)

        r = jax.nn.sigmoid(blockdiag(wr_ref) + br_ref[d:d + 1, :])
        ig = jax.nn.sigmoid(blockdiag(wi_ref) + bi_ref[d:d + 1, :])
        log_a = (-LRU_C * r) * jax.nn.softplus(-lam_ref[d:d + 1, :])
        a = jnp.exp(log_a)
        u = jnp.sqrt(jnp.tanh(-log_a) * (1.0 + a * a)) * (ig * xc)
        a_ref[...] = a
        u_ref[...] = u

    gates(xf_ref, xfp_ref, xfn_ref, j == 0, j == tps - 1, 0, af_ref, uf_ref)
    gates(xr_ref, xrp_ref, xrn_ref, j == tps - 1, j == 0, 1, ar_ref, ur_ref)

    @pl.when(j == 0)
    def _():
        hc_ref[...] = jnp.zeros(hc_ref.shape, F32)

    ng = tt // SUBLANE
    sub = lax.broadcasted_iota(I32, (SUBLANE, LRU_W), 0)

    def body(g, carry):
        hf_prev, hr_prev = carry
        r0 = pl.multiple_of(g * SUBLANE, SUBLANE)
        a = af_ref[pl.ds(r0, SUBLANE), :]
        u = uf_ref[pl.ds(r0, SUBLANE), :]
        for s in (1, 2, 4):
            a_s = jnp.where(sub >= s, pltpu.roll(a, s, 0), 1.0)
            u_s = jnp.where(sub >= s, pltpu.roll(u, s, 0), 0.0)
            u = a * u_s + u
            a = a * a_s
        h = a * hf_prev + u
        hf_ref[pl.ds(r0, SUBLANE), :] = h
        hf_new = jnp.broadcast_to(h[SUBLANE - 1:SUBLANE, :], (SUBLANE, LRU_W))

        r1 = pl.multiple_of((ng - 1 - g) * SUBLANE, SUBLANE)
        a = ar_ref[pl.ds(r1, SUBLANE), :]
        u = ur_ref[pl.ds(r1, SUBLANE), :]
        for s in (1, 2, 4):
            a_s = jnp.where(sub < SUBLANE - s, pltpu.roll(a, SUBLANE - s, 0), 1.0)
            u_s = jnp.where(sub < SUBLANE - s, pltpu.roll(u, SUBLANE - s, 0), 0.0)
            u = a * u_s + u
            a = a * a_s
        h = a * hr_prev + u
        hr_ref[pl.ds(r1, SUBLANE), :] = h
        hr_new = jnp.broadcast_to(h[0:1, :], (SUBLANE, LRU_W))
        return hf_new, hr_new

    hf_c, hr_c = lax.fori_loop(0, ng, body, (hc_ref[0], hc_ref[1]))
    hc_ref[0] = hf_c
    hc_ref[1] = hr_c


def _lru(rest, cw, cb, wr, br, wi, bi, lam, seq):
    n = rest.shape[0]
    tt = min(512, seq)
    tps = seq // tt
    nt = n // tt
    t8 = tt // HALO
    last8 = n // HALO - 1
    cblk = R_XC // LRU_W

    def rev(i):
        return (i // tps) * tps + (tps - 1 - i % tps)

    kern = functools.partial(_lru_kernel, tt=tt, tps=tps)
    full = lambda shape: pl.BlockSpec(shape, lambda i: (0,) * len(shape))
    return pl.pallas_call(
        kern,
        grid=(nt,),
        in_specs=[
            pl.BlockSpec((tt, LRU_W), lambda i: (i, cblk)),
            pl.BlockSpec((HALO, LRU_W), lambda i: (jnp.maximum(i * t8 - 1, 0), cblk)),
            pl.BlockSpec((HALO, LRU_W), lambda i: (jnp.minimum((i + 1) * t8, last8), cblk)),
            pl.BlockSpec((tt, LRU_W), lambda i: (rev(i), cblk)),
            pl.BlockSpec((HALO, LRU_W), lambda i: (jnp.maximum(rev(i) * t8 - 1, 0), cblk)),
            pl.BlockSpec((HALO, LRU_W), lambda i: (jnp.minimum((rev(i) + 1) * t8, last8), cblk)),
            full((4, LRU_W)),
            full((1, LRU_W)),
            full((2, LRU_BLOCKS, LRU_BW, LRU_BW)),
            full((2, LRU_W)),
            full((2, LRU_BLOCKS, LRU_BW, LRU_BW)),
            full((2, LRU_W)),
            full((2, LRU_W)),
        ],
        out_specs=[
            pl.BlockSpec((tt, LRU_W), lambda i: (i, 0)),
            pl.BlockSpec((tt, LRU_W), lambda i: (rev(i), 0)),
        ],
        out_shape=[jax.ShapeDtypeStruct((n, LRU_W), F32), jax.ShapeDtypeStruct((n, LRU_W), F32)],
        scratch_shapes=[pltpu.VMEM((tt, LRU_W), F32)] * 4 + [pltpu.VMEM((2, SUBLANE, LRU_W), F32)],
        compiler_params=_params("arbitrary"),
    )(rest, rest, rest, rest, rest, rest, cw, cb.reshape(1, -1), wr, br, wi, bi, lam)


def _merge_kernel(ya_ref, yb_ref, hf_ref, hr_ref, gate_ref, yd_ref, wbr_ref, g0_ref, g1_ref, g2_ref, g3_ref, o_ref):
    yc = (jax.nn.gelu(gate_ref[...].astype(F32)) * (hf_ref[...] + hr_ref[...])).astype(BF16)
    ys = (ya_ref[...], yb_ref[...], yc, yd_ref[...])
    gls = (g0_ref, g1_ref, g2_ref, g3_ref)
    acc = None
    for k in range(N_BRANCH):
        t = jax.nn.sigmoid(gls[k][...].astype(F32)) * jnp.dot(ys[k], wbr_ref[k], preferred_element_type=F32)
        acc = t if acc is None else acc + t
    o_ref[...] = acc.astype(o_ref.dtype)


def _merge(ya, yb, hf, hr, rest, yd, wbr, seq):
    n = ya.shape[0]
    d = wbr.shape[-1]
    tm = min(512, seq)
    tn = 512
    row = lambda w: pl.BlockSpec((tm, w), lambda i, j: (i, 0))

    def gl_spec(k):
        base = (R_GL + k * d) // tn
        return pl.BlockSpec((tm, tn), lambda i, j: (i, base + j))

    return pl.pallas_call(
        _merge_kernel,
        grid=(n // tm, d // tn),
        in_specs=[
            row(BRANCH_W), row(BRANCH_W), row(LRU_W), row(LRU_W),
            pl.BlockSpec((tm, LRU_W), lambda i, j: (i, R_GATE // LRU_W)),
            row(BRANCH_W),
            pl.BlockSpec((N_BRANCH, BRANCH_W, tn), lambda i, j: (0, 0, j)),
            gl_spec(0), gl_spec(1), gl_spec(2), gl_spec(3),
        ],
        out_specs=pl.BlockSpec((tm, tn), lambda i, j: (i, j)),
        out_shape=jax.ShapeDtypeStruct((n, d), BF16),
        compiler_params=_params("parallel", "parallel"),
    )(ya, yb, hf, hr, rest, yd, wbr, rest, rest, rest, rest)


def _outproj_kernel(m_ref, w_ref, x_ref, g_ref, o_ref):
    o_ref[...] = x_ref[...] + g_ref[...] * jnp.dot(m_ref[...], w_ref[...], preferred_element_type=F32)


def _outproj(merged, w, x, g1, seq):
    n, d = x.shape
    tm = min(512, seq)
    tps = seq // tm
    return pl.pallas_call(
        _outproj_kernel,
        grid=(n // tm,),
        in_specs=[
            pl.BlockSpec((tm, d), lambda i: (i, 0)),
            pl.BlockSpec((d, d), lambda i: (0, 0)),
            pl.BlockSpec((tm, d), lambda i: (i, 0)),
            pl.BlockSpec((None, 1, d), lambda i: (i // tps, 0, 0)),
        ],
        out_specs=pl.BlockSpec((tm, d), lambda i: (i, 0)),
        out_shape=jax.ShapeDtypeStruct((n, d), F32),
        compiler_params=_params("parallel"),
    )(merged, w, x, g1)


def _router_kernel(x_ref, g_ref, mod_ref, wh_ref, wl_ref, rb_ref, tri_ref, g2_ref, swg_ref, swu_ref, swd_ref,
                   x1_ref, h2g_ref, e_ref, w_ref, r_ref, cnt_ref, run_ref, *, tm, d):
    @pl.when(pl.program_id(0) == 0)
    def _():
        run_ref[...] = jnp.zeros(run_ref.shape, F32)

    x = x_ref[...]
    y = x * lax.rsqrt(jnp.mean(x * x, axis=-1, keepdims=True) + EPS) * g_ref[...]
    h = y * (1.0 + mod_ref[0:1, :]) + mod_ref[1:2, :]
    hb = h.astype(BF16)
    hs = (jax.nn.silu(jnp.dot(hb, swg_ref[...], preferred_element_type=F32))
          * jnp.dot(hb, swu_ref[...], preferred_element_type=F32)).astype(BF16)
    x1_ref[...] = x + g2_ref[...] * jnp.dot(hs, swd_ref[...], preferred_element_type=F32)
    slabs = d // LANE
    for s in range(slabs):
        h2g_ref[pl.ds(s, tm, stride=slabs), :] = h[:, s * LANE:(s + 1) * LANE]

    hl = (h - hb.astype(F32)).astype(BF16)
    logits = (lax.dot_general(wh_ref[...], hb, NT_DIMS, preferred_element_type=F32)
              + lax.dot_general(wl_ref[...], hb, NT_DIMS, preferred_element_type=F32)
              + lax.dot_general(wh_ref[...], hl, NT_DIMS, preferred_element_type=F32))
    sc = jax.nn.sigmoid(logits)
    sel = sc + rb_ref[...]

    neg = -jnp.inf
    i8 = lax.broadcasted_iota(I32, (GROUP_SZ, tm), 0).astype(F32)
    rows = []
    for g in range(N_GROUPS):
        blk = sel[g * GROUP_SZ:(g + 1) * GROUP_SZ, :]
        m1 = jnp.max(blk, axis=0, keepdims=True)
        i1 = jnp.min(jnp.where(blk == m1, i8, float(GROUP_SZ)), axis=0, keepdims=True)
        m2 = jnp.max(jnp.where(i8 == i1, neg, blk), axis=0, keepdims=True)
        rows.append(m1 + m2)
    gs = jnp.concatenate(rows, axis=0)
    gi8 = lax.broadcasted_iota(I32, (N_GROUPS, tm), 0).astype(F32)
    gsel = jnp.zeros((N_GROUPS, tm), F32)
    for _ in range(TOPK_GROUPS):
        gm = jnp.max(gs, axis=0, keepdims=True)
        gi = jnp.min(jnp.where(gs == gm, gi8, float(N_GROUPS)), axis=0, keepdims=True)
        hit = gi8 == gi
        gsel = jnp.where(hit, 1.0, gsel)
        gs = jnp.where(hit, neg, gs)
    emask = jnp.concatenate(
        [jnp.broadcast_to(gsel[g:g + 1, :], (GROUP_SZ, tm)) for g in range(N_GROUPS)], axis=0)
    selm = jnp.where(emask > 0.0, sel, neg)

    i64 = lax.broadcasted_iota(I32, (N_EXPERTS, tm), 0).astype(F32)
    chosen = jnp.zeros((N_EXPERTS, tm), F32)
    idxs, wts = [], []
    for _ in range(TOP_K):
        mx = jnp.max(selm, axis=0, keepdims=True)
        ix = jnp.min(jnp.where(selm == mx, i64, float(N_EXPERTS)), axis=0, keepdims=True)
        oh = i64 == ix
        wts.append(jnp.sum(jnp.where(oh, sc, 0.0), axis=0, keepdims=True))
        idxs.append(ix)
        selm = jnp.where(oh, neg, selm)
        chosen = jnp.where(oh, 1.0, chosen)
    wk = jnp.concatenate(wts, axis=0)
    w_ref[...] = wk / jnp.sum(wk, axis=0, keepdims=True) * ROUTE_SCALE
    e_ref[...] = jnp.concatenate(idxs, axis=0).astype(I32)

    before = jnp.dot(chosen.astype(BF16), tri_ref[...], preferred_element_type=F32)
    rank = run_ref[...] + before
    r_ref[...] = jnp.concatenate(
        [jnp.sum(jnp.where(i64 == idxs[k], rank, 0.0), axis=0, keepdims=True) for k in range(TOP_K)],
        axis=0).astype(I32)
    run_ref[...] = run_ref[...] + jnp.sum(chosen, axis=1, keepdims=True)
    cnt_ref[...] = jnp.broadcast_to(run_ref[...], cnt_ref.shape).astype(I32)


def _router(x, g, mod, wh, wl, rb, g2, swg, swu, swd, seq):
    n, d = x.shape
    ds_ = swg.shape[1]
    tm = min(512, seq)
    tps = seq // tm
    slabs = d // LANE
    tri = jnp.triu(jnp.ones((tm, tm), F32), 1).astype(BF16)
    kern = functools.partial(_router_kernel, tm=tm, d=d)
    return pl.pallas_call(
        kern,
        grid=(n // tm,),
        in_specs=[
            pl.BlockSpec((tm, d), lambda i: (i, 0)),
            pl.BlockSpec((1, d), lambda i: (0, 0)),
            pl.BlockSpec((None, 2, d), lambda i: (i // tps, 0, 0)),
            pl.BlockSpec((N_EXPERTS, d), lambda i: (0, 0)),
            pl.BlockSpec((N_EXPERTS, d), lambda i: (0, 0)),
            pl.BlockSpec((N_EXPERTS, 1), lambda i: (0, 0)),
            pl.BlockSpec((tm, tm), lambda i: (0, 0)),
            pl.BlockSpec((None, 1, d), lambda i: (i // tps, 0, 0)),
            pl.BlockSpec((d, ds_), lambda i: (0, 0)),
            pl.BlockSpec((d, ds_), lambda i: (0, 0)),
            pl.BlockSpec((ds_, d), lambda i: (0, 0)),
        ],
        out_specs=[
            pl.BlockSpec((tm, d), lambda i: (i, 0)),
            pl.BlockSpec((tm * slabs, LANE), lambda i: (i, 0)),
            pl.BlockSpec((TOP_K, tm), lambda i: (0, i)),
            pl.BlockSpec((TOP_K, tm), lambda i: (0, i)),
            pl.BlockSpec((TOP_K, tm), lambda i: (0, i)),
            pl.BlockSpec((N_EXPERTS, LANE), lambda i: (0, 0)),
        ],
        out_shape=[
            jax.ShapeDtypeStruct((n, d), F32),
            jax.ShapeDtypeStruct((n * slabs, LANE), F32),
            jax.ShapeDtypeStruct((TOP_K, n), I32),
            jax.ShapeDtypeStruct((TOP_K, n), F32),
            jax.ShapeDtypeStruct((TOP_K, n), I32),
            jax.ShapeDtypeStruct((N_EXPERTS, LANE), I32),
        ],
        scratch_shapes=[pltpu.VMEM((N_EXPERTS, 1), F32)],
        compiler_params=_params("arbitrary"),
    )(x, g.reshape(1, d), mod, wh, wl, rb.reshape(N_EXPERTS, 1), tri, g2, swg, swu, swd)


def _dispatch_kernel(dest_ref, pend_ref, pc_ref, h_ref, xs_ref, zero_ref, sem, zsem, *, tm, n, slabs):
    i = pl.program_id(0)
    rows_blk = MOE_BLK * slabs

    def tail_copy(e):
        start = pl.multiple_of((pend_ref[e] - MOE_BLK) * slabs, rows_blk)
        return pltpu.make_async_copy(zero_ref, xs_ref.at[pl.ds(start, rows_blk), :], zsem)

    @pl.when(i == 0)
    def _():
        zero_ref[...] = jnp.zeros(zero_ref.shape, F32)

        def zstart(e, c):
            @pl.when(pc_ref[e] > 0)
            def _():
                tail_copy(e).start()
            return c

        def zwait(e, c):
            @pl.when(pc_ref[e] > 0)
            def _():
                tail_copy(e).wait()
            return c

        lax.fori_loop(0, N_EXPERTS, zstart, 0)
        lax.fori_loop(0, N_EXPERTS, zwait, 0)

    def row_copy(r, k):
        src = h_ref.at[pl.ds(pl.multiple_of(r * slabs, slabs), slabs), :]
        dst_row = dest_ref[k * n + i * tm + r]
        dst = xs_ref.at[pl.ds(pl.multiple_of(dst_row * slabs, slabs), slabs), :]
        return pltpu.make_async_copy(src, dst, sem)

    def start(r, c):
        for k in range(TOP_K):
            row_copy(r, k).start(priority=k % 2)
        return c

    lax.fori_loop(0, tm, start, 0)
    for _ in range(TOP_K):
        pltpu.make_async_copy(h_ref, xs_ref.at[pl.ds(0, tm * slabs), :], sem).wait()


def _dispatch(dest, pend, pc, h2g, n_rows, seq):
    n = dest.shape[0] // TOP_K
    slabs = h2g.shape[0] // n
    tm = min(256, seq)
    kern = functools.partial(_dispatch_kernel, tm=tm, n=n, slabs=slabs)
    return pl.pallas_call(
        kern,
        grid_spec=pltpu.PrefetchScalarGridSpec(
            num_scalar_prefetch=3,
            grid=(n // tm,),
            in_specs=[pl.BlockSpec((tm * slabs, LANE), lambda i, *_: (i, 0))],
            out_specs=pl.BlockSpec(memory_space=pl.ANY),
            scratch_shapes=[
                pltpu.VMEM((MOE_BLK * slabs, LANE), F32),
                pltpu.SemaphoreType.DMA(()),
                pltpu.SemaphoreType.DMA(()),
            ],
        ),
        out_shape=jax.ShapeDtypeStruct((n_rows * slabs, LANE), F32),
        compiler_params=_params("arbitrary"),
    )(dest, pend, pc, h2g)


def _gmm_kernel(be_ref, nu_ref, x_ref, wg_ref, wu_ref, wd_ref, o_ref, *, slabs):
    @pl.when(pl.program_id(0) < nu_ref[0])
    def _():
        x = jnp.concatenate(
            [x_ref[pl.ds(s, MOE_BLK, stride=slabs), :].astype(BF16) for s in range(slabs)], axis=1)
        hg = jnp.dot(x, wg_ref[...], preferred_element_type=F32)
        hu = jnp.dot(x, wu_ref[...], preferred_element_type=F32)
        hb = (jax.nn.silu(hg) * hu).astype(BF16)
        per = 4
        for c in range(slabs // per):
            y = jnp.dot(hb, wd_ref[:, c * per * LANE:(c + 1) * per * LANE], preferred_element_type=F32)
            for j in range(per):
                o_ref[pl.ds(c * per + j, MOE_BLK, stride=slabs), :] = y[:, j * LANE:(j + 1) * LANE]


def _gmm(blk_e, nused, xs, wg, wu, wd, layer):
    _, _, d, de = wg.shape
    slabs = d // LANE
    nblk = xs.shape[0] // (MOE_BLK * slabs)
    kern = functools.partial(_gmm_kernel, slabs=slabs)

    def blk(b, be, nu):
        return jnp.minimum(b, nu[0] - 1)

    return pl.pallas_call(
        kern,
        grid_spec=pltpu.PrefetchScalarGridSpec(
            num_scalar_prefetch=2,
            grid=(nblk,),
            in_specs=[
                pl.BlockSpec((MOE_BLK * slabs, LANE), lambda b, be, nu: (blk(b, be, nu), 0)),
                pl.BlockSpec((None, None, d, de), lambda b, be, nu: (layer, be[blk(b, be, nu)], 0, 0)),
                pl.BlockSpec((None, None, d, de), lambda b, be, nu: (layer, be[blk(b, be, nu)], 0, 0)),
                pl.BlockSpec((None, None, de, d), lambda b, be, nu: (layer, be[blk(b, be, nu)], 0, 0)),
            ],
            out_specs=pl.BlockSpec((MOE_BLK * slabs, LANE), lambda b, be, nu: (blk(b, be, nu), 0)),
        ),
        out_shape=jax.ShapeDtypeStruct(xs.shape, F32),
        compiler_params=_params("arbitrary"),
    )(blk_e, nused, xs, wg, wu, wd)


def _combine_kernel(dest_ref, wt_ref, x_ref, g2_ref, fg_ref, ys_ref, o_ref, buf_ref, wb_ref, sem,
                    *, tm, n, slabs, nt, final):
    i = pl.program_id(0)
    slot = i % 2
    rows = tm * COMBINE_PITCH

    def gather(tile, sl):
        def start(r, c):
            for k in range(TOP_K):
                src_row = dest_ref[k * n + tile * tm + r]
                src = ys_ref.at[pl.ds(pl.multiple_of(src_row * slabs, slabs), slabs), :]
                dst = buf_ref.at[sl, pl.ds(pl.multiple_of(k * rows + r * COMBINE_PITCH, SUBLANE), slabs), :]
                pltpu.make_async_copy(src, dst, sem.at[sl]).start(priority=k % 2)
            return c

        lax.fori_loop(0, tm, start, 0)

    @pl.when(i == 0)
    def _():
        gather(0, 0)

    @pl.when(i + 1 < nt)
    def _():
        gather(i + 1, 1 - slot)

    moved = TOP_K * tm * slabs
    pltpu.make_async_copy(ys_ref.at[pl.ds(0, moved), :], buf_ref.at[slot, pl.ds(0, moved), :], sem.at[slot]).wait()

    wt = wt_ref[...]
    for k in range(TOP_K):
        wb_ref[k] = jnp.broadcast_to(wt[:, k:k + 1], (tm, LANE))
    for s in range(slabs):
        c0 = s * LANE
        routed = None
        for k in range(TOP_K):
            t = wb_ref[k] * buf_ref[slot, pl.ds(k * rows + s, tm, stride=COMBINE_PITCH), :]
            routed = t if routed is None else routed + t
        o_ref[:, c0:c0 + LANE] = x_ref[:, c0:c0 + LANE] + g2_ref[:, c0:c0 + LANE] * routed
    if final:
        o = o_ref[...]
        o_ref[...] = o * lax.rsqrt(jnp.mean(o * o, axis=-1, keepdims=True) + EPS) * fg_ref[...]


def _combine(dest, wt, x, g2, ys, final_g, final, seq):
    n, d = x.shape
    slabs = d // LANE
    tm = min(128, seq)
    tps = seq // tm
    assert COMBINE_PITCH >= slabs
    kern = functools.partial(_combine_kernel, tm=tm, n=n, slabs=slabs, nt=n // tm, final=final)
    return pl.pallas_call(
        kern,
        grid_spec=pltpu.PrefetchScalarGridSpec(
            num_scalar_prefetch=1,
            grid=(n // tm,),
            in_specs=[
                pl.BlockSpec((tm, TOP_K), lambda i, *_: (i, 0)),
                pl.BlockSpec((tm, d), lambda i, *_: (i, 0)),
                pl.BlockSpec((None, 1, d), lambda i, *_: (i // tps, 0, 0)),
                pl.BlockSpec((1, d), lambda i, *_: (0, 0)),
                pl.BlockSpec(memory_space=pl.ANY),
            ],
            out_specs=pl.BlockSpec((tm, d), lambda i, *_: (i, 0)),
            scratch_shapes=[
                pltpu.VMEM((2, TOP_K * tm * COMBINE_PITCH, LANE), F32),
                pltpu.VMEM((TOP_K, tm, LANE), F32),
                pltpu.SemaphoreType.DMA((2,)),
            ],
        ),
        out_shape=jax.ShapeDtypeStruct((n, d), F32),
        compiler_params=_params("arbitrary"),
    )(dest, wt, x, g2, final_g.reshape(1, d), ys)


def _swap_halves(w):
    half = w.shape[-1] // 2
    return jnp.concatenate([w[..., half:], w[..., :half]], axis=-1)


def _prep_w_in(w):
    depth, d, _ = w.shape
    sizes = [HA * 2 * DHA, HA * 2 * DHA, HA * DVA, 2 * BRANCH_W, 2 * LRU_W, Q_LORA, KV_LORA, D_ROPE, N_BRANCH * d]
    offs = np.cumsum([0] + sizes)
    qa, ka, va, uv, xg, cq, ckv, kr, gl = [w[:, :, offs[i]:offs[i + 1]] for i in range(len(sizes))]
    pad = jnp.zeros((depth, d, 512 - KV_LORA - 2 * D_ROPE), w.dtype)
    w_att = jnp.concatenate([qa * (DHA ** -0.5 * LOG2E), ka, va], axis=2).astype(BF16)
    w_rest = jnp.concatenate([uv, xg, cq, ckv, kr, _swap_halves(kr), pad, gl], axis=2).astype(BF16)
    return w_att, w_rest


def _prep_layer(l, p):
    wuq =p['w_uq'][l].reshape(Q_LORA, HD, D_NOPE + D_ROPE)
    wuq_r = wuq[:, :, D_NOPE:]
    wuq_ext = jnp.concatenate([wuq[:, :, :D_NOPE], wuq_r, _swap_halves(wuq_r)], axis=-1)
    wuq_ext = wuq_ext.reshape(Q_LORA, HD * 2 * LANE).astype(BF16)

    lam_init = 0.8 - 0.6 * math.exp(-0.3 * l)
    lam = (jnp.exp(jnp.sum(p['lam_q1'][l] * p['lam_k1'][l])) - jnp.exp(jnp.sum(p['lam_q2'][l] * p['lam_k2'][l]))
           + lam_init)
    slopes = jnp.asarray(2.0 ** (-8.0 * np.arange(1, HA + 1) / HA) * LOG2E, F32)
    par = jnp.concatenate([slopes, lam.reshape(1), jnp.full((1,), 1.0 - lam_init, F32), jnp.zeros((2,), F32)])

    rw = p['router_w'][l].T
    rw_hi = rw.astype(BF16)
    rw_lo = (rw - rw_hi.astype(F32)).astype(BF16)
    bs = jnp.repeat(p['sgu_b'][l].T, LANE, axis=1)
    return dict(
        wuq=wuq_ext, wukv=p['w_ukv'][l].astype(BF16), par=par,
        sgu_w=p['sgu_w'][l].astype(BF16), sgu_bs=bs,
        rg_wr=p['rg_wr'][l].astype(BF16), rg_wi=p['rg_wi'][l].astype(BF16),
        w_branch=p['w_branch'][l].astype(BF16), w_out=p['w_out'][l].astype(BF16),
        rw_hi=rw_hi, rw_lo=rw_lo,
        sh_wg=p['sh_wg'][l].astype(BF16), sh_wu=p['sh_wu'][l].astype(BF16), sh_wd=p['sh_wd'][l].astype(BF16),
    )


def _rope_table(n_seq, seq):
    inv_freq = ROPE_THETA ** (-jnp.arange(0, D_ROPE, 2, dtype=F32) / D_ROPE)
    ang = jnp.arange(seq, dtype=F32)[:, None] * inv_freq[None, :]
    cos, sin = jnp.cos(ang), jnp.sin(ang)
    tab = jnp.concatenate([cos, cos, -sin, sin], axis=1)
    return jnp.tile(tab, (n_seq, 1))


def _trunk(x3, mods, p, preps):
    n_seq, seq, d = x3.shape
    n = n_seq * seq
    x = x3.reshape(n, d)
    tab = _rope_table(n_seq, seq)
    depth = len(preps)
    for l in range(depth):
        w = preps[l]
        mod = mods[l]
        mod1 = jnp.stack([mod[:, 1], mod[:, 0]], axis=1)
        mod2 = jnp.stack([mod[:, 4], mod[:, 3]], axis=1)
        g1 = mod[:, 2:3]
        g2 = mod[:, 5:6]

        att = _inproj(x, p['norm1_g'][l], mod1, p['w_att'], l, BF16, seq)
        rest = _inproj(x, p['norm1_g'][l], mod1, p['w_rest'], l, BF16, seq)
        ya = _diff_attn(att, w['par'], p['subln_g'][l], n_seq, seq)
        yb = _sgu(rest, p['sgu_ln_g'][l], p['sgu_ln_b'][l], w['sgu_w'], w['sgu_bs'], seq)
        hf, hr = _lru(rest, p['conv_w'][l], p['conv_b'][l], w['rg_wr'], p['rg_br'][l], w['rg_wi'],
                      p['rg_bi'][l], p['rg_lam'][l], seq)
        q, k, v = _mla_prep(rest, tab, p['q_norm_g'][l], p['kv_norm_g'][l], w['wuq'], w['wukv'], seq)
        yd = _mla_attn(q, k, v, n_seq, seq)
        merged = _merge(ya, yb, hf, hr, rest, yd, w['w_branch'], seq)
        x = _outproj(merged, w['w_out'], x, g1, seq)

        x, h2g, eidx, wgt, rank, cnt = _router(x, p['norm2_g'][l], mod2, w['rw_hi'], w['rw_lo'],
                                               p['router_bias'][l], g2, w['sh_wg'], w['sh_wu'], w['sh_wd'], seq)
        counts = cnt[:, 0]
        pc = ((counts + MOE_BLK - 1) // MOE_BLK) * MOE_BLK
        pend = jnp.cumsum(pc).astype(I32)
        pstart = pend - pc
        eid = jnp.arange(N_EXPERTS, dtype=I32)
        dest = (jnp.sum(jnp.where(eidx[..., None] == eid, pstart, 0), axis=-1) + rank).reshape(-1).astype(I32)
        nblk = n * TOP_K // MOE_BLK + N_EXPERTS
        blk_row = jnp.arange(nblk, dtype=I32)[:, None] * MOE_BLK
        blk_e = jnp.minimum(jnp.sum((pend[None, :] <= blk_row).astype(I32), axis=1), N_EXPERTS - 1)
        nused = (pend[-1:] // MOE_BLK).astype(I32)
        xs = _dispatch(dest, pend, pc.astype(I32), h2g, nblk * MOE_BLK, seq)
        ys = _gmm(blk_e, nused, xs, p['exp_wg_bf16'], p['exp_wu_bf16'], p['exp_wd_bf16'], l)
        x = _combine(dest, wgt.T, x, g2, ys, p['final_g'], l == depth - 1, seq)
    return x.reshape(n_seq, seq, d)


def kernel(x_prompt, x_sample, c_prompt, c_sample, ada_w, ada_b, norm1_g, norm2_g, w_in, lam_q1, lam_k1, lam_q2, lam_k2, subln_g, sgu_ln_g, sgu_ln_b, sgu_w, sgu_b, conv_w, conv_b, rg_wr, rg_br, rg_wi, rg_bi, rg_lam, q_norm_g, kv_norm_g, w_uq, w_ukv, w_branch, w_out, router_w, router_bias, exp_wg, exp_wu, exp_wd, sh_wg, sh_wu, sh_wd, final_g):
    p = dict(ada_w=ada_w, ada_b=ada_b, norm1_g=norm1_g, norm2_g=norm2_g, w_in=w_in,
             lam_q1=lam_q1, lam_k1=lam_k1, lam_q2=lam_q2, lam_k2=lam_k2, subln_g=subln_g,
             sgu_ln_g=sgu_ln_g, sgu_ln_b=sgu_ln_b, sgu_w=sgu_w, sgu_b=sgu_b,
             conv_w=conv_w, conv_b=conv_b, rg_wr=rg_wr, rg_br=rg_br, rg_wi=rg_wi, rg_bi=rg_bi,
             rg_lam=rg_lam, q_norm_g=q_norm_g, kv_norm_g=kv_norm_g, w_uq=w_uq, w_ukv=w_ukv,
             w_branch=w_branch, w_out=w_out, router_w=router_w, router_bias=router_bias,
             exp_wg=exp_wg, exp_wu=exp_wu, exp_wd=exp_wd, sh_wg=sh_wg, sh_wu=sh_wu, sh_wd=sh_wd,
             final_g=final_g)
    depth, d, _ = ada_w.shape
    bp, bs = x_prompt.shape[0], x_sample.shape[0]
    assert bp + bs <= 8
    c8 = jnp.concatenate([c_prompt, c_sample, jnp.zeros((8 - bp - bs, d), F32)], axis=0)
    mod = _ada_mod(c8, ada_w, ada_b).reshape(depth, 8, 6, d)
    p['w_att'], p['w_rest'] = _prep_w_in(w_in)
    p.update(exp_wg_bf16=exp_wg.astype(BF16), exp_wu_bf16=exp_wu.astype(BF16), exp_wd_bf16=exp_wd.astype(BF16))
    preps = [_prep_layer(l, p) for l in range(depth)]
    y_prompt = _trunk(x_prompt, mod[:, :bp], p, preps)
    y_sample = _trunk(x_sample, mod[:, bp:bp + bs], p, preps)
    return (y_prompt, y_sample)
```
